```python
import math
import jax
import jax.numpy as jnp
from jax import lax
import numpy as np

D_MODEL = 1024
BATCH = 4
SEQ = 4096
DEPTH = 4
DEC_BATCH = 32
DEC_SEQ = 4
PAST_LEN = 8192
PAGE_SIZE = 128

N_A_LAYERS = DEPTH // 2
N_B_LAYERS = DEPTH - N_A_LAYERS
N_HEADS = 16
KV_HEADS = 4
GROUP = N_HEADS // KV_HEADS
HEAD_DIM = 64
CONV_W = 31
D_FF = -(-8 * D_MODEL // (3 * 256)) * 256
CMP_BLOCK = 32
SEL_BLOCK = 64
N_SEL = 16
WINDOW = 512
PHI_HIDDEN = 128
N_BUCKETS = 32
MAX_DISTANCE = 1024
Q_BLOCK = 128
EPS = 1e-6
NEG = -1e30
FORCE = 1e4

kernel_name = 'yoco_conformer_nsa_decoder_step'


def rmsnorm(x, g):
    x32 = x.astype(jnp.float32)
    y = x32 * lax.rsqrt(jnp.mean(x32 * x32, axis=-1, keepdims=True) + EPS) * g.astype(jnp.float32)
    return y.astype(x.dtype)


def layernorm(x, g, b):
    x32 = x.astype(jnp.float32)
    mu = jnp.mean(x32, axis=-1, keepdims=True)
    var = jnp.mean(jnp.square(x32 - mu), axis=-1, keepdims=True)
    y = (x32 - mu) * lax.rsqrt(var + EPS) * g.astype(jnp.float32) + b.astype(jnp.float32)
    return y.astype(x.dtype)


def ada_modulation(c, w, b):
    m = jax.nn.silu(c) @ w + b
    shift, scale, gate = jnp.split(m, 3, axis=-1)
    return shift[:, None, :], scale[:, None, :], gate[:, None, :]


def swiglu(h, w_gate, w_up, w_down):
    return (jax.nn.silu(h @ w_gate) * (h @ w_up)) @ w_down


def rel_bucket(dist):
    n = jnp.maximum(dist, 0)
    exact = N_BUCKETS // 2
    nf = jnp.maximum(n, 1).astype(jnp.float32)
    large = exact + (jnp.log(nf / exact) / math.log(MAX_DISTANCE / exact) * (N_BUCKETS - exact)).astype(jnp.int32)
    large = jnp.minimum(large, N_BUCKETS - 1)
    return jnp.where(n < exact, n, large)


def rel_bias_2d(table, q_pos, k_pos):
    b = table.astype(jnp.float32)[rel_bucket(q_pos[:, None] - k_pos[None, :])]
    return b.reshape(q_pos.shape[0], k_pos.shape[0], KV_HEADS, GROUP).transpose(2, 3, 0, 1)


def masked_softmax(s, mask, axis):
    s = jnp.where(mask, s, NEG)
    return jax.nn.softmax(s, axis=axis) * mask


def conformer_conv(h, prev, w_pw1, b_pw1, w_dw, b_dw, ln_g, ln_b, w_pw2, b_pw2):
    u = h @ w_pw1 + b_pw1
    a, g = jnp.split(u, 2, axis=-1)
    u = a * jax.nn.sigmoid(g)
    if prev is None:
        prev = jnp.zeros((u.shape[0], CONV_W - 1, u.shape[2]), u.dtype)
    full = jnp.concatenate([prev.astype(u.dtype), u], axis=1)
    y = lax.conv_general_dilated(full, w_dw[:, None, :].astype(u.dtype), (1,), 'VALID',
                                 dimension_numbers=('NWC', 'WIO', 'NWC'),
                                 feature_group_count=u.shape[-1]) + b_dw
    y = jax.nn.silu(layernorm(y, ln_g, ln_b))
    return y @ w_pw2 + b_pw2, full[:, -(CONV_W - 1):]


def shared_kv_rows(x, g_kv, w_kv):
    B, T, _ = x.shape
    return (rmsnorm(x, g_kv) @ w_kv).reshape(B, T, 3, 2, KV_HEADS, HEAD_DIM)


def gather_pages(cache, page_table):
    g = cache[page_table]
    return g.reshape(page_table.shape[0], page_table.shape[1] * PAGE_SIZE, *cache.shape[2:])


def compress(kv_full, w_phi1, b_phi1, w_phi2, b_phi2, pe_cmp):
    B, Tk = kv_full.shape[:2]
    n = Tk // CMP_BLOCK
    blk = kv_full[:, :n * CMP_BLOCK].reshape(B, n, CMP_BLOCK, 2, KV_HEADS, HEAD_DIM)
    blk = blk + pe_cmp.transpose(1, 0, 2)[:, :, None, :]
    blk = blk.transpose(0, 1, 3, 4, 2, 5).reshape(B, n, 2, KV_HEADS, CMP_BLOCK * HEAD_DIM)
    hid = jax.nn.silu(jnp.einsum('bnegf,efh->bnegh', blk, w_phi1) + b_phi1[:, None, :])
    out = jnp.einsum('bnegh,ehd->bnegd', hid, w_phi2) + b_phi2[:, None, :]
    end = jnp.arange(n) * CMP_BLOCK + CMP_BLOCK - 1
    return out[:, :, 0], out[:, :, 1], end


def selection_blocks(kv_full):
    B, Tk = kv_full.shape[:2]
    n_blk = -(-Tk // SEL_BLOCK)
    kv = jnp.pad(kv_full, ((0, 0), (0, n_blk * SEL_BLOCK - Tk), (0, 0), (0, 0), (0, 0)))
    kv = kv.reshape(B, n_blk, SEL_BLOCK, 2, KV_HEADS, HEAD_DIM).transpose(3, 0, 4, 1, 2, 5)
    return kv[0], kv[1]


def nsa_chunk(q, gates, q_pos, cmp_k, cmp_v, cmp_end, blk_k, blk_v, win_k, win_v, win_pos, table):
    B, Q = q.shape[:2]
    s_c = jnp.einsum('bqgrd,bngd->bgrqn', q, cmp_k).astype(jnp.float32) + rel_bias_2d(table, q_pos, cmp_end)
    p_c = masked_softmax(s_c, cmp_end[None, :] <= q_pos[:, None], -1)
    o_c = jnp.einsum('bgrqn,bngd->bqgrd', p_c.astype(cmp_v.dtype), cmp_v)
    n_cmp = cmp_end.shape[0]
    n_blk = blk_k.shape[2]
    ratio = SEL_BLOCK // CMP_BLOCK
    imp = jnp.pad(p_c.sum(axis=2), ((0, 0), (0, 0), (0, 0), (0, n_blk * ratio - n_cmp)))
    imp = imp.reshape(B, KV_HEADS, Q, n_blk, ratio).sum(-1)
    j = jnp.arange(n_blk)[None, :]
    jq = (q_pos // SEL_BLOCK)[:, None]
    forced = (j == 0) | (j == jq) | (j == jq - 1)
    score = jnp.where(j <= jq, jnp.where(forced, FORCE, imp), -1.0)
    top_s, idx = lax.top_k(score, min(N_SEL, n_blk))
    bi = jnp.arange(B)[:, None, None, None]
    gi = jnp.arange(KV_HEADS)[None, :, None, None]
    ks = blk_k[bi, gi, idx]
    vs = blk_v[bi, gi, idx]
    kpos = idx[..., None] * SEL_BLOCK + jnp.arange(SEL_BLOCK)
    qp = q_pos[None, None, :, None, None]
    smask = (top_s >= 0)[..., None] & (kpos <= qp)
    table_g = table.astype(jnp.float32).reshape(N_BUCKETS, KV_HEADS, GROUP).transpose(1, 0, 2)
    bias_s = table_g[jnp.arange(KV_HEADS)[None, :, None, None, None], rel_bucket(qp - kpos)]
    s_s = jnp.einsum('bqgrd,bgqksd->bgqksr', q, ks).astype(jnp.float32) + bias_s
    nk = ks.shape[3] * SEL_BLOCK
    p_s = masked_softmax(s_s.reshape(B, KV_HEADS, Q, nk, GROUP), smask.reshape(B, KV_HEADS, Q, nk)[..., None], 3)
    o_s = jnp.einsum('bgqnr,bgqnd->bqgrd', p_s.astype(vs.dtype), vs.reshape(B, KV_HEADS, Q, nk, HEAD_DIM))
    d = q_pos[:, None] - win_pos[None, :]
    wmask = (d >= 0) & (d < WINDOW) & (win_pos[None, :] >= 0)
    s_w = jnp.einsum('bqgrd,bkgd->bgrqk', q, win_k).astype(jnp.float32) + rel_bias_2d(table, q_pos, win_pos)
    p_w = masked_softmax(s_w, wmask, -1)
    o_w = jnp.einsum('bgrqk,bkgd->bqgrd', p_w.astype(win_v.dtype), win_v)
    return gates[..., 0:1] * o_c + gates[..., 1:2] * o_s + gates[..., 2:3] * o_w


def nsa_mixer(h, w_qg, w_o, table, cmp_k, cmp_v, cmp_end, blk_k, blk_v, win_all, q_pos0, is_prompt):
    B, T, _ = h.shape
    qg = h @ w_qg
    q = qg[..., :N_HEADS * HEAD_DIM].reshape(B, T, KV_HEADS, GROUP, HEAD_DIM) * (HEAD_DIM ** -0.5)
    gates = jax.nn.sigmoid(qg[..., N_HEADS * HEAD_DIM:]).reshape(B, T, KV_HEADS, GROUP, 3)
    if is_prompt:
        nb = T // Q_BLOCK
        wpad = jnp.pad(win_all, ((0, 0), (WINDOW, 0), (0, 0), (0, 0), (0, 0)))

        def body(args):
            qc, gc, i = args
            start = i * Q_BLOCK
            q_pos = start + jnp.arange(Q_BLOCK)
            wkv = lax.dynamic_slice_in_dim(wpad, start, Q_BLOCK + WINDOW, axis=1)
            w_pos = start - WINDOW + jnp.arange(Q_BLOCK + WINDOW)
            return nsa_chunk(qc, gc, q_pos, cmp_k, cmp_v, cmp_end, blk_k, blk_v,
                             wkv[:, :, 0], wkv[:, :, 1], w_pos, table)

        qc = q.reshape(B, nb, Q_BLOCK, KV_HEADS, GROUP, HEAD_DIM).swapaxes(0, 1)
        gc = gates.reshape(B, nb, Q_BLOCK, KV_HEADS, GROUP, 3).swapaxes(0, 1)
        o = lax.map(body, (qc, gc, jnp.arange(nb)))
        o = o.swapaxes(0, 1).reshape(B, T, N_HEADS * HEAD_DIM)
    else:
        q_pos = q_pos0 + jnp.arange(T)
        w_pos = q_pos0 + T - win_all.shape[1] + jnp.arange(win_all.shape[1])
        o = nsa_chunk(q, gates, q_pos, cmp_k, cmp_v, cmp_end, blk_k, blk_v,
                      win_all[:, :, 0], win_all[:, :, 1], w_pos, table)
        o = o.reshape(B, T, N_HEADS * HEAD_DIM)
    return o @ w_o


def trunk(x, c, is_prompt, cache_kv_cmp, cache_kv_sel, cache_kv_win, state_conv, page_table, P):
    B, T, _ = x.shape
    conv_states = []
    for l in range(DEPTH):
        if l == N_A_LAYERS:
            rows = shared_kv_rows(x, P['g_kv'], P['w_kv'])
            if is_prompt:
                q_pos0 = 0
                cmp_full = rows[:, :, 0]
                sel_full = rows[:, :, 1]
                win_all = rows[:, :, 2]
                win_state = win_all[:, -min(WINDOW, T):]
            else:
                q_pos0 = page_table.shape[1] * PAGE_SIZE
                cmp_full = jnp.concatenate([gather_pages(cache_kv_cmp, page_table), rows[:, :, 0]], axis=1)
                sel_full = jnp.concatenate([gather_pages(cache_kv_sel, page_table), rows[:, :, 1]], axis=1)
                win_all = jnp.concatenate([cache_kv_win, rows[:, :, 2]], axis=1)
                win_state = win_all[:, -cache_kv_win.shape[1]:]
            cmp_k, cmp_v, cmp_end = compress(cmp_full, P['w_phi1'], P['b_phi1'], P['w_phi2'], P['b_phi2'], P['pe_cmp'])
            blk_k, blk_v = selection_blocks(sel_full)
        shift, scale, gate = ada_modulation(c, P['w_ada'][l, 0], P['b_ada'][l, 0])
        h = rmsnorm(x, P['norm_g'][l, 0]) * (1 + scale) + shift
        if l < N_A_LAYERS:
            prev = None if is_prompt else state_conv[l]
            out, st = conformer_conv(h, prev, P['w_pw1'][l], P['b_pw1'][l], P['w_dw'][l], P['b_dw'][l],
                                     P['ln_g'][l], P['ln_b'][l], P['w_pw2'][l], P['b_pw2'][l])
            conv_states.append(st)
        else:
            lb = l - N_A_LAYERS
            out = nsa_mixer(h, P['w_qg'][lb], P['w_o'][lb], P['rel_table'], cmp_k, cmp_v, cmp_end,
                            blk_k, blk_v, win_all, q_pos0, is_prompt)
        x = x + gate * out
        shift, scale, gate = ada_modulation(c, P['w_ada'][l, 1], P['b_ada'][l, 1])
        h = rmsnorm(x, P['norm_g'][l, 1]) * (1 + scale) + shift
        x = x + gate * swiglu(h, P['w_gate'][l], P['w_up'][l], P['w_down'][l])
    y = rmsnorm(x, P['final_g'])
    return y, rows[:, :, 0], rows[:, :, 1], win_state, jnp.stack(conv_states)


def setup_inputs(seed: int = 0) -> dict:
    key = jax.random.key(seed)
    ks = jax.random.split(key, 48)
    it = iter(range(48))

    def nrm(shape, scale):
        return jax.random.normal(ks[next(it)], shape, jnp.float32) * scale

    n_pages = PAST_LEN // PAGE_SIZE
    n_pool = (DEC_BATCH * n_pages * 5 + 3) // 4
    win_buf = min(WINDOW, PAST_LEN)
    qg_width = N_HEADS * HEAD_DIM + 3 * N_HEADS
    kv_width = 3 * 2 * KV_HEADS * HEAD_DIM
    perm = jax.random.permutation(ks[next(it)], n_pool)
    page_table = perm[:DEC_BATCH * n_pages].reshape(DEC_BATCH, n_pages).astype(jnp.int32)
    return {
        'x_prompt': nrm((BATCH, SEQ, D_MODEL), 1.0),
        'x_sample': nrm((DEC_BATCH, DEC_SEQ, D_MODEL), 1.0),
        'c_prompt': nrm((BATCH, D_MODEL), 1.0),
        'c_sample': nrm((DEC_BATCH, D_MODEL), 1.0),
        'cache_kv_cmp': nrm((n_pool, PAGE_SIZE, 2, KV_HEADS, HEAD_DIM), 1.0),
        'cache_kv_sel': nrm((n_pool, PAGE_SIZE, 2, KV_HEADS, HEAD_DIM), 1.0),
        'cache_kv_win': nrm((DEC_BATCH, win_buf, 2, KV_HEADS, HEAD_DIM), 1.0),
        'state_conv': nrm((N_A_LAYERS, DEC_BATCH, CONV_W - 1, D_MODEL), 0.5),
        'page_table': page_table,
        'w_ada': nrm((DEPTH, 2, D_MODEL, 3 * D_MODEL), 0.5 * D_MODEL ** -0.5),
        'b_ada': nrm((DEPTH, 2, 3 * D_MODEL), 0.02),
        'norm_g': 1.0 + nrm((DEPTH, 2, D_MODEL), 0.05),
        'w_pw1': nrm((N_A_LAYERS, D_MODEL, 2 * D_MODEL), D_MODEL ** -0.5),
        'b_pw1': nrm((N_A_LAYERS, 2 * D_MODEL), 0.02),
        'w_dw': nrm((N_A_LAYERS, CONV_W, D_MODEL), CONV_W ** -0.5),
        'b_dw': nrm((N_A_LAYERS, D_MODEL), 0.02),
        'ln_g': 1.0 + nrm((N_A_LAYERS, D_MODEL), 0.05),
        'ln_b': nrm((N_A_LAYERS, D_MODEL), 0.02),
        'w_pw2': nrm((N_A_LAYERS, D_MODEL, D_MODEL), D_MODEL ** -0.5),
        'b_pw2': nrm((N_A_LAYERS, D_MODEL), 0.02),
        'g_kv': 1.0 + nrm((D_MODEL,), 0.05),
        'w_kv': nrm((D_MODEL, kv_width), D_MODEL ** -0.5),
        'w_phi1': nrm((2, CMP_BLOCK * HEAD_DIM, PHI_HIDDEN), (CMP_BLOCK * HEAD_DIM) ** -0.5),
        'b_phi1': nrm((2, PHI_HIDDEN), 0.02),
        'w_phi2': nrm((2, PHI_HIDDEN, HEAD_DIM), PHI_HIDDEN ** -0.5),
        'b_phi2': nrm((2, HEAD_DIM), 0.02),
        'pe_cmp': nrm((2, CMP_BLOCK, HEAD_DIM), 0.1),
        'w_qg': nrm((N_B_LAYERS, D_MODEL, qg_width), D_MODEL ** -0.5),
        'w_o': nrm((N_B_LAYERS, N_HEADS * HEAD_DIM, D_MODEL), (N_HEADS * HEAD_DIM) ** -0.5),
        'rel_table': nrm((N_BUCKETS, N_HEADS), 0.5),
        'w_gate': nrm((DEPTH, D_MODEL, D_FF), D_MODEL ** -0.5),
        'w_up': nrm((DEPTH, D_MODEL, D_FF), D_MODEL ** -0.5),
        'w_down': nrm((DEPTH, D_FF, D_MODEL), D_FF ** -0.5),
        'final_g': 1.0 + nrm((D_MODEL,), 0.05),
    }


def reference(x_prompt, x_sample, c_prompt, c_sample, cache_kv_cmp, cache_kv_sel, cache_kv_win, state_conv,
              page_table, w_ada, b_ada, norm_g, w_pw1, b_pw1, w_dw, b_dw, ln_g, ln_b, w_pw2, b_pw2,
              g_kv, w_kv, w_phi1, b_phi1, w_phi2, b_phi2, pe_cmp, w_qg, w_o, rel_table,
              w_gate, w_up, w_down, final_g):
    P = dict(w_ada=w_ada, b_ada=b_ada, norm_g=norm_g, w_pw1=w_pw1, b_pw1=b_pw1, w_dw=w_dw, b_dw=b_dw,
             ln_g=ln_g, ln_b=ln_b, w_pw2=w_pw2, b_pw2=b_pw2, g_kv=g_kv, w_kv=w_kv, w_phi1=w_phi1,
             b_phi1=b_phi1, w_phi2=w_phi2, b_phi2=b_phi2, pe_cmp=pe_cmp, w_qg=w_qg, w_o=w_o,
             rel_table=rel_table, w_gate=w_gate, w_up=w_up, w_down=w_down, final_g=final_g)
    y_prompt, cmp_p, sel_p, win_p, conv_p = trunk(x_prompt, c_prompt, True, None, None, None, None, None, P)
    y_sample, cmp_s, sel_s, win_s, conv_s = trunk(x_sample, c_sample, False, cache_kv_cmp, cache_kv_sel,
                                                  cache_kv_win, state_conv, page_table, P)
    return (y_prompt, y_sample, cmp_p, cmp_s, sel_p, sel_s, win_p, win_s, conv_p, conv_s)
```

```python
import functools
import math

import numpy as np
import jax
import jax.numpy as jnp
from jax import lax
from jax.experimental import pallas as pl
from jax.experimental.pallas import tpu as pltpu

F32 = jnp.float32
BF16 = jnp.bfloat16

N_HEADS = 16
KV_HEADS = 4
GROUP = N_HEADS // KV_HEADS
HEAD_DIM = 64
CONV_W = 31
CONV_PAD = 32
CMP_BLOCK = 32
SEL_BLOCK = 64
N_SEL = 16
WINDOW = 512
N_BUCKETS = 32
MAX_DISTANCE = 1024
EPS = 1e-6
FORCE = 1e4
MASK_NEG = -1e9
M_INIT = -1e30
QT = 128
NEAR_TILES = 8
V7X_VMEM_LIMIT_BYTES = 56 * 1024 * 1024


def _cparams(*sem):
    return pltpu.CompilerParams(dimension_semantics=sem, vmem_limit_bytes=V7X_VMEM_LIMIT_BYTES)


def _full_spec(a):
    nd = a.ndim
    return pl.BlockSpec(a.shape, lambda *_: (0,) * nd)


def _silu(x):
    return x * jax.nn.sigmoid(x)


def _norm_mod(x, g, shift, scale):
    ms = jnp.mean(x * x, axis=-1, keepdims=True)
    return x * lax.rsqrt(ms + EPS) * g * (1.0 + scale) + shift


def _dot(a, b):
    return jnp.dot(a, b, preferred_element_type=F32)


def _dot_nt(a, b):
    return lax.dot_general(a, b, (((1,), (1,)), ((), ())), preferred_element_type=F32)


def _ada_kernel(c_ref, w_ref, b_ref, o_ref):
    c = c_ref[...]
    o_ref[0] = _dot(_silu(c), w_ref[0]) + b_ref[0]


def ada_modulation(c_all, w_ada, b_ada):
    nl, d, n3 = w_ada.shape
    r = c_all.shape[0]
    tn = n3 // 2
    return pl.pallas_call(
        _ada_kernel,
        grid=(nl, n3 // tn),
        in_specs=[pl.BlockSpec((r, d), lambda l, j: (0, 0)),
                  pl.BlockSpec((1, d, tn), lambda l, j: (l, 0, j)),
                  pl.BlockSpec((1, 1, tn), lambda l, j: (l, 0, j))],
        out_specs=pl.BlockSpec((1, r, tn), lambda l, j: (l, 0, j)),
        out_shape=jax.ShapeDtypeStruct((nl, r, n3), F32),
        compiler_params=_cparams("parallel", "parallel"),
        name="ada_modulation",
    )(c_all, w_ada, b_ada)


def _row_spec(tm, width):
    return pl.BlockSpec((tm, width), lambda i: (i, 0))


def _mod_spec(mod, tiles_per_group):
    _, r, d = mod.shape
    return pl.BlockSpec((None, r, d), lambda i: (i // tiles_per_group, 0, 0))


def _pw1_kernel(x_ref, g_ref, sh_ref, sc_ref, w_ref, b_ref, u_ref):
    d = x_ref.shape[1]
    h = _norm_mod(x_ref[...], g_ref[...], sh_ref[...], sc_ref[...]).astype(BF16)
    z = _dot(h, w_ref[...]) + b_ref[...]
    u_ref[...] = z[:, :d] * jax.nn.sigmoid(z[:, d:])


def pw1_glu(x, g, shift, scale, w, b, tm, tpg):
    m, d = x.shape
    return pl.pallas_call(
        _pw1_kernel,
        grid=(m // tm,),
        in_specs=[_row_spec(tm, d), _full_spec(g), _mod_spec(shift, tpg), _mod_spec(scale, tpg),
                  _full_spec(w), _full_spec(b)],
        out_specs=_row_spec(tm, d),
        out_shape=jax.ShapeDtypeStruct((m, d), F32),
        compiler_params=_cparams("parallel"),
        name="pw1_glu",
    )(x, g, shift, scale, w, b)


def _conv_tail(full, wdw_ref, bdw_ref, lg_ref, lb_ref, w2_ref, b2_ref, tm):
    acc = None
    for k in range(CONV_W):
        term = full[pl.ds(k + CONV_PAD - CONV_W + 1, tm), :] * wdw_ref[k:k + 1, :]
        acc = term if acc is None else acc + term
    acc = acc + bdw_ref[...]
    mu = jnp.mean(acc, axis=-1, keepdims=True)
    cen = acc - mu
    var = jnp.mean(cen * cen, axis=-1, keepdims=True)
    yn = cen * lax.rsqrt(var + EPS) * lg_ref[...] + lb_ref[...]
    return _dot(_silu(yn).astype(BF16), w2_ref[...]) + b2_ref[...]


def _conv_kernel(fa_ref, fb_ref, wdw_ref, bdw_ref, lg_ref, lb_ref, w2_ref, b2_ref, x_ref, gate_ref,
                 o_ref, buf):
    tm = x_ref.shape[0]
    buf[0:tm, :] = fa_ref[0]
    buf[tm:tm + CONV_PAD, :] = fb_ref[0]
    out = _conv_tail(buf, wdw_ref, bdw_ref, lg_ref, lb_ref, w2_ref, b2_ref, tm)
    o_ref[...] = x_ref[...] + gate_ref[...] * out


def conv_pw2_residual(full, wdw, bdw, lg, lb, w2, b2, x, gate, tm, tpg):
    m, d = x.shape
    bsz, tp, _ = full.shape
    t = tp - CONV_PAD
    assert t % tm == 0 and tm % CONV_PAD == 0 and m == bsz * t
    tps = t // tm
    return pl.pallas_call(
        _conv_kernel,
        grid=(m // tm,),
        in_specs=[pl.BlockSpec((1, tm, d), lambda i: (i // tps, i % tps, 0)),
                  pl.BlockSpec((1, CONV_PAD, d),
                               lambda i: (i // tps, (i % tps + 1) * (tm // CONV_PAD), 0)),
                  _full_spec(wdw), _full_spec(bdw), _full_spec(lg), _full_spec(lb),
                  _full_spec(w2), _full_spec(b2), _row_spec(tm, d), _mod_spec(gate, tpg)],
        out_specs=_row_spec(tm, d),
        out_shape=jax.ShapeDtypeStruct((m, d), F32),
        scratch_shapes=[pltpu.VMEM((tm + CONV_PAD, d), F32)],
        compiler_params=_cparams("parallel"),
        name="conv_pw2_residual",
    )(full, full, wdw, bdw, lg, lb, w2, b2, x, gate)


def _conv_small_kernel(full_ref, wdw_ref, bdw_ref, lg_ref, lb_ref, w2_ref, b2_ref, x_ref, gate_ref,
                       o_ref):
    tm = x_ref.shape[1]
    out = _conv_tail(full_ref.at[0], wdw_ref, bdw_ref, lg_ref, lb_ref, w2_ref, b2_ref, tm)
    o_ref[0] = x_ref[0] + gate_ref[0] * out


def conv_pw2_residual_small(full, wdw, bdw, lg, lb, w2, b2, x, gate):
    bsz, ts, d = x.shape
    seq = lambda a: pl.BlockSpec((1,) + a.shape[1:], lambda i: (i, 0, 0))
    return pl.pallas_call(
        _conv_small_kernel,
        grid=(bsz,),
        in_specs=[seq(full), _full_spec(wdw), _full_spec(bdw), _full_spec(lg), _full_spec(lb),
                  _full_spec(w2), _full_spec(b2), seq(x), seq(gate)],
        out_specs=seq(x),
        out_shape=jax.ShapeDtypeStruct(x.shape, F32),
        compiler_params=_cparams("parallel"),
        name="conv_pw2_residual_small",
    )(full, wdw, bdw, lg, lb, w2, b2, x, gate)


def _ffn_kernel(x_ref, g_ref, sh_ref, sc_ref, gate_ref, wg_ref, wu_ref, wd_ref, *rest, final):
    x = x_ref[...]
    h = _norm_mod(x, g_ref[...], sh_ref[...], sc_ref[...]).astype(BF16)
    a = _dot(h, wg_ref[...])
    b = _dot(h, wu_ref[...])
    act = (_silu(a) * b).astype(BF16)
    xn = x + gate_ref[...] * _dot(act, wd_ref[...])
    if final:
        fg_ref, o_ref = rest
        ms = jnp.mean(xn * xn, axis=-1, keepdims=True)
        o_ref[...] = xn * lax.rsqrt(ms + EPS) * fg_ref[...]
    else:
        (o_ref,) = rest
        o_ref[...] = xn


def ffn(x, g, shift, scale, gate, wg, wu, wd, tm, tpg, final_g=None):
    m, d = x.shape
    final = final_g is not None
    ins = [x, g, shift, scale, gate, wg, wu, wd] + ([final_g] if final else [])
    specs = [_row_spec(tm, d), _full_spec(g), _mod_spec(shift, tpg), _mod_spec(scale, tpg),
             _mod_spec(gate, tpg), _full_spec(wg), _full_spec(wu), _full_spec(wd)]
    if final:
        specs.append(_full_spec(final_g))
    return pl.pallas_call(
        functools.partial(_ffn_kernel, final=final),
        grid=(m // tm,),
        in_specs=specs,
        out_specs=_row_spec(tm, d),
        out_shape=jax.ShapeDtypeStruct((m, d), F32),
        compiler_params=_cparams("parallel"),
        name="ffn_final" if final else "ffn",
    )(*ins)


def _kvproj_kernel(x_ref, g_ref, w_ref, o_ref):
    x = x_ref[...]
    ms = jnp.mean(x * x, axis=-1, keepdims=True)
    h = (x * lax.rsqrt(ms + EPS) * g_ref[...]).astype(BF16)
    o_ref[...] = _dot(h, w_ref[...])


def kv_proj(x, g, w, tm):
    m, d = x.shape
    n = w.shape[1]
    return pl.pallas_call(
        _kvproj_kernel,
        grid=(m // tm,),
        in_specs=[_row_spec(tm, d), _full_spec(g), _full_spec(w)],
        out_specs=_row_spec(tm, n),
        out_shape=jax.ShapeDtypeStruct((m, n), F32),
        compiler_params=_cparams("parallel"),
        name="kv_proj",
    )(x, g, w)


def _qg_kernel(x_ref, g_ref, sh_ref, sc_ref, w_ref, q_ref, gates_ref):
    nq = q_ref.shape[1]
    h = _norm_mod(x_ref[...], g_ref[...], sh_ref[...], sc_ref[...]).astype(BF16)
    z = _dot(h, w_ref[...])
    q_ref[...] = (z[:, :nq] * (HEAD_DIM ** -0.5)).astype(BF16)
    gates_ref[...] = jax.nn.sigmoid(z[:, nq:])


def qg_proj(x, g, shift, scale, w, tm, tpg):
    m, d = x.shape
    nq = N_HEADS * HEAD_DIM
    ng = w.shape[1] - nq
    return pl.pallas_call(
        _qg_kernel,
        grid=(m // tm,),
        in_specs=[_row_spec(tm, d), _full_spec(g), _mod_spec(shift, tpg), _mod_spec(scale, tpg),
                  _full_spec(w)],
        out_specs=[_row_spec(tm, nq), _row_spec(tm, ng)],
        out_shape=[jax.ShapeDtypeStruct((m, nq), BF16), jax.ShapeDtypeStruct((m, ng), F32)],
        compiler_params=_cparams("parallel"),
        name="qg_proj",
    )(x, g, shift, scale, w)


def _oproj_kernel(o_ref, w_ref, x_ref, gate_ref, out_ref):
    out_ref[...] = x_ref[...] + gate_ref[...] * _dot(o_ref[...], w_ref[...])


def o_proj_residual(o, w, x, gate, tm, tpg):
    m, d = x.shape
    return pl.pallas_call(
        _oproj_kernel,
        grid=(m // tm,),
        in_specs=[_row_spec(tm, o.shape[1]), _full_spec(w), _row_spec(tm, d), _mod_spec(gate, tpg)],
        out_specs=_row_spec(tm, d),
        out_shape=jax.ShapeDtypeStruct((m, d), F32),
        compiler_params=_cparams("parallel"),
        name="o_proj_residual",
    )(o, w, x, gate)


def _compress_kernel(x_ref, pe_ref, w1_ref, b1_ref, w2_ref, b2_ref, o_ref):
    for g in range(KV_HEADS):
        xin = (x_ref[0, 0, g] + pe_ref[0]).astype(BF16)
        hid = _silu(_dot(xin, w1_ref[0]) + b1_ref[0])
        o_ref[0, 0, g] = _dot(hid.astype(BF16), w2_ref[0]) + b2_ref[0]


def compress(x, pe_flat, w1, b1, w2, b2):
    bc, _, kv, n, f = x.shape
    per_e = lambda a: pl.BlockSpec((1,) + a.shape[1:], lambda b, e: (e,) + (0,) * (a.ndim - 1))
    return pl.pallas_call(
        _compress_kernel,
        grid=(bc, 2),
        in_specs=[pl.BlockSpec((1, 1, kv, n, f), lambda b, e: (b, e, 0, 0, 0)),
                  per_e(pe_flat), per_e(w1), per_e(b1), per_e(w2), per_e(b2)],
        out_specs=pl.BlockSpec((1, 1, kv, n, HEAD_DIM), lambda b, e: (b, e, 0, 0, 0)),
        out_shape=jax.ShapeDtypeStruct((bc, 2, kv, n, HEAD_DIM), F32),
        compiler_params=_cparams("parallel", "parallel"),
        name="compress",
    )(x, pe_flat, w1, b1, w2, b2)


GATHER_PAGES_PER_STEP = 8


def _gather_kernel(pt_ref, *refs):
    o_ref = refs[-1]
    page = refs[0].shape[1]
    for k, r in enumerate(refs[:-1]):
        o_ref[0, k * page:(k + 1) * page, :] = r[0]


def gather_pages(cache, page_table):
    n_pool, page, c = cache.shape
    db, n_pages = page_table.shape
    pps = GATHER_PAGES_PER_STEP
    assert n_pages % pps == 0

    def page_spec(k):
        return pl.BlockSpec((1, page, c), lambda b, j, pt: (pt[b, j * pps + k], 0, 0))

    return pl.pallas_call(
        _gather_kernel,
        grid_spec=pltpu.PrefetchScalarGridSpec(
            num_scalar_prefetch=1,
            grid=(db, n_pages // pps),
            in_specs=[page_spec(k) for k in range(pps)],
            out_specs=pl.BlockSpec((1, pps * page, c), lambda b, j, pt: (b, j, 0)),
        ),
        out_shape=jax.ShapeDtypeStruct((db, n_pages * page, c), cache.dtype),
        compiler_params=_cparams("parallel", "parallel"),
        name="gather_pages",
    )(page_table, *([cache] * pps))


def _rel_bucket(dist):
    n = jnp.maximum(dist, 0)
    exact = N_BUCKETS // 2
    nf = jnp.maximum(n, 1).astype(F32)
    large = exact + (jnp.log(nf / exact) / math.log(MAX_DISTANCE / exact) * (N_BUCKETS - exact)).astype(jnp.int32)
    large = jnp.minimum(large, N_BUCKETS - 1)
    return jnp.where(n < exact, n, large)


def _bias_by_distance(table, max_dist):
    return table.astype(F32)[_rel_bucket(jnp.arange(max_dist))].T


def _softmax_step(s, vt, m, l, acc):
    m_new = jnp.maximum(m, jnp.max(s, axis=0, keepdims=True))
    alpha = jnp.exp(m - m_new)
    p = jnp.exp(s - m_new)
    l = alpha * l + jnp.sum(p, axis=0, keepdims=True)
    acc = alpha * acc + _dot(vt, p.astype(BF16))
    return m_new, l, acc


def _attn_kernel(bfar_ref, q_ref, gt_ref, kaug_ref, vst_ref, kw_ref, vwt_ref, ck_ref, cvt_ref,
                 bc_ref, bn_ref, o_ref, imp_ref, sc_ref, *, nb, nc):
    g = pl.program_id(0)
    qt = pl.program_id(2)
    qs = qt * QT
    gq = GROUP * QT

    lane_cat = lambda parts: jnp.concatenate(parts, axis=1)
    q_t = q_ref[0].astype(F32).T.astype(BF16)
    q_heads = [q_t[h * HEAD_DIM:(h + 1) * HEAD_DIM] for h in range(GROUP)]
    q_plain = lane_cat(q_heads)

    col_i = lax.broadcasted_iota(jnp.int32, (1, gq), 1) % QT

    s_c = _dot(ck_ref[0, 0], q_plain) + lane_cat([bc_ref[h, 0] for h in range(GROUP)])
    n_idx = lax.broadcasted_iota(jnp.int32, (nc, gq), 0)
    valid_c = (n_idx * CMP_BLOCK + CMP_BLOCK - 1) <= (qs + col_i)
    s_c = jnp.where(valid_c, s_c, M_INIT)
    m_c = jnp.max(s_c, axis=0, keepdims=True)
    p_c = jnp.where(valid_c, jnp.exp(s_c - m_c), 0.0)
    l_c = jnp.sum(p_c, axis=0, keepdims=True)
    p_c = p_c * jnp.where(l_c > 0.0, 1.0 / l_c, 0.0)
    o_c = _dot(cvt_ref[0, 0], p_c.astype(BF16))

    imp = p_c[:, 0:QT]
    for h in range(1, GROUP):
        imp = imp + p_c[:, h * QT:(h + 1) * QT]
    imp_ref[...] = imp
    ratio = SEL_BLOCK // CMP_BLOCK
    imp2 = imp_ref[pl.ds(0, nb, stride=ratio), :]
    for r in range(1, ratio):
        imp2 = imp2 + imp_ref[pl.ds(r, nb, stride=ratio), :]
    j_idx = lax.broadcasted_iota(jnp.int32, (nb, QT), 0)
    i_idx = lax.broadcasted_iota(jnp.int32, (nb, QT), 1)
    jq = (qs + i_idx) // SEL_BLOCK
    forced = (j_idx == 0) | (j_idx == jq) | (j_idx == jq - 1)
    score = jnp.where(j_idx <= jq, jnp.where(forced, FORCE, imp2), -1.0)
    sc_ref[...] = score

    def rank_body(c, cnt):
        row = sc_ref[pl.ds(c, 1), :]
        ge = jnp.where(row >= score, 1.0, 0.0)
        gt = jnp.where(row > score, 1.0, 0.0)
        return cnt + jnp.where(j_idx > c, ge, gt)

    cnt = lax.fori_loop(0, nb, rank_body, jnp.zeros((nb, QT), F32))
    selected = (cnt < float(min(N_SEL, nb))) & (score >= 0.0)
    mask_feat = jnp.where(selected, 0.0, MASK_NEG).astype(BF16)
    q_aug = lane_cat([jnp.concatenate([qh, mask_feat], axis=0) for qh in q_heads])
    q_win = lane_cat([jnp.concatenate([qh, jnp.zeros_like(mask_feat)], axis=0) for qh in q_heads])

    row_j = lax.broadcasted_iota(jnp.int32, (QT, gq), 0)
    far_row = lane_cat([jnp.full((1, QT), bfar_ref[g * GROUP + h], F32) for h in range(GROUP)])

    def near_bias(dm):
        return lane_cat([bn_ref[dm, h] for h in range(GROUP)])

    init = (jnp.full((1, gq), M_INIT, F32), jnp.zeros((1, gq), F32), jnp.zeros((HEAD_DIM, gq), F32))

    def sel_tile(kt, carry, bias):
        ks = pl.multiple_of(kt * QT, QT)
        s = _dot(kaug_ref[0, 0, pl.ds(ks, QT), :], q_aug) + bias
        return s, vst_ref[0, 0, :, pl.ds(ks, QT)]

    def far_body(kt, carry):
        s, vt = sel_tile(kt, carry, far_row)
        return _softmax_step(s, vt, *carry)

    def near_body(kt, carry):
        s, vt = sel_tile(kt, carry, near_bias(qt - kt))
        return _softmax_step(s, vt, *carry)

    n_far = jnp.maximum(qt - (NEAR_TILES - 1), 0)
    carry = lax.fori_loop(0, n_far, far_body, init)
    carry = lax.fori_loop(n_far, qt, near_body, carry)
    s, vt = sel_tile(qt, carry, near_bias(0))
    s = jnp.where(row_j <= col_i, s, MASK_NEG)
    m_s, l_s, acc_s = _softmax_step(s, vt, *carry)
    o_s = acc_s / l_s

    def win_body(kt, carry):
        ks = pl.multiple_of(kt * QT, QT)
        s = _dot(kw_ref[0, 0, pl.ds(ks, QT), :], q_win) + near_bias(qt - kt)
        dist = (qs + col_i) - (ks + row_j)
        s = jnp.where((dist >= 0) & (dist < WINDOW), s, MASK_NEG)
        return _softmax_step(s, vwt_ref[0, 0, :, pl.ds(ks, QT)], *carry)

    m_w, l_w, acc_w = lax.fori_loop(jnp.maximum(qt - WINDOW // QT, 0), qt + 1, win_body, init)
    o_w = acc_w / l_w

    gate = lambda br: lane_cat([gt_ref[0, 0, h * 3 + br:h * 3 + br + 1, :] for h in range(GROUP)])
    o_t = gate(0) * o_c + gate(1) * o_s + gate(2) * o_w
    o_rows = jnp.concatenate([o_t[:, h * QT:(h + 1) * QT] for h in range(GROUP)], axis=0)
    o_ref[0] = o_rows.T.astype(BF16)


def prompt_attention(q, gates_t, kaug, vst, kwin, vwt, ck, cvt, bias_cmp, bias_near, bias_far):
    bsz, t, dq = q.shape
    nb = t // SEL_BLOCK
    nc = t // CMP_BLOCK
    nq = t // QT
    gd = GROUP * HEAD_DIM
    per_bg = lambda a: pl.BlockSpec((1, 1) + a.shape[2:], lambda g, b, i, *_: (b, g, 0, 0))
    grid_spec = pltpu.PrefetchScalarGridSpec(
        num_scalar_prefetch=1,
        grid=(KV_HEADS, bsz, nq),
        in_specs=[pl.BlockSpec((1, QT, gd), lambda g, b, i, *_: (b, i, g)),
                  pl.BlockSpec((1, 1, 3 * GROUP, QT), lambda g, b, i, *_: (b, g, 0, i)),
                  per_bg(kaug), per_bg(vst), per_bg(kwin), per_bg(vwt), per_bg(ck), per_bg(cvt),
                  pl.BlockSpec((GROUP, 1, nc, QT), lambda g, b, i, *_: (g, i, 0, 0)),
                  pl.BlockSpec((NEAR_TILES, GROUP, QT, QT), lambda g, b, i, *_: (0, g, 0, 0))],
        out_specs=pl.BlockSpec((1, QT, gd), lambda g, b, i, *_: (b, i, g)),
        scratch_shapes=[pltpu.VMEM((nc, QT), F32), pltpu.VMEM((nb, QT), F32)],
    )
    return pl.pallas_call(
        functools.partial(_attn_kernel, nb=nb, nc=nc),
        grid_spec=grid_spec,
        out_shape=jax.ShapeDtypeStruct((bsz, t, dq), BF16),
        compiler_params=_cparams("parallel", "parallel", "parallel"),
        name="prompt_attention",
    )(bias_far, q, gates_t, kaug, vst, kwin, vwt, ck, cvt, bias_cmp, bias_near)


KEY_CHUNK = 1024
NEW_PAD = 128


def _diag_heads(full, row_g):
    out = None
    for g in range(KV_HEADS):
        part = jnp.where(row_g == g, full[:, g * HEAD_DIM:(g + 1) * HEAD_DIM], 0.0)
        out = part if out is None else out + part
    return out


def _sample_attn_kernel(q_ref, gates_ref, ksel_ref, nsel_ref, kwin_ref, nwin_ref, ck_ref, cv_ref,
                        bsel_ref, bwin_ref, bcmp_ref, e_ref, o_ref, s_scr, *, past, ts):
    kvd = KV_HEADS * HEAD_DIM
    rows = GROUP * KV_HEADS * ts
    nc = ck_ref.shape[1]
    q = q_ref[0]
    r_idx = lax.broadcasted_iota(jnp.int32, (rows, 1), 0)
    row_q = r_idx % ts
    row_g = (r_idx // ts) % KV_HEADS

    s_c = _dot_nt(q, ck_ref[0]) + bcmp_ref[...]
    p_c = jnp.exp(s_c - jnp.max(s_c, axis=1, keepdims=True))
    p_c = p_c / jnp.sum(p_c, axis=1, keepdims=True)
    o_c = _diag_heads(_dot(p_c.astype(BF16), cv_ref[0]), row_g)

    per = KV_HEADS * ts
    imp = p_c[0:per]
    for h in range(1, GROUP):
        imp = imp + p_c[h * per:(h + 1) * per]
    ratio = SEL_BLOCK // CMP_BLOCK
    imp2 = imp
    for r in range(1, ratio):
        imp2 = imp2 + pltpu.roll(imp, nc - r, axis=1)
    lane = lax.broadcasted_iota(jnp.int32, (per, nc), 1)
    blk = lane // ratio
    jq = past // SEL_BLOCK
    forced = (blk == 0) | (blk == jq - 1)
    score = jnp.where(lane % ratio == 0, jnp.where(forced, FORCE, imp2), -2.0)
    cnt = jnp.where(score < FORCE, 1.0, 0.0)
    for c in range(nc // ratio):
        col = score[:, c * ratio:c * ratio + 1]
        ge = jnp.where(col >= score, 1.0, 0.0)
        gt = jnp.where(col > score, 1.0, 0.0)
        cnt = cnt + jnp.where(lane > c * ratio, ge, gt)
    selected = (cnt < float(min(N_SEL, jq + 1))) & (score >= 0.0)
    mask_feat = jnp.where(selected, 0.0, MASK_NEG).astype(BF16)
    mask_feat = jnp.concatenate([mask_feat] * GROUP, axis=0)

    for c in range(past // KEY_CHUNK):
        sl = slice(c * KEY_CHUNK, (c + 1) * KEY_CHUNK)
        k = ksel_ref[0, sl, 0:kvd].astype(BF16)
        s_scr[:, sl] = _dot_nt(q, k) + bsel_ref[:, sl] + _dot(mask_feat, e_ref[:, sl])
    new_j = lax.broadcasted_iota(jnp.int32, (rows, NEW_PAD), 1)
    new_ok = new_j <= row_q
    s_new = _dot_nt(q, nsel_ref[0, :, 0:kvd].astype(BF16)) + bsel_ref[:, past:past + NEW_PAD]
    s_scr[:, past:past + NEW_PAD] = jnp.where(new_ok, s_new, MASK_NEG)
    s_all = s_scr[...]
    p_s = jnp.exp(s_all - jnp.max(s_all, axis=1, keepdims=True))
    l_s = jnp.sum(p_s, axis=1, keepdims=True)
    s_scr[...] = p_s
    acc = _dot(s_scr[:, past:past + NEW_PAD].astype(BF16), nsel_ref[0, :, kvd:2 * kvd].astype(BF16))
    for c in range(past // KEY_CHUNK):
        sl = slice(c * KEY_CHUNK, (c + 1) * KEY_CHUNK)
        acc = acc + _dot(s_scr[:, sl].astype(BF16), ksel_ref[0, sl, kvd:2 * kvd].astype(BF16))
    o_s = _diag_heads(acc, row_g) / l_s

    wlen = kwin_ref.shape[1]
    win_i = lax.broadcasted_iota(jnp.int32, (rows, wlen), 1)
    s_w1 = _dot_nt(q, kwin_ref[0, :, 0:kvd].astype(BF16)) + bwin_ref[:, 0:wlen]
    s_w1 = jnp.where(win_i + (WINDOW - wlen) > row_q, s_w1, MASK_NEG)
    s_w2 = _dot_nt(q, nwin_ref[0, :, 0:kvd].astype(BF16)) + bwin_ref[:, wlen:wlen + NEW_PAD]
    s_w2 = jnp.where(new_ok, s_w2, MASK_NEG)
    m_w = jnp.maximum(jnp.max(s_w1, axis=1, keepdims=True), jnp.max(s_w2, axis=1, keepdims=True))
    p_w1 = jnp.exp(s_w1 - m_w)
    p_w2 = jnp.exp(s_w2 - m_w)
    l_w = jnp.sum(p_w1, axis=1, keepdims=True) + jnp.sum(p_w2, axis=1, keepdims=True)
    acc_w = (_dot(p_w1.astype(BF16), kwin_ref[0, :, kvd:2 * kvd].astype(BF16))
             + _dot(p_w2.astype(BF16), nwin_ref[0, :, kvd:2 * kvd].astype(BF16)))
    o_w = _diag_heads(acc_w, row_g) / l_w

    gts = gates_ref[0]
    o_ref[0] = gts[:, 0:1] * o_c + gts[:, 1:2] * o_s + gts[:, 2:3] * o_w


def sample_attention(q_bd, gates, ksel, nsel, kwin, nwin, ck, cv, bsel, bwin, bcmp, e_mat, past, ts):
    db, rows, kvd = q_bd.shape
    seq = lambda a: pl.BlockSpec((1,) + a.shape[1:], lambda b: (b,) + (0,) * (a.ndim - 1))
    return pl.pallas_call(
        functools.partial(_sample_attn_kernel, past=past, ts=ts),
        grid=(db,),
        in_specs=[seq(q_bd), seq(gates), seq(ksel), seq(nsel), seq(kwin), seq(nwin), seq(ck), seq(cv),
                  _full_spec(bsel), _full_spec(bwin), _full_spec(bcmp), _full_spec(e_mat)],
        out_specs=pl.BlockSpec((1, rows, HEAD_DIM), lambda b: (b, 0, 0)),
        out_shape=jax.ShapeDtypeStruct((db, rows, HEAD_DIM), F32),
        scratch_shapes=[pltpu.VMEM((rows, past + NEW_PAD), F32)],
        compiler_params=_cparams("parallel"),
        name="sample_attention",
    )(q_bd, gates, ksel, nsel, kwin, nwin, ck, cv, bsel, bwin, bcmp, e_mat)


def _prep_weights(P):
    W = {}
    cast = lambda a: a.astype(BF16)
    for k in ("w_pw1", "w_pw2", "w_kv", "w_o", "w_gate", "w_up", "w_down", "w_phi1", "w_phi2"):
        W[k] = cast(P[k])
    nq = N_HEADS * HEAD_DIM
    wqg = P["w_qg"]
    pad = 128 - (wqg.shape[2] - nq)
    W["w_qg"] = cast(jnp.pad(wqg, ((0, 0), (0, 0), (0, pad))))
    return W


def _trunk(x, mods, is_prompt, caches, P, W, bias_dist):
    bsz, t, d = x.shape
    m = bsz * t
    depth = P["w_gate"].shape[0]
    n_a = P["w_pw1"].shape[0]
    if is_prompt:
        tm, tpg = 256, t // 256
    else:
        tm, tpg = m, 1
    x2 = x.reshape(m, d)
    row2 = lambda a: a.reshape(1, -1)
    conv_states = []
    y = None
    for l in range(depth):
        shift, scale, gate = mods[(l, 0)]
        g0 = row2(P["norm_g"][l, 0])
        if l < n_a:
            u = pw1_glu(x2, g0, shift, scale, W["w_pw1"][l], row2(P["b_pw1"][l]), tm, tpg)
            u3 = u.reshape(bsz, t, d)
            if is_prompt:
                hist = jnp.zeros((bsz, CONV_PAD, d), F32)
            else:
                hist = jnp.pad(caches["state_conv"][l], ((0, 0), (CONV_PAD - CONV_W + 1, 0), (0, 0)))
            full = jnp.concatenate([hist, u3], axis=1)
            conv_states.append(full[:, -(CONV_W - 1):])
            wdw = jnp.pad(P["w_dw"][l], ((0, CONV_PAD - CONV_W), (0, 0)))
            args = (wdw, row2(P["b_dw"][l]), row2(P["ln_g"][l]), row2(P["ln_b"][l]), W["w_pw2"][l],
                    row2(P["b_pw2"][l]))
            if is_prompt:
                x2 = conv_pw2_residual(full, *args, x2, gate, tm, tpg)
            else:
                x2 = conv_pw2_residual_small(full, *args, x2.reshape(bsz, t, d),
                                             gate.reshape(bsz, t, d)[:, 0:1]).reshape(m, d)
        else:
            if l == n_a:
                rows = kv_proj(x2, row2(P["g_kv"]), W["w_kv"], tm)
                rows6 = rows.reshape(bsz, t, 3, 2, KV_HEADS, HEAD_DIM)
                if is_prompt:
                    att = _prompt_attention_setup(rows6, P, W, bias_dist)
                    win_state = rows6[:, -min(WINDOW, t):, 2]
                else:
                    att = _sample_attention_setup(rows6, caches, P, W, bias_dist)
                    wl = caches["cache_kv_win"].shape[1]
                    win_state = jnp.concatenate([caches["cache_kv_win"], rows6[:, :, 2]], axis=1)[:, -wl:]
            lb = l - n_a
            q, gates = qg_proj(x2, g0, shift, scale, W["w_qg"][lb], tm, tpg)
            if is_prompt:
                o = _prompt_attention_layer(q.reshape(bsz, t, -1), gates.reshape(bsz, t, -1), att)
            else:
                o = _sample_attention_layer(q.reshape(bsz, t, -1), gates.reshape(bsz, t, -1), att)
            x2 = o_proj_residual(o.reshape(m, -1), W["w_o"][lb], x2, gate, tm, tpg)
        shift, scale, gate = mods[(l, 1)]
        fg = row2(P["final_g"]) if l == depth - 1 else None
        out = ffn(x2, row2(P["norm_g"][l, 1]), shift, scale, gate, W["w_gate"][l], W["w_up"][l],
                  W["w_down"][l], tm, tpg, final_g=fg)
        if l == depth - 1:
            y = out
        else:
            x2 = out
    return (y.reshape(bsz, t, d), rows6[:, :, 0], rows6[:, :, 1], win_state, jnp.stack(conv_states))


def _compress_inputs(P):
    pe_flat = P["pe_cmp"].reshape(2, 1, CMP_BLOCK * HEAD_DIM)
    return pe_flat, P["b_phi1"][:, None, :], P["b_phi2"][:, None, :]


def _prompt_attention_setup(rows6, P, W, bias_dist):
    bsz, t = rows6.shape[:2]
    nb = t // SEL_BLOCK
    nc = t // CMP_BLOCK
    nq = t // QT
    xc = rows6[:, :, 0].reshape(bsz, nc, CMP_BLOCK, 2, KV_HEADS, HEAD_DIM)
    xc = xc.transpose(0, 3, 4, 1, 2, 5).reshape(bsz, 2, KV_HEADS, nc, CMP_BLOCK * HEAD_DIM)
    pe_flat, b1, b2 = _compress_inputs(P)
    cmp = compress(xc, pe_flat, W["w_phi1"], b1, W["w_phi2"], b2)
    ck = cmp[:, 0].astype(BF16)
    cvt = cmp[:, 1].transpose(0, 1, 3, 2).astype(BF16)
    onehot = (jnp.arange(t)[:, None] // SEL_BLOCK == jnp.arange(nb)[None, :]).astype(BF16)
    ksel = rows6[:, :, 1, 0].transpose(0, 2, 1, 3).astype(BF16)
    kaug = jnp.concatenate([ksel, jnp.broadcast_to(onehot, (bsz, KV_HEADS, t, nb))], axis=-1)
    vst = rows6[:, :, 1, 1].transpose(0, 2, 3, 1).astype(BF16)
    kwin = rows6[:, :, 2, 0].transpose(0, 2, 1, 3).astype(BF16)
    kwin = jnp.concatenate([kwin, jnp.zeros((bsz, KV_HEADS, t, nb), BF16)], axis=-1)
    vwt = rows6[:, :, 2, 1].transpose(0, 2, 3, 1).astype(BF16)
    qpos = np.arange(t).reshape(nq, 1, QT)
    cend = (np.arange(nc) * CMP_BLOCK + CMP_BLOCK - 1).reshape(1, nc, 1)
    bias_cmp = bias_dist[:, np.maximum(qpos - cend, 0)]
    dm = np.arange(NEAR_TILES).reshape(-1, 1, 1)
    dist = dm * QT + np.arange(QT).reshape(1, 1, QT) - np.arange(QT).reshape(1, QT, 1)
    bias_near = bias_dist[:, np.maximum(dist, 0)].transpose(1, 0, 2, 3)
    bias_far = bias_dist[:, NEAR_TILES * QT]
    return dict(kaug=kaug, vst=vst, kwin=kwin, vwt=vwt, ck=ck, cvt=cvt, bias_cmp=bias_cmp,
                bias_near=bias_near, bias_far=bias_far)


def _prompt_attention_layer(q, gates, att):
    bsz, t, _ = q.shape
    gates_t = gates[:, :, :3 * N_HEADS].reshape(bsz, t, KV_HEADS, 3 * GROUP).transpose(0, 2, 3, 1)
    return prompt_attention(q, gates_t, att["kaug"], att["vst"], att["kwin"], att["vwt"], att["ck"],
                            att["cvt"], att["bias_cmp"], att["bias_near"], att["bias_far"])


def _sample_attention_setup(rows6, caches, P, W, bias_dist):
    db, ts = rows6.shape[:2]
    pt = caches["page_table"]
    n_pages = pt.shape[1]
    page = caches["cache_kv_cmp"].shape[1]
    past = n_pages * page
    c = 2 * KV_HEADS * HEAD_DIM
    cmp_g = gather_pages(caches["cache_kv_cmp"].reshape(-1, page, c), pt)
    sel_g = gather_pages(caches["cache_kv_sel"].reshape(-1, page, c), pt)
    nc = past // CMP_BLOCK
    assert (past + ts) // CMP_BLOCK == nc and past % KEY_CHUNK == 0
    xc = cmp_g.reshape(db, nc, CMP_BLOCK, 2, KV_HEADS, HEAD_DIM)
    xc = xc.transpose(0, 3, 4, 1, 2, 5).reshape(db, 2, KV_HEADS, nc, CMP_BLOCK * HEAD_DIM)
    pe_flat, b1, b2 = _compress_inputs(P)
    cmp = compress(xc, pe_flat, W["w_phi1"], b1, W["w_phi2"], b2)
    ck = cmp[:, 0].transpose(0, 2, 1, 3).reshape(db, nc, KV_HEADS * HEAD_DIM).astype(BF16)
    cv = cmp[:, 1].transpose(0, 2, 1, 3).reshape(db, nc, KV_HEADS * HEAD_DIM).astype(BF16)
    pad_new = lambda a: jnp.pad(a.reshape(db, ts, c), ((0, 0), (0, NEW_PAD - ts), (0, 0)))
    nsel = pad_new(rows6[:, :, 1])
    nwin = pad_new(rows6[:, :, 2])
    kwin = caches["cache_kv_win"].reshape(db, -1, c)
    wlen = kwin.shape[1]
    rows = GROUP * KV_HEADS * ts
    r = np.arange(rows)
    head = ((r // ts) % KV_HEADS) * GROUP + r // (KV_HEADS * ts)
    qpos = past + r % ts
    by_row = lambda kpos: bias_dist[head[:, None], np.maximum(qpos[:, None] - kpos[None, :], 0)]
    bsel = by_row(np.arange(past + NEW_PAD))
    wpos = np.concatenate([past - wlen + np.arange(wlen), past + np.arange(NEW_PAD)])
    bwin = by_row(wpos)
    bcmp = by_row(np.arange(nc) * CMP_BLOCK + CMP_BLOCK - 1)
    ratio = SEL_BLOCK // CMP_BLOCK
    lane_blk = np.where(np.arange(nc) % ratio == 0, np.arange(nc) // ratio, -1)
    e_mat = jnp.asarray(lane_blk[:, None] == (np.arange(past) // SEL_BLOCK)[None, :], BF16)
    return dict(ksel=sel_g, nsel=nsel, kwin=kwin, nwin=nwin, ck=ck, cv=cv, bsel=bsel, bwin=bwin,
                bcmp=bcmp, e_mat=e_mat, past=past)


def _sample_attention_layer(q, gates, att):
    db, ts, _ = q.shape
    rows = GROUP * KV_HEADS * ts
    q5 = q.reshape(db, ts, KV_HEADS, GROUP, HEAD_DIM).transpose(0, 3, 2, 1, 4)
    eye = jnp.eye(KV_HEADS, dtype=q.dtype)
    q_bd = (q5[:, :, :, :, None, :] * eye[None, None, :, None, :, None]).reshape(db, rows, KV_HEADS * HEAD_DIM)
    g5 = gates[:, :, :3 * N_HEADS].reshape(db, ts, KV_HEADS, GROUP, 3).transpose(0, 3, 2, 1, 4)
    g_rows = jnp.pad(g5.reshape(db, rows, 3), ((0, 0), (0, 0), (0, 125)))
    o = sample_attention(q_bd, g_rows, att["ksel"], att["nsel"], att["kwin"], att["nwin"], att["ck"],
                         att["cv"], att["bsel"], att["bwin"], att["bcmp"], att["e_mat"], att["past"], ts)
    o = o.reshape(db, GROUP, KV_HEADS, ts, HEAD_DIM).transpose(0, 3, 2, 1, 4)
    return o.reshape(db, ts, N_HEADS * HEAD_DIM).astype(BF16)


def kernel(x_prompt, x_sample, c_prompt, c_sample, cache_kv_cmp, cache_kv_sel, cache_kv_win, state_conv, page_table, w_ada, b_ada, norm_g, w_pw1, b_pw1, w_dw, b_dw, ln_g, ln_b, w_pw2, b_pw2, g_kv, w_kv, w_phi1, b_phi1, w_phi2, b_phi2, pe_cmp, w_qg, w_o, rel_table, w_gate, w_up, w_down, final_g):
    P = dict(w_ada=w_ada, b_ada=b_ada, norm_g=norm_g, w_pw1=w_pw1, b_pw1=b_pw1, w_dw=w_dw, b_dw=b_dw,
             ln_g=ln_g, ln_b=ln_b, w_pw2=w_pw2, b_pw2=b_pw2, g_kv=g_kv, w_kv=w_kv, w_phi1=w_phi1,
             b_phi1=b_phi1, w_phi2=w_phi2, b_phi2=b_phi2, pe_cmp=pe_cmp, w_qg=w_qg, w_o=w_o,
             rel_table=rel_table, w_gate=w_gate, w_up=w_up, w_down=w_down, final_g=final_g)
    W = _prep_weights(P)
    bp, tp, d = x_prompt.shape
    db, ts, _ = x_sample.shape
    depth = w_ada.shape[0]

    n_c = bp + db
    r_pad = -(-n_c // 8) * 8
    c_all = jnp.pad(jnp.concatenate([c_prompt, c_sample], axis=0), ((0, r_pad - n_c), (0, 0)))
    mod = ada_modulation(c_all, w_ada.reshape(depth * 2, d, 3 * d), b_ada.reshape(depth * 2, 1, 3 * d))
    mods_p, mods_s = {}, {}
    for l in range(depth):
        for j in range(2):
            parts = [mod[l * 2 + j, :, k * d:(k + 1) * d] for k in range(3)]
            mods_p[(l, j)] = tuple(p[:bp, None, :] for p in parts)
            mods_s[(l, j)] = tuple(jnp.repeat(p[bp:n_c], ts, axis=0)[None] for p in parts)

    past = page_table.shape[1] * cache_kv_cmp.shape[1]
    bias_dist = _bias_by_distance(rel_table, max(tp, past + 2 * NEW_PAD))
    caches = dict(cache_kv_cmp=cache_kv_cmp, cache_kv_sel=cache_kv_sel, cache_kv_win=cache_kv_win,
                  state_conv=state_conv, page_table=page_table)
    y_p, cmp_p, sel_p, win_p, conv_p = _trunk(x_prompt, mods_p, True, None, P, W, bias_dist)
    y_s, cmp_s, sel_s, win_s, conv_s = _trunk(x_sample, mods_s, False, caches, P, W, bias_dist)
    return (y_p, y_s, cmp_p, cmp_s, sel_p, sel_s, win_p, win_s, conv_p, conv_s)
```

```python
import functools
import math

import numpy as np
import jax
import jax.numpy as jnp
from jax import lax
from jax.experimental import pallas as pl
from jax.experimental.pallas import tpu as pltpu

F32 = jnp.float32
BF16 = jnp.bfloat16

N_HEADS = 16
KV_HEADS = 4
GROUP = N_HEADS // KV_HEADS
HEAD_DIM = 64
CONV_W = 31
CONV_PAD = 32
CMP_BLOCK = 32
SEL_BLOCK = 64
N_SEL = 16
WINDOW = 512
N_BUCKETS = 32
MAX_DISTANCE = 1024
EPS = 1e-6
FORCE = 1e4
MASK_NEG = -1e9
M_INIT = -1e30
QT = 128
NEAR_TILES = 8
KEY_STEP = 512
NEAR_SPAN = NEAR_TILES + KEY_STEP // QT - 1
V7X_VMEM_LIMIT_BYTES = 56 * 1024 * 1024


def _cparams(*sem):
    return pltpu.CompilerParams(dimension_semantics=sem, vmem_limit_bytes=V7X_VMEM_LIMIT_BYTES)


def _full_spec(a):
    nd = a.ndim
    return pl.BlockSpec(a.shape, lambda *_: (0,) * nd)


def _silu(x):
    return x * jax.nn.sigmoid(x)


def _norm_mod(x, g, shift, scale):
    ms = jnp.mean(x * x, axis=-1, keepdims=True)
    return x * lax.rsqrt(ms + EPS) * g * (1.0 + scale) + shift


def _dot(a, b):
    return jnp.dot(a, b, preferred_element_type=F32)


def _dot_nt(a, b):
    return lax.dot_general(a, b, (((1,), (1,)), ((), ())), preferred_element_type=F32)


def _ada_kernel(c_ref, w_ref, b_ref, o_ref):
    c = c_ref[...]
    o_ref[0] = _dot(_silu(c), w_ref[0]) + b_ref[0]


def ada_modulation(c_all, w_ada, b_ada):
    nl, d, n3 = w_ada.shape
    r = c_all.shape[0]
    tn = n3 // 2
    return pl.pallas_call(
        _ada_kernel,
        grid=(nl, n3 // tn),
        in_specs=[pl.BlockSpec((r, d), lambda l, j: (0, 0)),
                  pl.BlockSpec((1, d, tn), lambda l, j: (l, 0, j)),
                  pl.BlockSpec((1, 1, tn), lambda l, j: (l, 0, j))],
        out_specs=pl.BlockSpec((1, r, tn), lambda l, j: (l, 0, j)),
        out_shape=jax.ShapeDtypeStruct((nl, r, n3), F32),
        compiler_params=_cparams("parallel", "parallel"),
        name="ada_modulation",
    )(c_all, w_ada, b_ada)


def _row_spec(tm, width):
    return pl.BlockSpec((tm, width), lambda i: (i, 0))


def _mod_spec(mod, tiles_per_group):
    _, r, d = mod.shape
    return pl.BlockSpec((None, r, d), lambda i: (i // tiles_per_group, 0, 0))


def _pw1_kernel(x_ref, g_ref, sh_ref, sc_ref, w_ref, b_ref, u_ref):
    d = x_ref.shape[1]
    h = _norm_mod(x_ref[...], g_ref[...], sh_ref[...], sc_ref[...]).astype(BF16)
    z = _dot(h, w_ref[...]) + b_ref[...]
    u_ref[...] = z[:, :d] * jax.nn.sigmoid(z[:, d:])


def pw1_glu(x, g, shift, scale, w, b, tm, tpg):
    m, d = x.shape
    return pl.pallas_call(
        _pw1_kernel,
        grid=(m // tm,),
        in_specs=[_row_spec(tm, d), _full_spec(g), _mod_spec(shift, tpg), _mod_spec(scale, tpg),
                  _full_spec(w), _full_spec(b)],
        out_specs=_row_spec(tm, d),
        out_shape=jax.ShapeDtypeStruct((m, d), F32),
        compiler_params=_cparams("parallel"),
        name="pw1_glu",
    )(x, g, shift, scale, w, b)


def _conv_tail(full, wdw_ref, bdw_ref, lg_ref, lb_ref, w2_ref, b2_ref, tm):
    acc = None
    for k in range(CONV_W):
        term = full[pl.ds(k + CONV_PAD - CONV_W + 1, tm), :] * wdw_ref[k:k + 1, :]
        acc = term if acc is None else acc + term
    acc = acc + bdw_ref[...]
    mu = jnp.mean(acc, axis=-1, keepdims=True)
    cen = acc - mu
    var = jnp.mean(cen * cen, axis=-1, keepdims=True)
    yn = cen * lax.rsqrt(var + EPS) * lg_ref[...] + lb_ref[...]
    return _dot(_silu(yn).astype(BF16), w2_ref[...]) + b2_ref[...]


def _conv_kernel(fa_ref, fb_ref, wdw_ref, bdw_ref, lg_ref, lb_ref, w2_ref, b2_ref, x_ref, gate_ref,
                 o_ref, buf):
    tm = x_ref.shape[0]
    buf[0:tm, :] = fa_ref[0]
    buf[tm:tm + CONV_PAD, :] = fb_ref[0]
    out = _conv_tail(buf, wdw_ref, bdw_ref, lg_ref, lb_ref, w2_ref, b2_ref, tm)
    o_ref[...] = x_ref[...] + gate_ref[...] * out


def conv_pw2_residual(full, wdw, bdw, lg, lb, w2, b2, x, gate, tm, tpg):
    m, d = x.shape
    bsz, tp, _ = full.shape
    t = tp - CONV_PAD
    assert t % tm == 0 and tm % CONV_PAD == 0 and m == bsz * t
    tps = t // tm
    return pl.pallas_call(
        _conv_kernel,
        grid=(m // tm,),
        in_specs=[pl.BlockSpec((1, tm, d), lambda i: (i // tps, i % tps, 0)),
                  pl.BlockSpec((1, CONV_PAD, d),
                               lambda i: (i // tps, (i % tps + 1) * (tm // CONV_PAD), 0)),
                  _full_spec(wdw), _full_spec(bdw), _full_spec(lg), _full_spec(lb),
                  _full_spec(w2), _full_spec(b2), _row_spec(tm, d), _mod_spec(gate, tpg)],
        out_specs=_row_spec(tm, d),
        out_shape=jax.ShapeDtypeStruct((m, d), F32),
        scratch_shapes=[pltpu.VMEM((tm + CONV_PAD, d), F32)],
        compiler_params=_cparams("parallel"),
        name="conv_pw2_residual",
    )(full, full, wdw, bdw, lg, lb, w2, b2, x, gate)


def _conv_small_kernel(full_ref, wdw_ref, bdw_ref, lg_ref, lb_ref, w2_ref, b2_ref, x_ref, gate_ref,
                       o_ref):
    tm = x_ref.shape[1]
    out = _conv_tail(full_ref.at[0], wdw_ref, bdw_ref, lg_ref, lb_ref, w2_ref, b2_ref, tm)
    o_ref[0] = x_ref[0] + gate_ref[0] * out


def conv_pw2_residual_small(full, wdw, bdw, lg, lb, w2, b2, x, gate):
    bsz, ts, d = x.shape
    seq = lambda a: pl.BlockSpec((1,) + a.shape[1:], lambda i: (i, 0, 0))
    return pl.pallas_call(
        _conv_small_kernel,
        grid=(bsz,),
        in_specs=[seq(full), _full_spec(wdw), _full_spec(bdw), _full_spec(lg), _full_spec(lb),
                  _full_spec(w2), _full_spec(b2), seq(x), seq(gate)],
        out_specs=seq(x),
        out_shape=jax.ShapeDtypeStruct(x.shape, F32),
        compiler_params=_cparams("parallel"),
        name="conv_pw2_residual_small",
    )(full, wdw, bdw, lg, lb, w2, b2, x, gate)


def _ffn_kernel(x_ref, g_ref, sh_ref, sc_ref, gate_ref, wg_ref, wu_ref, wd_ref, *rest, final):
    x = x_ref[...]
    h = _norm_mod(x, g_ref[...], sh_ref[...], sc_ref[...]).astype(BF16)
    a = _dot(h, wg_ref[...])
    b = _dot(h, wu_ref[...])
    act = (_silu(a) * b).astype(BF16)
    xn = x + gate_ref[...] * _dot(act, wd_ref[...])
    if final:
        fg_ref, o_ref = rest
        ms = jnp.mean(xn * xn, axis=-1, keepdims=True)
        o_ref[...] = xn * lax.rsqrt(ms + EPS) * fg_ref[...]
    else:
        (o_ref,) = rest
        o_ref[...] = xn


def ffn(x, g, shift, scale, gate, wg, wu, wd, tm, tpg, final_g=None):
    m, d = x.shape
    final = final_g is not None
    ins = [x, g, shift, scale, gate, wg, wu, wd] + ([final_g] if final else [])
    specs = [_row_spec(tm, d), _full_spec(g), _mod_spec(shift, tpg), _mod_spec(scale, tpg),
             _mod_spec(gate, tpg), _full_spec(wg), _full_spec(wu), _full_spec(wd)]
    if final:
        specs.append(_full_spec(final_g))
    return pl.pallas_call(
        functools.partial(_ffn_kernel, final=final),
        grid=(m // tm,),
        in_specs=specs,
        out_specs=_row_spec(tm, d),
        out_shape=jax.ShapeDtypeStruct((m, d), F32),
        compiler_params=_cparams("parallel"),
        name="ffn_final" if final else "ffn",
    )(*ins)


def _kvproj_kernel(x_ref, g_ref, w_ref, o_ref):
    x = x_ref[...]
    ms = jnp.mean(x * x, axis=-1, keepdims=True)
    h = (x * lax.rsqrt(ms + EPS) * g_ref[...]).astype(BF16)
    o_ref[...] = _dot(h, w_ref[...])


def kv_proj(x, g, w, tm):
    m, d = x.shape
    n = w.shape[1]
    return pl.pallas_call(
        _kvproj_kernel,
        grid=(m // tm,),
        in_specs=[_row_spec(tm, d), _full_spec(g), _full_spec(w)],
        out_specs=_row_spec(tm, n),
        out_shape=jax.ShapeDtypeStruct((m, n), F32),
        compiler_params=_cparams("parallel"),
        name="kv_proj",
    )(x, g, w)


def _qg_kernel(x_ref, g_ref, sh_ref, sc_ref, w_ref, q_ref, gates_ref):
    nq = q_ref.shape[1]
    h = _norm_mod(x_ref[...], g_ref[...], sh_ref[...], sc_ref[...]).astype(BF16)
    z = _dot(h, w_ref[...])
    q_ref[...] = (z[:, :nq] * (HEAD_DIM ** -0.5)).astype(BF16)
    gates_ref[...] = jax.nn.sigmoid(z[:, nq:])


def qg_proj(x, g, shift, scale, w, tm, tpg):
    m, d = x.shape
    nq = N_HEADS * HEAD_DIM
    ng = w.shape[1] - nq
    return pl.pallas_call(
        _qg_kernel,
        grid=(m // tm,),
        in_specs=[_row_spec(tm, d), _full_spec(g), _mod_spec(shift, tpg), _mod_spec(scale, tpg),
                  _full_spec(w)],
        out_specs=[_row_spec(tm, nq), _row_spec(tm, ng)],
        out_shape=[jax.ShapeDtypeStruct((m, nq), BF16), jax.ShapeDtypeStruct((m, ng), F32)],
        compiler_params=_cparams("parallel"),
        name="qg_proj",
    )(x, g, shift, scale, w)


def _oproj_kernel(o_ref, w_ref, x_ref, gate_ref, out_ref):
    out_ref[...] = x_ref[...] + gate_ref[...] * _dot(o_ref[...], w_ref[...])


def o_proj_residual(o, w, x, gate, tm, tpg):
    m, d = x.shape
    return pl.pallas_call(
        _oproj_kernel,
        grid=(m // tm,),
        in_specs=[_row_spec(tm, o.shape[1]), _full_spec(w), _row_spec(tm, d), _mod_spec(gate, tpg)],
        out_specs=_row_spec(tm, d),
        out_shape=jax.ShapeDtypeStruct((m, d), F32),
        compiler_params=_cparams("parallel"),
        name="o_proj_residual",
    )(o, w, x, gate)


def _compress_kernel(x_ref, pe_ref, w1_ref, b1_ref, w2_ref, b2_ref, o_ref):
    for g in range(KV_HEADS):
        xin = (x_ref[0, 0, g] + pe_ref[0]).astype(BF16)
        hid = _silu(_dot(xin, w1_ref[0]) + b1_ref[0])
        o_ref[0, 0, g] = _dot(hid.astype(BF16), w2_ref[0]) + b2_ref[0]


def compress(x, pe_flat, w1, b1, w2, b2):
    bc, _, kv, n, f = x.shape
    per_e = lambda a: pl.BlockSpec((1,) + a.shape[1:], lambda b, e: (e,) + (0,) * (a.ndim - 1))
    return pl.pallas_call(
        _compress_kernel,
        grid=(bc, 2),
        in_specs=[pl.BlockSpec((1, 1, kv, n, f), lambda b, e: (b, e, 0, 0, 0)),
                  per_e(pe_flat), per_e(w1), per_e(b1), per_e(w2), per_e(b2)],
        out_specs=pl.BlockSpec((1, 1, kv, n, HEAD_DIM), lambda b, e: (b, e, 0, 0, 0)),
        out_shape=jax.ShapeDtypeStruct((bc, 2, kv, n, HEAD_DIM), F32),
        compiler_params=_cparams("parallel", "parallel"),
        name="compress",
    )(x, pe_flat, w1, b1, w2, b2)


GATHER_PAGES_PER_STEP = 8


def _gather_kernel(pt_ref, *refs):
    o_ref = refs[-1]
    page = refs[0].shape[1]
    for k, r in enumerate(refs[:-1]):
        o_ref[0, k * page:(k + 1) * page, :] = r[0]


def gather_pages(cache, page_table):
    n_pool, page, c = cache.shape
    db, n_pages = page_table.shape
    pps = GATHER_PAGES_PER_STEP
    assert n_pages % pps == 0

    def page_spec(k):
        return pl.BlockSpec((1, page, c), lambda b, j, pt: (pt[b, j * pps + k], 0, 0))

    return pl.pallas_call(
        _gather_kernel,
        grid_spec=pltpu.PrefetchScalarGridSpec(
            num_scalar_prefetch=1,
            grid=(db, n_pages // pps),
            in_specs=[page_spec(k) for k in range(pps)],
            out_specs=pl.BlockSpec((1, pps * page, c), lambda b, j, pt: (b, j, 0)),
        ),
        out_shape=jax.ShapeDtypeStruct((db, n_pages * page, c), cache.dtype),
        compiler_params=_cparams("parallel", "parallel"),
        name="gather_pages",
    )(page_table, *([cache] * pps))


def _rel_bucket(dist):
    n = jnp.maximum(dist, 0)
    exact = N_BUCKETS // 2
    nf = jnp.maximum(n, 1).astype(F32)
    large = exact + (jnp.log(nf / exact) / math.log(MAX_DISTANCE / exact) * (N_BUCKETS - exact)).astype(jnp.int32)
    large = jnp.minimum(large, N_BUCKETS - 1)
    return jnp.where(n < exact, n, large)


class _BiasTable:
    def __init__(self, table, max_dist, pad):
        bucket = _rel_bucket(jnp.arange(max_dist))
        tab = table.astype(F32)
        by_dist = jnp.zeros((tab.shape[1], max_dist), F32)
        for k in range(N_BUCKETS):
            by_dist = jnp.where(bucket[None, :] == k, tab[k][:, None], by_dist)
        self.pad = pad
        self.n = max_dist
        self.ext = jnp.concatenate([jnp.broadcast_to(by_dist[:, :1], (tab.shape[1], pad)), by_dist], axis=1)
        self.rev = self.ext[:, ::-1]

    def at(self, d):
        return self.ext[:, self.pad + d]

    def rising(self, d0, length):
        return lax.slice_in_dim(self.ext, self.pad + d0, self.pad + d0 + length, axis=1)

    def falling(self, head, d0, length, step=1):
        start = self.n - 1 - d0
        return lax.slice(self.rev[head], (start,), (start + step * (length - 1) + 1,), (step,))


def _softmax_step(s, vt, m, l, acc):
    m_new = jnp.maximum(m, jnp.max(s, axis=0, keepdims=True))
    alpha = jnp.exp(m - m_new)
    p = jnp.exp(s - m_new)
    l = alpha * l + jnp.sum(p, axis=0, keepdims=True)
    acc = alpha * acc + _dot(vt, p.astype(BF16))
    return m_new, l, acc


def _attn_kernel(bfar_ref, q_ref, gt_ref, kaug_ref, vst_ref, kw_ref, vwt_ref, ck_ref, cvt_ref,
                 bc_ref, bn_ref, o_ref, imp_ref, sc_ref, *, nb, nc):
    g = pl.program_id(0)
    qt = pl.program_id(2)
    qs = qt * QT
    gq = GROUP * QT

    lane_cat = lambda parts: jnp.concatenate(parts, axis=1)
    q_t = q_ref[0].astype(F32).T.astype(BF16)
    q_heads = [q_t[h * HEAD_DIM:(h + 1) * HEAD_DIM] for h in range(GROUP)]
    q_plain = lane_cat(q_heads)

    col_i = lax.broadcasted_iota(jnp.int32, (1, gq), 1) % QT

    s_c = _dot(ck_ref[0, 0], q_plain) + lane_cat([bc_ref[h, 0] for h in range(GROUP)])
    n_idx = lax.broadcasted_iota(jnp.int32, (nc, gq), 0)
    valid_c = (n_idx * CMP_BLOCK + CMP_BLOCK - 1) <= (qs + col_i)
    s_c = jnp.where(valid_c, s_c, M_INIT)
    m_c = jnp.max(s_c, axis=0, keepdims=True)
    p_c = jnp.where(valid_c, jnp.exp(s_c - m_c), 0.0)
    l_c = jnp.sum(p_c, axis=0, keepdims=True)
    p_c = p_c * jnp.where(l_c > 0.0, 1.0 / l_c, 0.0)
    o_c = _dot(cvt_ref[0, 0], p_c.astype(BF16))

    imp = p_c[:, 0:QT]
    for h in range(1, GROUP):
        imp = imp + p_c[:, h * QT:(h + 1) * QT]
    imp_ref[...] = imp
    ratio = SEL_BLOCK // CMP_BLOCK
    imp2 = imp_ref[pl.ds(0, nb, stride=ratio), :]
    for r in range(1, ratio):
        imp2 = imp2 + imp_ref[pl.ds(r, nb, stride=ratio), :]
    j_idx = lax.broadcasted_iota(jnp.int32, (nb, QT), 0)
    i_idx = lax.broadcasted_iota(jnp.int32, (nb, QT), 1)
    jq = (qs + i_idx) // SEL_BLOCK
    forced = (j_idx == 0) | (j_idx == jq) | (j_idx == jq - 1)
    score = jnp.where(j_idx <= jq, jnp.where(forced, FORCE, imp2), -1.0)
    sc_ref[...] = score

    def rank_body(c, cnt):
        row = sc_ref[pl.ds(c, 1), :]
        ge = jnp.where(row >= score, 1.0, 0.0)
        gt = jnp.where(row > score, 1.0, 0.0)
        return cnt + jnp.where(j_idx > c, ge, gt)

    cnt = lax.fori_loop(0, nb, rank_body, jnp.zeros((nb, QT), F32))
    selected = (cnt < float(min(N_SEL, nb))) & (score >= 0.0)
    mask_feat = jnp.where(selected, 0.0, MASK_NEG).astype(BF16)
    q_aug = lane_cat([jnp.concatenate([qh, mask_feat], axis=0) for qh in q_heads])

    far_row = lane_cat([jnp.full((1, QT), bfar_ref[g * GROUP + h], F32) for h in range(GROUP)])

    def near_bias(kt0, n_sub):
        parts = []
        for sub in range(n_sub):
            dm = jnp.clip(qt - (kt0 + sub), 0, NEAR_SPAN - 1)
            parts.append(lane_cat([bn_ref[h, dm] for h in range(GROUP)]))
        return jnp.concatenate(parts, axis=0)

    init = (jnp.full((1, gq), M_INIT, F32), jnp.zeros((1, gq), F32), jnp.zeros((HEAD_DIM, gq), F32))
    sub_per_chunk = KEY_STEP // QT
    row_k = lax.broadcasted_iota(jnp.int32, (KEY_STEP, gq), 0)

    def sel_scores(c):
        ks = pl.multiple_of(c * KEY_STEP, KEY_STEP)
        return _dot(kaug_ref[0, 0, pl.ds(ks, KEY_STEP), :], q_aug), ks

    def far_body(c, carry):
        s, ks = sel_scores(c)
        return _softmax_step(s + far_row, vst_ref[0, 0, :, pl.ds(ks, KEY_STEP)], *carry)

    def near_body(c, carry):
        s, ks = sel_scores(c)
        s = s + near_bias(c * sub_per_chunk, sub_per_chunk)
        s = jnp.where(ks + row_k <= qs + col_i, s, MASK_NEG)
        return _softmax_step(s, vst_ref[0, 0, :, pl.ds(ks, KEY_STEP)], *carry)

    n_far = jnp.maximum(qt - (NEAR_TILES - 1), 0) // sub_per_chunk
    carry = lax.fori_loop(0, n_far, far_body, init)
    m_s, l_s, acc_s = lax.fori_loop(n_far, qt // sub_per_chunk + 1, near_body, carry)
    o_s = acc_s / l_s

    wt0 = jnp.maximum(qt - WINDOW // QT, 0)
    ws = pl.multiple_of(wt0 * QT, QT)
    wk = WINDOW + QT
    s_w = _dot(kw_ref[0, 0, pl.ds(ws, wk), :], q_plain) + near_bias(wt0, wk // QT)
    dist = (qs + col_i) - (ws + lax.broadcasted_iota(jnp.int32, (wk, gq), 0))
    s_w = jnp.where((dist >= 0) & (dist < WINDOW), s_w, MASK_NEG)
    p_w = jnp.exp(s_w - jnp.max(s_w, axis=0, keepdims=True))
    l_w = jnp.sum(p_w, axis=0, keepdims=True)
    o_w = _dot(vwt_ref[0, 0, :, pl.ds(ws, wk)], p_w.astype(BF16)) / l_w

    gate = lambda br: lane_cat([gt_ref[0, 0, h * 3 + br:h * 3 + br + 1, :] for h in range(GROUP)])
    o_t = gate(0) * o_c + gate(1) * o_s + gate(2) * o_w
    o_rows = jnp.concatenate([o_t[:, h * QT:(h + 1) * QT] for h in range(GROUP)], axis=0)
    o_ref[0] = o_rows.T.astype(BF16)


def prompt_attention(q, gates_t, kaug, vst, kwin, vwt, ck, cvt, bias_cmp, bias_near, bias_far):
    bsz, t, dq = q.shape
    nb = t // SEL_BLOCK
    nc = t // CMP_BLOCK
    nq = t // QT
    gd = GROUP * HEAD_DIM
    per_bg = lambda a: pl.BlockSpec((1, 1) + a.shape[2:], lambda g, b, i, *_: (b, g, 0, 0))
    grid_spec = pltpu.PrefetchScalarGridSpec(
        num_scalar_prefetch=1,
        grid=(KV_HEADS, bsz, nq),
        in_specs=[pl.BlockSpec((1, QT, gd), lambda g, b, i, *_: (b, i, g)),
                  pl.BlockSpec((1, 1, 3 * GROUP, QT), lambda g, b, i, *_: (b, g, 0, i)),
                  per_bg(kaug), per_bg(vst), per_bg(kwin), per_bg(vwt), per_bg(ck), per_bg(cvt),
                  pl.BlockSpec((GROUP, 1, nc, QT), lambda g, b, i, *_: (g, i, 0, 0)),
                  pl.BlockSpec((GROUP, NEAR_SPAN, QT, QT), lambda g, b, i, *_: (g, 0, 0, 0))],
        out_specs=pl.BlockSpec((1, QT, gd), lambda g, b, i, *_: (b, i, g)),
        scratch_shapes=[pltpu.VMEM((nc, QT), F32), pltpu.VMEM((nb, QT), F32)],
    )
    return pl.pallas_call(
        functools.partial(_attn_kernel, nb=nb, nc=nc),
        grid_spec=grid_spec,
        out_shape=jax.ShapeDtypeStruct((bsz, t, dq), BF16),
        compiler_params=_cparams("parallel", "parallel", "parallel"),
        name="prompt_attention",
    )(bias_far, q, gates_t, kaug, vst, kwin, vwt, ck, cvt, bias_cmp, bias_near)


KEY_CHUNK = 1024
NEW_PAD = 128


def _diag_heads(full, row_g):
    out = None
    for g in range(KV_HEADS):
        part = jnp.where(row_g == g, full[:, g * HEAD_DIM:(g + 1) * HEAD_DIM], 0.0)
        out = part if out is None else out + part
    return out


def _sample_attn_kernel(q_ref, gates_ref, ksel_ref, nsel_ref, kwin_ref, nwin_ref, ck_ref, cv_ref,
                        bsel_ref, bwin_ref, bcmp_ref, e_ref, o_ref, s_scr, *, past, ts):
    kvd = KV_HEADS * HEAD_DIM
    rows = GROUP * KV_HEADS * ts
    nc = ck_ref.shape[1]
    q = q_ref[0]
    r_idx = lax.broadcasted_iota(jnp.int32, (rows, 1), 0)
    row_q = r_idx % ts
    row_g = (r_idx // ts) % KV_HEADS

    s_c = _dot_nt(q, ck_ref[0]) + bcmp_ref[...]
    p_c = jnp.exp(s_c - jnp.max(s_c, axis=1, keepdims=True))
    p_c = p_c / jnp.sum(p_c, axis=1, keepdims=True)
    o_c = _diag_heads(_dot(p_c.astype(BF16), cv_ref[0]), row_g)

    per = KV_HEADS * ts
    imp = p_c[0:per]
    for h in range(1, GROUP):
        imp = imp + p_c[h * per:(h + 1) * per]
    ratio = SEL_BLOCK // CMP_BLOCK
    imp2 = imp
    for r in range(1, ratio):
        imp2 = imp2 + pltpu.roll(imp, nc - r, axis=1)
    lane = lax.broadcasted_iota(jnp.int32, (per, nc), 1)
    blk = lane // ratio
    jq = past // SEL_BLOCK
    forced = (blk == 0) | (blk == jq - 1)
    score = jnp.where(lane % ratio == 0, jnp.where(forced, FORCE, imp2), -2.0)
    cnt = jnp.where(score < FORCE, 1.0, 0.0)
    for c in range(nc // ratio):
        col = score[:, c * ratio:c * ratio + 1]
        ge = jnp.where(col >= score, 1.0, 0.0)
        gt = jnp.where(col > score, 1.0, 0.0)
        cnt = cnt + jnp.where(lane > c * ratio, ge, gt)
    selected = (cnt < float(min(N_SEL, jq + 1))) & (score >= 0.0)
    mask_feat = jnp.where(selected, 0.0, MASK_NEG).astype(BF16)
    mask_feat = jnp.concatenate([mask_feat] * GROUP, axis=0)

    for c in range(past // KEY_CHUNK):
        sl = slice(c * KEY_CHUNK, (c + 1) * KEY_CHUNK)
        k = ksel_ref[0, sl, 0:kvd].astype(BF16)
        s_scr[:, sl] = _dot_nt(q, k) + bsel_ref[:, sl] + _dot(mask_feat, e_ref[:, sl])
    new_j = lax.broadcasted_iota(jnp.int32, (rows, NEW_PAD), 1)
    new_ok = new_j <= row_q
    s_new = _dot_nt(q, nsel_ref[0, :, 0:kvd].astype(BF16)) + bsel_ref[:, past:past + NEW_PAD]
    s_scr[:, past:past + NEW_PAD] = jnp.where(new_ok, s_new, MASK_NEG)
    s_all = s_scr[...]
    p_s = jnp.exp(s_all - jnp.max(s_all, axis=1, keepdims=True))
    l_s = jnp.sum(p_s, axis=1, keepdims=True)
    s_scr[...] = p_s
    acc = _dot(s_scr[:, past:past + NEW_PAD].astype(BF16), nsel_ref[0, :, kvd:2 * kvd].astype(BF16))
    for c in range(past // KEY_CHUNK):
        sl = slice(c * KEY_CHUNK, (c + 1) * KEY_CHUNK)
        acc = acc + _dot(s_scr[:, sl].astype(BF16), ksel_ref[0, sl, kvd:2 * kvd].astype(BF16))
    o_s = _diag_heads(acc, row_g) / l_s

    wlen = kwin_ref.shape[1]
    win_i = lax.broadcasted_iota(jnp.int32, (rows, wlen), 1)
    s_w1 = _dot_nt(q, kwin_ref[0, :, 0:kvd].astype(BF16)) + bwin_ref[:, 0:wlen]
    s_w1 = jnp.where(win_i + (WINDOW - wlen) > row_q, s_w1, MASK_NEG)
    s_w2 = _dot_nt(q, nwin_ref[0, :, 0:kvd].astype(BF16)) + bwin_ref[:, wlen:wlen + NEW_PAD]
    s_w2 = jnp.where(new_ok, s_w2, MASK_NEG)
    m_w = jnp.maximum(jnp.max(s_w1, axis=1, keepdims=True), jnp.max(s_w2, axis=1, keepdims=True))
    p_w1 = jnp.exp(s_w1 - m_w)
    p_w2 = jnp.exp(s_w2 - m_w)
    l_w = jnp.sum(p_w1, axis=1, keepdims=True) + jnp.sum(p_w2, axis=1, keepdims=True)
    acc_w = (_dot(p_w1.astype(BF16), kwin_ref[0, :, kvd:2 * kvd].astype(BF16))
             + _dot(p_w2.astype(BF16), nwin_ref[0, :, kvd:2 * kvd].astype(BF16)))
    o_w = _diag_heads(acc_w, row_g) / l_w

    gts = gates_ref[0]
    o_ref[0] = gts[:, 0:1] * o_c + gts[:, 1:2] * o_s + gts[:, 2:3] * o_w


def sample_attention(q_bd, gates, ksel, nsel, kwin, nwin, ck, cv, bsel, bwin, bcmp, e_mat, past, ts):
    db, rows, kvd = q_bd.shape
    seq = lambda a: pl.BlockSpec((1,) + a.shape[1:], lambda b: (b,) + (0,) * (a.ndim - 1))
    return pl.pallas_call(
        functools.partial(_sample_attn_kernel, past=past, ts=ts),
        grid=(db,),
        in_specs=[seq(q_bd), seq(gates), seq(ksel), seq(nsel), seq(kwin), seq(nwin), seq(ck), seq(cv),
                  _full_spec(bsel), _full_spec(bwin), _full_spec(bcmp), _full_spec(e_mat)],
        out_specs=pl.BlockSpec((1, rows, HEAD_DIM), lambda b: (b, 0, 0)),
        out_shape=jax.ShapeDtypeStruct((db, rows, HEAD_DIM), F32),
        scratch_shapes=[pltpu.VMEM((rows, past + NEW_PAD), F32)],
        compiler_params=_cparams("parallel"),
        name="sample_attention",
    )(q_bd, gates, ksel, nsel, kwin, nwin, ck, cv, bsel, bwin, bcmp, e_mat)


def _prep_weights(P):
    W = {}
    cast = lambda a: a.astype(BF16)
    for k in ("w_pw1", "w_pw2", "w_kv", "w_o", "w_gate", "w_up", "w_down", "w_phi1", "w_phi2"):
        W[k] = cast(P[k])
    nq = N_HEADS * HEAD_DIM
    wqg = P["w_qg"]
    pad = 128 - (wqg.shape[2] - nq)
    W["w_qg"] = cast(jnp.pad(wqg, ((0, 0), (0, 0), (0, pad))))
    return W


def _trunk(x, mods, is_prompt, caches, P, W, bias_dist):
    bsz, t, d = x.shape
    m = bsz * t
    depth = P["w_gate"].shape[0]
    n_a = P["w_pw1"].shape[0]
    if is_prompt:
        tm, tpg = 256, t // 256
    else:
        tm, tpg = m, 1
    x2 = x.reshape(m, d)
    row2 = lambda a: a.reshape(1, -1)
    conv_states = []
    y = None
    for l in range(depth):
        shift, scale, gate = mods[(l, 0)]
        g0 = row2(P["norm_g"][l, 0])
        if l < n_a:
            u = pw1_glu(x2, g0, shift, scale, W["w_pw1"][l], row2(P["b_pw1"][l]), tm, tpg)
            u3 = u.reshape(bsz, t, d)
            if is_prompt:
                hist = jnp.zeros((bsz, CONV_PAD, d), F32)
            else:
                hist = jnp.pad(caches["state_conv"][l], ((0, 0), (CONV_PAD - CONV_W + 1, 0), (0, 0)))
            full = jnp.concatenate([hist, u3], axis=1)
            conv_states.append(full[:, -(CONV_W - 1):])
            wdw = jnp.pad(P["w_dw"][l], ((0, CONV_PAD - CONV_W), (0, 0)))
            args = (wdw, row2(P["b_dw"][l]), row2(P["ln_g"][l]), row2(P["ln_b"][l]), W["w_pw2"][l],
                    row2(P["b_pw2"][l]))
            if is_prompt:
                x2 = conv_pw2_residual(full, *args, x2, gate, tm, tpg)
            else:
                x2 = conv_pw2_residual_small(full, *args, x2.reshape(bsz, t, d),
                                             gate.reshape(bsz, t, d)[:, 0:1]).reshape(m, d)
        else:
            if l == n_a:
                rows = kv_proj(x2, row2(P["g_kv"]), W["w_kv"], tm)
                rows6 = rows.reshape(bsz, t, 3, 2, KV_HEADS, HEAD_DIM)
                if is_prompt:
                    att = _prompt_attention_setup(rows6, P, W, bias_dist)
                    win_state = rows6[:, -min(WINDOW, t):, 2]
                else:
                    att = _sample_attention_setup(rows6, caches, P, W, bias_dist)
                    wl = caches["cache_kv_win"].shape[1]
                    win_state = jnp.concatenate([caches["cache_kv_win"], rows6[:, :, 2]], axis=1)[:, -wl:]
            lb = l - n_a
            q, gates = qg_proj(x2, g0, shift, scale, W["w_qg"][lb], tm, tpg)
            if is_prompt:
                o = _prompt_attention_layer(q.reshape(bsz, t, -1), gates.reshape(bsz, t, -1), att)
            else:
                o = _sample_attention_layer(q.reshape(bsz, t, -1), gates.reshape(bsz, t, -1), att)
            x2 = o_proj_residual(o.reshape(m, -1), W["w_o"][lb], x2, gate, tm, tpg)
        shift, scale, gate = mods[(l, 1)]
        fg = row2(P["final_g"]) if l == depth - 1 else None
        out = ffn(x2, row2(P["norm_g"][l, 1]), shift, scale, gate, W["w_gate"][l], W["w_up"][l],
                  W["w_down"][l], tm, tpg, final_g=fg)
        if l == depth - 1:
            y = out
        else:
            x2 = out
    return (y.reshape(bsz, t, d), rows6[:, :, 0], rows6[:, :, 1], win_state, jnp.stack(conv_states))


def _compress_inputs(P):
    pe_flat = P["pe_cmp"].reshape(2, 1, CMP_BLOCK * HEAD_DIM)
    return pe_flat, P["b_phi1"][:, None, :], P["b_phi2"][:, None, :]


def _prompt_attention_setup(rows6, P, W, bias_dist):
    bsz, t = rows6.shape[:2]
    nb = t // SEL_BLOCK
    nc = t // CMP_BLOCK
    nq = t // QT
    xc = rows6[:, :, 0].reshape(bsz, nc, CMP_BLOCK, 2, KV_HEADS, HEAD_DIM)
    xc = xc.transpose(0, 3, 4, 1, 2, 5).reshape(bsz, 2, KV_HEADS, nc, CMP_BLOCK * HEAD_DIM)
    pe_flat, b1, b2 = _compress_inputs(P)
    cmp = compress(xc, pe_flat, W["w_phi1"], b1, W["w_phi2"], b2)
    ck = cmp[:, 0].astype(BF16)
    cvt = cmp[:, 1].transpose(0, 1, 3, 2).astype(BF16)
    onehot = (jnp.arange(t)[:, None] // SEL_BLOCK == jnp.arange(nb)[None, :]).astype(BF16)
    ksel = rows6[:, :, 1, 0].transpose(0, 2, 1, 3).astype(BF16)
    kaug = jnp.concatenate([ksel, jnp.broadcast_to(onehot, (bsz, KV_HEADS, t, nb))], axis=-1)
    vst = rows6[:, :, 1, 1].transpose(0, 2, 3, 1).astype(BF16)
    kwin = rows6[:, :, 2, 0].transpose(0, 2, 1, 3).astype(BF16)
    vwt = rows6[:, :, 2, 1].transpose(0, 2, 3, 1).astype(BF16)
    nh = N_HEADS
    bias_cmp = jnp.stack([bias_dist.rising(-(n * CMP_BLOCK + CMP_BLOCK - 1), t).reshape(nh, nq, QT)
                          for n in range(nc)], axis=2)
    bias_near = jnp.stack([bias_dist.rising(-j, NEAR_SPAN * QT).reshape(nh, NEAR_SPAN, QT)
                           for j in range(QT)], axis=2)
    bias_far = bias_dist.at(NEAR_TILES * QT)
    return dict(kaug=kaug, vst=vst, kwin=kwin, vwt=vwt, ck=ck, cvt=cvt, bias_cmp=bias_cmp,
                bias_near=bias_near, bias_far=bias_far)


def _prompt_attention_layer(q, gates, att):
    bsz, t, _ = q.shape
    gates_t = gates[:, :, :3 * N_HEADS].reshape(bsz, t, KV_HEADS, 3 * GROUP).transpose(0, 2, 3, 1)
    return prompt_attention(q, gates_t, att["kaug"], att["vst"], att["kwin"], att["vwt"], att["ck"],
                            att["cvt"], att["bias_cmp"], att["bias_near"], att["bias_far"])


def _sample_attention_setup(rows6, caches, P, W, bias_dist):
    db, ts = rows6.shape[:2]
    pt = caches["page_table"]
    n_pages = pt.shape[1]
    page = caches["cache_kv_cmp"].shape[1]
    past = n_pages * page
    c = 2 * KV_HEADS * HEAD_DIM
    cmp_g = gather_pages(caches["cache_kv_cmp"].reshape(-1, page, c), pt)
    sel_g = gather_pages(caches["cache_kv_sel"].reshape(-1, page, c), pt)
    nc = past // CMP_BLOCK
    assert (past + ts) // CMP_BLOCK == nc and past % KEY_CHUNK == 0
    xc = cmp_g.reshape(db, nc, CMP_BLOCK, 2, KV_HEADS, HEAD_DIM)
    xc = xc.transpose(0, 3, 4, 1, 2, 5).reshape(db, 2, KV_HEADS, nc, CMP_BLOCK * HEAD_DIM)
    pe_flat, b1, b2 = _compress_inputs(P)
    cmp = compress(xc, pe_flat, W["w_phi1"], b1, W["w_phi2"], b2)
    ck = cmp[:, 0].transpose(0, 2, 1, 3).reshape(db, nc, KV_HEADS * HEAD_DIM).astype(BF16)
    cv = cmp[:, 1].transpose(0, 2, 1, 3).reshape(db, nc, KV_HEADS * HEAD_DIM).astype(BF16)
    pad_new = lambda a: jnp.pad(a.reshape(db, ts, c), ((0, 0), (0, NEW_PAD - ts), (0, 0)))
    nsel = pad_new(rows6[:, :, 1])
    nwin = pad_new(rows6[:, :, 2])
    kwin = caches["cache_kv_win"].reshape(db, -1, c)
    wlen = kwin.shape[1]
    rows = GROUP * KV_HEADS * ts
    head = lambda r: ((r // ts) % KV_HEADS) * GROUP + r // (KV_HEADS * ts)
    qoff = lambda r: r % ts
    by_row = lambda f: jnp.stack([f(head(r), qoff(r)) for r in range(rows)], axis=0)
    bsel = by_row(lambda h, q: bias_dist.falling(h, past + q, past + NEW_PAD))
    bwin = by_row(lambda h, q: jnp.concatenate([bias_dist.falling(h, wlen + q, wlen),
                                                bias_dist.falling(h, q, NEW_PAD)]))
    bcmp = by_row(lambda h, q: bias_dist.falling(h, past + q - (CMP_BLOCK - 1), nc, step=CMP_BLOCK))
    ratio = SEL_BLOCK // CMP_BLOCK
    lane_blk = np.where(np.arange(nc) % ratio == 0, np.arange(nc) // ratio, -1)
    e_mat = jnp.asarray(lane_blk[:, None] == (np.arange(past) // SEL_BLOCK)[None, :], BF16)
    return dict(ksel=sel_g, nsel=nsel, kwin=kwin, nwin=nwin, ck=ck, cv=cv, bsel=bsel, bwin=bwin,
                bcmp=bcmp, e_mat=e_mat, past=past)


def _sample_attention_layer(q, gates, att):
    db, ts, _ = q.shape
    rows = GROUP * KV_HEADS * ts
    q5 = q.reshape(db, ts, KV_HEADS, GROUP, HEAD_DIM).transpose(0, 3, 2, 1, 4)
    eye = jnp.eye(KV_HEADS, dtype=q.dtype)
    q_bd = (q5[:, :, :, :, None, :] * eye[None, None, :, None, :, None]).reshape(db, rows, KV_HEADS * HEAD_DIM)
    g5 = gates[:, :, :3 * N_HEADS].reshape(db, ts, KV_HEADS, GROUP, 3).transpose(0, 3, 2, 1, 4)
    g_rows = jnp.pad(g5.reshape(db, rows, 3), ((0, 0), (0, 0), (0, 125)))
    o = sample_attention(q_bd, g_rows, att["ksel"], att["nsel"], att["kwin"], att["nwin"], att["ck"],
                         att["cv"], att["bsel"], att["bwin"], att["bcmp"], att["e_mat"], att["past"], ts)
    o = o.reshape(db, GROUP, KV_HEADS, ts, HEAD_DIM).transpose(0, 3, 2, 1, 4)
    return o.reshape(db, ts, N_HEADS * HEAD_DIM).astype(BF16)


def kernel(x_prompt, x_sample, c_prompt, c_sample, cache_kv_cmp, cache_kv_sel, cache_kv_win, state_conv, page_table, w_ada, b_ada, norm_g, w_pw1, b_pw1, w_dw, b_dw, ln_g, ln_b, w_pw2, b_pw2, g_kv, w_kv, w_phi1, b_phi1, w_phi2, b_phi2, pe_cmp, w_qg, w_o, rel_table, w_gate, w_up, w_down, final_g):
    P = dict(w_ada=w_ada, b_ada=b_ada, norm_g=norm_g, w_pw1=w_pw1, b_pw1=b_pw1, w_dw=w_dw, b_dw=b_dw,
             ln_g=ln_g, ln_b=ln_b, w_pw2=w_pw2, b_pw2=b_pw2, g_kv=g_kv, w_kv=w_kv, w_phi1=w_phi1,
             b_phi1=b_phi1, w_phi2=w_phi2, b_phi2=b_phi2, pe_cmp=pe_cmp, w_qg=w_qg, w_o=w_o,
             rel_table=rel_table, w_gate=w_gate, w_up=w_up, w_down=w_down, final_g=final_g)
    W = _prep_weights(P)
    bp, tp, d = x_prompt.shape
    db, ts, _ = x_sample.shape
    depth = w_ada.shape[0]

    n_c = bp + db
    r_pad = -(-n_c // 8) * 8
    c_all = jnp.pad(jnp.concatenate([c_prompt, c_sample], axis=0), ((0, r_pad - n_c), (0, 0)))
    mod = ada_modulation(c_all, w_ada.reshape(depth * 2, d, 3 * d), b_ada.reshape(depth * 2, 1, 3 * d))
    mods_p, mods_s = {}, {}
    for l in range(depth):
        for j in range(2):
            parts = [mod[l * 2 + j, :, k * d:(k + 1) * d] for k in range(3)]
            mods_p[(l, j)] = tuple(p[:bp, None, :] for p in parts)
            mods_s[(l, j)] = tuple(jnp.repeat(p[bp:n_c], ts, axis=0)[None] for p in parts)

    past = page_table.shape[1] * cache_kv_cmp.shape[1]
    bias_dist = _BiasTable(rel_table, max(tp, past + 2 * NEW_PAD), pad=max(tp, NEW_PAD))
    caches = dict(cache_kv_cmp=cache_kv_cmp, cache_kv_sel=cache_kv_sel, cache_kv_win=cache_kv_win,
                  state_conv=state_conv, page_table=page_table)
    y_p, cmp_p, sel_p, win_p, conv_p = _trunk(x_prompt, mods_p, True, None, P, W, bias_dist)
    y_s, cmp_s, sel_s, win_s, conv_s = _trunk(x_sample, mods_s, False, caches, P, W, bias_dist)
    return (y_p, y_s, cmp_p, cmp_s, sel_p, sel_s, win_p, win_s, conv_p, conv_s)
```

```python
import functools
import math

import numpy as np
import jax
import jax.numpy as jnp
from jax import lax
from jax.experimental import pallas as pl
from jax.experimental.pallas import tpu as pltpu

F32 = jnp.float32
BF16 = jnp.bfloat16

N_HEADS = 16
KV_HEADS = 4
GROUP = N_HEADS // KV_HEADS
HEAD_DIM = 64
CONV_W = 31
CONV_PAD = 32
CMP_BLOCK = 32
SEL_BLOCK = 64
N_SEL = 16
WINDOW = 512
N_BUCKETS = 32
MAX_DISTANCE = 1024
EPS = 1e-6
FORCE = 1e4
MASK_NEG = -1e9
M_INIT = -1e30
QT = 128
NEAR_TILES = 8
KEY_STEP = 512
NEAR_SPAN = NEAR_TILES + KEY_STEP // QT - 1
V7X_VMEM_LIMIT_BYTES = 56 * 1024 * 1024


def _cparams(*sem):
    return pltpu.CompilerParams(dimension_semantics=sem, vmem_limit_bytes=V7X_VMEM_LIMIT_BYTES)


def _full_spec(a):
    nd = a.ndim
    return pl.BlockSpec(a.shape, lambda *_: (0,) * nd)


def _silu(x):
    return x * jax.nn.sigmoid(x)


def _norm_mod(x, g, shift, scale):
    ms = jnp.mean(x * x, axis=-1, keepdims=True)
    return x * lax.rsqrt(ms + EPS) * g * (1.0 + scale) + shift


def _dot(a, b):
    return jnp.dot(a, b, preferred_element_type=F32)


def _dot_nt(a, b):
    return lax.dot_general(a, b, (((1,), (1,)), ((), ())), preferred_element_type=F32)


def _ada_kernel(c_ref, w_ref, b_ref, o_ref):
    c = c_ref[...]
    o_ref[0] = _dot(_silu(c), w_ref[0]) + b_ref[0]


def ada_modulation(c_all, w_ada, b_ada):
    nl, d, n3 = w_ada.shape
    r = c_all.shape[0]
    tn = n3 // 2
    return pl.pallas_call(
        _ada_kernel,
        grid=(nl, n3 // tn),
        in_specs=[pl.BlockSpec((r, d), lambda l, j: (0, 0)),
                  pl.BlockSpec((1, d, tn), lambda l, j: (l, 0, j)),
                  pl.BlockSpec((1, 1, tn), lambda l, j: (l, 0, j))],
        out_specs=pl.BlockSpec((1, r, tn), lambda l, j: (l, 0, j)),
        out_shape=jax.ShapeDtypeStruct((nl, r, n3), F32),
        compiler_params=_cparams("parallel", "parallel"),
        name="ada_modulation",
    )(c_all, w_ada, b_ada)


def _row_spec(tm, width):
    return pl.BlockSpec((tm, width), lambda i: (i, 0))


def _mod_spec(mod, tiles_per_group):
    _, r, d = mod.shape
    return pl.BlockSpec((None, r, d), lambda i: (i // tiles_per_group, 0, 0))


def _pw1_kernel(x_ref, g_ref, sh_ref, sc_ref, w_ref, b_ref, u_ref):
    d = x_ref.shape[1]
    h = _norm_mod(x_ref[...], g_ref[...], sh_ref[...], sc_ref[...]).astype(BF16)
    z = _dot(h, w_ref[...]) + b_ref[...]
    u_ref[...] = z[:, :d] * jax.nn.sigmoid(z[:, d:])


def pw1_glu(x, g, shift, scale, w, b, tm, tpg):
    m, d = x.shape
    return pl.pallas_call(
        _pw1_kernel,
        grid=(m // tm,),
        in_specs=[_row_spec(tm, d), _full_spec(g), _mod_spec(shift, tpg), _mod_spec(scale, tpg),
                  _full_spec(w), _full_spec(b)],
        out_specs=_row_spec(tm, d),
        out_shape=jax.ShapeDtypeStruct((m, d), F32),
        compiler_params=_cparams("parallel"),
        name="pw1_glu",
    )(x, g, shift, scale, w, b)


def _conv_tail(full, wdw_ref, bdw_ref, lg_ref, lb_ref, w2_ref, b2_ref, tm):
    acc = None
    for k in range(CONV_W):
        term = full[pl.ds(k + CONV_PAD - CONV_W + 1, tm), :] * wdw_ref[k:k + 1, :]
        acc = term if acc is None else acc + term
    acc = acc + bdw_ref[...]
    mu = jnp.mean(acc, axis=-1, keepdims=True)
    cen = acc - mu
    var = jnp.mean(cen * cen, axis=-1, keepdims=True)
    yn = cen * lax.rsqrt(var + EPS) * lg_ref[...] + lb_ref[...]
    return _dot(_silu(yn).astype(BF16), w2_ref[...]) + b2_ref[...]


def _conv_kernel(fa_ref, fb_ref, wdw_ref, bdw_ref, lg_ref, lb_ref, w2_ref, b2_ref, x_ref, gate_ref,
                 o_ref, buf):
    tm = x_ref.shape[0]
    buf[0:tm, :] = fa_ref[0]
    buf[tm:tm + CONV_PAD, :] = fb_ref[0]
    out = _conv_tail(buf, wdw_ref, bdw_ref, lg_ref, lb_ref, w2_ref, b2_ref, tm)
    o_ref[...] = x_ref[...] + gate_ref[...] * out


def conv_pw2_residual(full, wdw, bdw, lg, lb, w2, b2, x, gate, tm, tpg):
    m, d = x.shape
    bsz, tp, _ = full.shape
    t = tp - CONV_PAD
    assert t % tm == 0 and tm % CONV_PAD == 0 and m == bsz * t
    tps = t // tm
    return pl.pallas_call(
        _conv_kernel,
        grid=(m // tm,),
        in_specs=[pl.BlockSpec((1, tm, d), lambda i: (i // tps, i % tps, 0)),
                  pl.BlockSpec((1, CONV_PAD, d),
                               lambda i: (i // tps, (i % tps + 1) * (tm // CONV_PAD), 0)),
                  _full_spec(wdw), _full_spec(bdw), _full_spec(lg), _full_spec(lb),
                  _full_spec(w2), _full_spec(b2), _row_spec(tm, d), _mod_spec(gate, tpg)],
        out_specs=_row_spec(tm, d),
        out_shape=jax.ShapeDtypeStruct((m, d), F32),
        scratch_shapes=[pltpu.VMEM((tm + CONV_PAD, d), F32)],
        compiler_params=_cparams("parallel"),
        name="conv_pw2_residual",
    )(full, full, wdw, bdw, lg, lb, w2, b2, x, gate)


def _conv_small_kernel(full_ref, wdw_ref, bdw_ref, lg_ref, lb_ref, w2_ref, b2_ref, x_ref, gate_ref,
                       o_ref):
    tm = x_ref.shape[1]
    out = _conv_tail(full_ref.at[0], wdw_ref, bdw_ref, lg_ref, lb_ref, w2_ref, b2_ref, tm)
    o_ref[0] = x_ref[0] + gate_ref[0] * out


def conv_pw2_residual_small(full, wdw, bdw, lg, lb, w2, b2, x, gate):
    bsz, ts, d = x.shape
    seq = lambda a: pl.BlockSpec((1,) + a.shape[1:], lambda i: (i, 0, 0))
    return pl.pallas_call(
        _conv_small_kernel,
        grid=(bsz,),
        in_specs=[seq(full), _full_spec(wdw), _full_spec(bdw), _full_spec(lg), _full_spec(lb),
                  _full_spec(w2), _full_spec(b2), seq(x), seq(gate)],
        out_specs=seq(x),
        out_shape=jax.ShapeDtypeStruct(x.shape, F32),
        compiler_params=_cparams("parallel"),
        name="conv_pw2_residual_small",
    )(full, wdw, bdw, lg, lb, w2, b2, x, gate)


def _ffn_kernel(x_ref, g_ref, sh_ref, sc_ref, gate_ref, wg_ref, wu_ref, wd_ref, *rest, final):
    x = x_ref[...]
    h = _norm_mod(x, g_ref[...], sh_ref[...], sc_ref[...]).astype(BF16)
    a = _dot(h, wg_ref[...])
    b = _dot(h, wu_ref[...])
    act = (_silu(a) * b).astype(BF16)
    xn = x + gate_ref[...] * _dot(act, wd_ref[...])
    if final:
        fg_ref, o_ref = rest
        ms = jnp.mean(xn * xn, axis=-1, keepdims=True)
        o_ref[...] = xn * lax.rsqrt(ms + EPS) * fg_ref[...]
    else:
        (o_ref,) = rest
        o_ref[...] = xn


def ffn(x, g, shift, scale, gate, wg, wu, wd, tm, tpg, final_g=None):
    m, d = x.shape
    final = final_g is not None
    ins = [x, g, shift, scale, gate, wg, wu, wd] + ([final_g] if final else [])
    specs = [_row_spec(tm, d), _full_spec(g), _mod_spec(shift, tpg), _mod_spec(scale, tpg),
             _mod_spec(gate, tpg), _full_spec(wg), _full_spec(wu), _full_spec(wd)]
    if final:
        specs.append(_full_spec(final_g))
    return pl.pallas_call(
        functools.partial(_ffn_kernel, final=final),
        grid=(m // tm,),
        in_specs=specs,
        out_specs=_row_spec(tm, d),
        out_shape=jax.ShapeDtypeStruct((m, d), F32),
        compiler_params=_cparams("parallel"),
        name="ffn_final" if final else "ffn",
    )(*ins)


def _kvproj_kernel(x_ref, g_ref, w_ref, o_ref):
    x = x_ref[...]
    ms = jnp.mean(x * x, axis=-1, keepdims=True)
    h = (x * lax.rsqrt(ms + EPS) * g_ref[...]).astype(BF16)
    o_ref[...] = _dot(h, w_ref[...])


def kv_proj(x, g, w, tm):
    m, d = x.shape
    n = w.shape[1]
    return pl.pallas_call(
        _kvproj_kernel,
        grid=(m // tm,),
        in_specs=[_row_spec(tm, d), _full_spec(g), _full_spec(w)],
        out_specs=_row_spec(tm, n),
        out_shape=jax.ShapeDtypeStruct((m, n), F32),
        compiler_params=_cparams("parallel"),
        name="kv_proj",
    )(x, g, w)


def _qg_kernel(x_ref, g_ref, sh_ref, sc_ref, w_ref, q_ref, gates_ref):
    nq = q_ref.shape[1]
    h = _norm_mod(x_ref[...], g_ref[...], sh_ref[...], sc_ref[...]).astype(BF16)
    z = _dot(h, w_ref[...])
    q_ref[...] = (z[:, :nq] * (HEAD_DIM ** -0.5)).astype(BF16)
    gates_ref[...] = jax.nn.sigmoid(z[:, nq:])


def qg_proj(x, g, shift, scale, w, tm, tpg):
    m, d = x.shape
    nq = N_HEADS * HEAD_DIM
    ng = w.shape[1] - nq
    return pl.pallas_call(
        _qg_kernel,
        grid=(m // tm,),
        in_specs=[_row_spec(tm, d), _full_spec(g), _mod_spec(shift, tpg), _mod_spec(scale, tpg),
                  _full_spec(w)],
        out_specs=[_row_spec(tm, nq), _row_spec(tm, ng)],
        out_shape=[jax.ShapeDtypeStruct((m, nq), BF16), jax.ShapeDtypeStruct((m, ng), F32)],
        compiler_params=_cparams("parallel"),
        name="qg_proj",
    )(x, g, shift, scale, w)


def _oproj_kernel(o_ref, w_ref, x_ref, gate_ref, out_ref):
    out_ref[...] = x_ref[...] + gate_ref[...] * _dot(o_ref[...], w_ref[...])


def o_proj_residual(o, w, x, gate, tm, tpg):
    m, d = x.shape
    return pl.pallas_call(
        _oproj_kernel,
        grid=(m // tm,),
        in_specs=[_row_spec(tm, o.shape[1]), _full_spec(w), _row_spec(tm, d), _mod_spec(gate, tpg)],
        out_specs=_row_spec(tm, d),
        out_shape=jax.ShapeDtypeStruct((m, d), F32),
        compiler_params=_cparams("parallel"),
        name="o_proj_residual",
    )(o, w, x, gate)


SLAB = 128
HEADS_PER_SLAB = SLAB // HEAD_DIM


def _compress_kernel(x_ref, pe_ref, w1_ref, b1_ref, w2_ref, b2_ref, o_ref, *, nblk, pitch):
    hidden = w2_ref.shape[1]
    acc = None
    for r in range(CMP_BLOCK):
        xr = (x_ref[0, pl.ds(r, nblk, stride=pitch), :] + pe_ref[0, r:r + 1, :]).astype(BF16)
        term = _dot(xr, w1_ref[0, r])
        acc = term if acc is None else acc + term
    hid = _silu(acc + b1_ref[0])
    for hh in range(HEADS_PER_SLAB):
        o_ref[0, 0, hh] = _dot(hid[:, hh * hidden:(hh + 1) * hidden].astype(BF16), w2_ref[0]) + b2_ref[0]


def compress(x, nblk, pitch, pe_slab, w1_bd, b1_slab, w2, b2):
    bc, rows, _ = x.shape
    n_slab = 2 * KV_HEADS // HEADS_PER_SLAB
    slabs_per_e = n_slab // 2
    per_e = lambda a: pl.BlockSpec((1,) + a.shape[1:], lambda b, s: (s // slabs_per_e,) + (0,) * (a.ndim - 1))
    return pl.pallas_call(
        functools.partial(_compress_kernel, nblk=nblk, pitch=pitch),
        grid=(bc, n_slab),
        in_specs=[pl.BlockSpec((1, rows, SLAB), lambda b, s: (b, 0, s)),
                  pl.BlockSpec((1, CMP_BLOCK, SLAB), lambda b, s: (s, 0, 0)),
                  per_e(w1_bd), per_e(b1_slab), per_e(w2), per_e(b2)],
        out_specs=pl.BlockSpec((1, 1, HEADS_PER_SLAB, nblk, HEAD_DIM),
                               lambda b, s: (b, s // slabs_per_e, s % slabs_per_e, 0, 0)),
        out_shape=jax.ShapeDtypeStruct((bc, 2, KV_HEADS, nblk, HEAD_DIM), F32),
        compiler_params=_cparams("parallel", "parallel"),
        name="compress",
    )(x, pe_slab, w1_bd, b1_slab, w2, b2)


GATHER_PAGES_PER_STEP = 8
CMP_PITCH = 40


def _gather_kernel(pt_ref, *refs, pitch):
    o_ref = refs[-1]
    page = refs[0].shape[1]
    for k, r in enumerate(refs[:-1]):
        if pitch == CMP_BLOCK:
            o_ref[0, k * page:(k + 1) * page, :] = r[0]
        else:
            for blk in range(page // CMP_BLOCK):
                base = (k * (page // CMP_BLOCK) + blk) * pitch
                o_ref[0, base:base + CMP_BLOCK, :] = r[0, blk * CMP_BLOCK:(blk + 1) * CMP_BLOCK, :]
                o_ref[0, base + CMP_BLOCK:base + pitch, :] = jnp.zeros((pitch - CMP_BLOCK, o_ref.shape[2]),
                                                                        o_ref.dtype)


def gather_pages(cache, page_table, pitch=CMP_BLOCK):
    n_pool, page, c = cache.shape
    db, n_pages = page_table.shape
    pps = GATHER_PAGES_PER_STEP
    assert n_pages % pps == 0 and page % CMP_BLOCK == 0
    out_page = page // CMP_BLOCK * pitch

    def page_spec(k):
        return pl.BlockSpec((1, page, c), lambda b, j, pt: (pt[b, j * pps + k], 0, 0))

    return pl.pallas_call(
        functools.partial(_gather_kernel, pitch=pitch),
        grid_spec=pltpu.PrefetchScalarGridSpec(
            num_scalar_prefetch=1,
            grid=(db, n_pages // pps),
            in_specs=[page_spec(k) for k in range(pps)],
            out_specs=pl.BlockSpec((1, pps * out_page, c), lambda b, j, pt: (b, j, 0)),
        ),
        out_shape=jax.ShapeDtypeStruct((db, n_pages * out_page, c), cache.dtype),
        compiler_params=_cparams("parallel", "parallel"),
        name="gather_pages",
    )(page_table, *([cache] * pps))


def _rel_bucket(dist):
    n = jnp.maximum(dist, 0)
    exact = N_BUCKETS // 2
    nf = jnp.maximum(n, 1).astype(F32)
    large = exact + (jnp.log(nf / exact) / math.log(MAX_DISTANCE / exact) * (N_BUCKETS - exact)).astype(jnp.int32)
    large = jnp.minimum(large, N_BUCKETS - 1)
    return jnp.where(n < exact, n, large)


class _BiasTable:
    def __init__(self, table, max_dist, pad):
        bucket = _rel_bucket(jnp.arange(max_dist))
        tab = table.astype(F32)
        by_dist = jnp.zeros((tab.shape[1], max_dist), F32)
        for k in range(N_BUCKETS):
            by_dist = jnp.where(bucket[None, :] == k, tab[k][:, None], by_dist)
        self.pad = pad
        self.n = max_dist
        self.ext = jnp.concatenate([jnp.broadcast_to(by_dist[:, :1], (tab.shape[1], pad)), by_dist], axis=1)
        self.rev = self.ext[:, ::-1]

    def at(self, d):
        return self.ext[:, self.pad + d]

    def rising(self, d0, length):
        return lax.slice_in_dim(self.ext, self.pad + d0, self.pad + d0 + length, axis=1)

    def falling(self, d0, length, step=1):
        start = self.n - 1 - d0
        return lax.slice(self.rev, (0, start), (self.rev.shape[0], start + step * (length - 1) + 1), (1, step))

    def frames(self, d0, count, hop, length):
        reps = length // hop
        seq = self.rising(d0, hop * (count + reps - 1)).reshape(-1, count + reps - 1, hop)
        return jnp.concatenate([seq[:, a:a + count] for a in range(reps)], axis=-1)

    def toeplitz(self, d0, count, hop, size):
        span = 2 * size - 1
        base = self.rising(d0 - (size - 1), hop * (count + 1)).reshape(-1, count + 1, hop)
        w = jnp.concatenate([base[:, :count], base[:, 1:, :span - hop]], axis=-1)
        hank = jnp.tile(w, (1, 1, size + 1))[:, :, :size * (span + 1)]
        hank = hank.reshape(w.shape[0], count, size, span + 1)[..., :size]
        return hank[:, :, ::-1, :]


def _softmax_step(s, vt, m, l, acc):
    m_new = jnp.maximum(m, jnp.max(s, axis=0, keepdims=True))
    alpha = jnp.exp(m - m_new)
    p = jnp.exp(s - m_new)
    l = alpha * l + jnp.sum(p, axis=0, keepdims=True)
    acc = alpha * acc + _dot(vt, p.astype(BF16))
    return m_new, l, acc


def _attn_kernel(bfar_ref, q_ref, gt_ref, kaug_ref, vst_ref, kw_ref, vwt_ref, ck_ref, cvt_ref,
                 bc_ref, bn_ref, o_ref, imp_ref, sc_ref, s_ref, *, nb, nc):
    g = pl.program_id(0)
    qt = pl.program_id(2)
    qs = qt * QT
    gq = GROUP * QT

    lane_cat = lambda parts: jnp.concatenate(parts, axis=1)
    q_t = q_ref[0].astype(F32).T.astype(BF16)
    q_heads = [q_t[h * HEAD_DIM:(h + 1) * HEAD_DIM] for h in range(GROUP)]
    q_plain = lane_cat(q_heads)

    col_i = lax.broadcasted_iota(jnp.int32, (1, gq), 1) % QT

    s_c = _dot(ck_ref[0, 0], q_plain) + lane_cat([bc_ref[h, 0] for h in range(GROUP)])
    n_idx = lax.broadcasted_iota(jnp.int32, (nc, gq), 0)
    valid_c = (n_idx * CMP_BLOCK + CMP_BLOCK - 1) <= (qs + col_i)
    s_c = jnp.where(valid_c, s_c, M_INIT)
    m_c = jnp.max(s_c, axis=0, keepdims=True)
    p_c = jnp.where(valid_c, jnp.exp(s_c - m_c), 0.0)
    l_c = jnp.sum(p_c, axis=0, keepdims=True)
    p_c = p_c * jnp.where(l_c > 0.0, 1.0 / l_c, 0.0)
    o_c = _dot(cvt_ref[0, 0], p_c.astype(BF16))

    imp = p_c[:, 0:QT]
    for h in range(1, GROUP):
        imp = imp + p_c[:, h * QT:(h + 1) * QT]
    imp_ref[...] = imp
    ratio = SEL_BLOCK // CMP_BLOCK
    imp2 = imp_ref[pl.ds(0, nb, stride=ratio), :]
    for r in range(1, ratio):
        imp2 = imp2 + imp_ref[pl.ds(r, nb, stride=ratio), :]
    j_idx = lax.broadcasted_iota(jnp.int32, (nb, QT), 0)
    i_idx = lax.broadcasted_iota(jnp.int32, (nb, QT), 1)
    jq = (qs + i_idx) // SEL_BLOCK
    forced = (j_idx == 0) | (j_idx == jq) | (j_idx == jq - 1)
    score = jnp.where(j_idx <= jq, jnp.where(forced, FORCE, imp2), -1.0)
    sc_ref[...] = score

    sub8 = lax.broadcasted_iota(jnp.int32, (8, QT), 0)
    groups = [score[8 * r:8 * r + 8] for r in range(nb // 8)]
    counts = [jnp.zeros((8, QT), F32) for _ in groups]
    for c in range(nb):
        row = sc_ref[c:c + 1, :]
        for r, grp_score in enumerate(groups):
            if r > c // 8:
                beats = row >= grp_score
            elif r < c // 8:
                beats = row > grp_score
            else:
                tie = jnp.where(row == grp_score, 1.0, 0.0) * jnp.where(sub8 > c % 8, 1.0, 0.0)
                counts[r] = counts[r] + tie
                beats = row > grp_score
            counts[r] = counts[r] + jnp.where(beats, 1.0, 0.0)
    cnt = jnp.concatenate(counts, axis=0)
    selected = (cnt < float(min(N_SEL, nb))) & (score >= 0.0)
    mask_feat = jnp.where(selected, 0.0, MASK_NEG).astype(BF16)
    q_aug = lane_cat([jnp.concatenate([qh, mask_feat], axis=0) for qh in q_heads])

    far_row = lane_cat([jnp.full((1, QT), bfar_ref[g * GROUP + h], F32) for h in range(GROUP)])

    def near_bias(kt0, n_sub):
        parts = []
        for sub in range(n_sub):
            dm = jnp.clip(qt - (kt0 + sub), 0, NEAR_SPAN - 1)
            parts.append(lane_cat([bn_ref[h, dm] for h in range(GROUP)]))
        return jnp.concatenate(parts, axis=0)

    init = (jnp.full((1, gq), M_INIT, F32), jnp.zeros((1, gq), F32), jnp.zeros((HEAD_DIM, gq), F32))
    sub_per_chunk = KEY_STEP // QT
    row_k = lax.broadcasted_iota(jnp.int32, (KEY_STEP, gq), 0)

    last_chunk = qt // sub_per_chunk

    def raw_scores(c):
        ks = pl.multiple_of(jnp.minimum(c, last_chunk) * KEY_STEP, KEY_STEP)
        return _dot(kaug_ref[0, 0, pl.ds(ks, KEY_STEP), :], q_aug)

    def pipelined(finish):
        def body(c, carry):
            nxt = raw_scores(c + 1)
            cur = s_ref[c % 2]
            s_ref[(c + 1) % 2] = nxt
            ks = pl.multiple_of(c * KEY_STEP, KEY_STEP)
            return _softmax_step(finish(c, cur, ks), vst_ref[0, 0, :, pl.ds(ks, KEY_STEP)], *carry)
        return body

    def far_finish(c, s, ks):
        return s + far_row

    def near_finish(c, s, ks):
        s = s + near_bias(c * sub_per_chunk, sub_per_chunk)
        return jnp.where(ks + row_k <= qs + col_i, s, MASK_NEG)

    s_ref[0] = raw_scores(0)
    n_far = jnp.maximum(qt - (NEAR_TILES - 1), 0) // sub_per_chunk
    carry = lax.fori_loop(0, n_far, pipelined(far_finish), init)
    m_s, l_s, acc_s = lax.fori_loop(n_far, last_chunk + 1, pipelined(near_finish), carry)
    o_s = acc_s / l_s

    wt0 = jnp.maximum(qt - WINDOW // QT, 0)
    ws = pl.multiple_of(wt0 * QT, QT)
    wk = WINDOW + QT
    s_w = _dot(kw_ref[0, 0, pl.ds(ws, wk), :], q_plain) + near_bias(wt0, wk // QT)
    dist = (qs + col_i) - (ws + lax.broadcasted_iota(jnp.int32, (wk, gq), 0))
    s_w = jnp.where((dist >= 0) & (dist < WINDOW), s_w, MASK_NEG)
    p_w = jnp.exp(s_w - jnp.max(s_w, axis=0, keepdims=True))
    l_w = jnp.sum(p_w, axis=0, keepdims=True)
    o_w = _dot(vwt_ref[0, 0, :, pl.ds(ws, wk)], p_w.astype(BF16)) / l_w

    gate = lambda br: lane_cat([gt_ref[0, 0, h * 3 + br:h * 3 + br + 1, :] for h in range(GROUP)])
    o_t = gate(0) * o_c + gate(1) * o_s + gate(2) * o_w
    o_rows = jnp.concatenate([o_t[:, h * QT:(h + 1) * QT] for h in range(GROUP)], axis=0)
    o_ref[0] = o_rows.T.astype(BF16)


def prompt_attention(q, gates_t, kaug, vst, kwin, vwt, ck, cvt, bias_cmp, bias_near, bias_far):
    bsz, t, dq = q.shape
    nb = t // SEL_BLOCK
    nc = t // CMP_BLOCK
    nq = t // QT
    gd = GROUP * HEAD_DIM
    per_bg = lambda a: pl.BlockSpec((1, 1) + a.shape[2:], lambda g, b, i, *_: (b, g, 0, 0))
    grid_spec = pltpu.PrefetchScalarGridSpec(
        num_scalar_prefetch=1,
        grid=(KV_HEADS, bsz, nq),
        in_specs=[pl.BlockSpec((1, QT, gd), lambda g, b, i, *_: (b, i, g)),
                  pl.BlockSpec((1, 1, 3 * GROUP, QT), lambda g, b, i, *_: (b, g, 0, i)),
                  per_bg(kaug), per_bg(vst), per_bg(kwin), per_bg(vwt), per_bg(ck), per_bg(cvt),
                  pl.BlockSpec((GROUP, 1, nc, QT), lambda g, b, i, *_: (g, i, 0, 0)),
                  pl.BlockSpec((GROUP, NEAR_SPAN, QT, QT), lambda g, b, i, *_: (g, 0, 0, 0))],
        out_specs=pl.BlockSpec((1, QT, gd), lambda g, b, i, *_: (b, i, g)),
        scratch_shapes=[pltpu.VMEM((nc, QT), F32), pltpu.VMEM((nb, QT), F32),
                        pltpu.VMEM((2, KEY_STEP, GROUP * QT), F32)],
    )
    return pl.pallas_call(
        functools.partial(_attn_kernel, nb=nb, nc=nc),
        grid_spec=grid_spec,
        out_shape=jax.ShapeDtypeStruct((bsz, t, dq), BF16),
        compiler_params=_cparams("parallel", "parallel", "parallel"),
        name="prompt_attention",
    )(bias_far, q, gates_t, kaug, vst, kwin, vwt, ck, cvt, bias_cmp, bias_near)


KEY_CHUNK = 1024
NEW_PAD = 128


def _diag_heads(full, row_g):
    out = None
    for g in range(KV_HEADS):
        part = jnp.where(row_g == g, full[:, g * HEAD_DIM:(g + 1) * HEAD_DIM], 0.0)
        out = part if out is None else out + part
    return out


def _sample_attn_kernel(q_ref, gates_ref, ksel_ref, nsel_ref, kwin_ref, nwin_ref, ck_ref, cv_ref,
                        bsel_ref, bwin_ref, bcmp_ref, e_ref, o_ref, s_scr, *, past, ts):
    kvd = KV_HEADS * HEAD_DIM
    rows = GROUP * KV_HEADS * ts
    nc = ck_ref.shape[1]
    q = q_ref[0]
    r_idx = lax.broadcasted_iota(jnp.int32, (rows, 1), 0)
    row_q = r_idx % ts
    row_g = (r_idx // ts) % KV_HEADS

    s_c = _dot_nt(q, ck_ref[0]) + bcmp_ref[...]
    p_c = jnp.exp(s_c - jnp.max(s_c, axis=1, keepdims=True))
    p_c = p_c / jnp.sum(p_c, axis=1, keepdims=True)
    o_c = _diag_heads(_dot(p_c.astype(BF16), cv_ref[0]), row_g)

    per = KV_HEADS * ts
    imp = p_c[0:per]
    for h in range(1, GROUP):
        imp = imp + p_c[h * per:(h + 1) * per]
    ratio = SEL_BLOCK // CMP_BLOCK
    imp2 = imp
    for r in range(1, ratio):
        imp2 = imp2 + pltpu.roll(imp, nc - r, axis=1)
    lane = lax.broadcasted_iota(jnp.int32, (per, nc), 1)
    blk = lane // ratio
    jq = past // SEL_BLOCK
    forced = (blk == 0) | (blk == jq - 1)
    score = jnp.where(lane % ratio == 0, jnp.where(forced, FORCE, imp2), -2.0)
    cnt = jnp.where(score < FORCE, 1.0, 0.0)
    for c in range(nc // ratio):
        col = score[:, c * ratio:c * ratio + 1]
        ge = jnp.where(col >= score, 1.0, 0.0)
        gt = jnp.where(col > score, 1.0, 0.0)
        cnt = cnt + jnp.where(lane > c * ratio, ge, gt)
    selected = (cnt < float(min(N_SEL, jq + 1))) & (score >= 0.0)
    mask_feat = jnp.where(selected, 0.0, MASK_NEG).astype(BF16)
    mask_feat = jnp.concatenate([mask_feat] * GROUP, axis=0)

    for c in range(past // KEY_CHUNK):
        sl = slice(c * KEY_CHUNK, (c + 1) * KEY_CHUNK)
        k = ksel_ref[0, sl, 0:kvd].astype(BF16)
        s_scr[:, sl] = _dot_nt(q, k) + bsel_ref[:, sl] + _dot(mask_feat, e_ref[:, sl])
    new_j = lax.broadcasted_iota(jnp.int32, (rows, NEW_PAD), 1)
    new_ok = new_j <= row_q
    s_new = _dot_nt(q, nsel_ref[0, :, 0:kvd].astype(BF16)) + bsel_ref[:, past:past + NEW_PAD]
    s_scr[:, past:past + NEW_PAD] = jnp.where(new_ok, s_new, MASK_NEG)
    s_all = s_scr[...]
    p_s = jnp.exp(s_all - jnp.max(s_all, axis=1, keepdims=True))
    l_s = jnp.sum(p_s, axis=1, keepdims=True)
    s_scr[...] = p_s
    acc = _dot(s_scr[:, past:past + NEW_PAD].astype(BF16), nsel_ref[0, :, kvd:2 * kvd].astype(BF16))
    for c in range(past // KEY_CHUNK):
        sl = slice(c * KEY_CHUNK, (c + 1) * KEY_CHUNK)
        acc = acc + _dot(s_scr[:, sl].astype(BF16), ksel_ref[0, sl, kvd:2 * kvd].astype(BF16))
    o_s = _diag_heads(acc, row_g) / l_s

    wlen = kwin_ref.shape[1]
    win_i = lax.broadcasted_iota(jnp.int32, (rows, wlen), 1)
    s_w1 = _dot_nt(q, kwin_ref[0, :, 0:kvd].astype(BF16)) + bwin_ref[:, 0:wlen]
    s_w1 = jnp.where(win_i + (WINDOW - wlen) > row_q, s_w1, MASK_NEG)
    s_w2 = _dot_nt(q, nwin_ref[0, :, 0:kvd].astype(BF16)) + bwin_ref[:, wlen:wlen + NEW_PAD]
    s_w2 = jnp.where(new_ok, s_w2, MASK_NEG)
    m_w = jnp.maximum(jnp.max(s_w1, axis=1, keepdims=True), jnp.max(s_w2, axis=1, keepdims=True))
    p_w1 = jnp.exp(s_w1 - m_w)
    p_w2 = jnp.exp(s_w2 - m_w)
    l_w = jnp.sum(p_w1, axis=1, keepdims=True) + jnp.sum(p_w2, axis=1, keepdims=True)
    acc_w = (_dot(p_w1.astype(BF16), kwin_ref[0, :, kvd:2 * kvd].astype(BF16))
             + _dot(p_w2.astype(BF16), nwin_ref[0, :, kvd:2 * kvd].astype(BF16)))
    o_w = _diag_heads(acc_w, row_g) / l_w

    gts = gates_ref[0]
    o_ref[0] = gts[:, 0:1] * o_c + gts[:, 1:2] * o_s + gts[:, 2:3] * o_w


def sample_attention(q_bd, gates, ksel, nsel, kwin, nwin, ck, cv, bsel, bwin, bcmp, e_mat, past, ts):
    db, rows, kvd = q_bd.shape
    seq = lambda a: pl.BlockSpec((1,) + a.shape[1:], lambda b: (b,) + (0,) * (a.ndim - 1))
    return pl.pallas_call(
        functools.partial(_sample_attn_kernel, past=past, ts=ts),
        grid=(db,),
        in_specs=[seq(q_bd), seq(gates), seq(ksel), seq(nsel), seq(kwin), seq(nwin), seq(ck), seq(cv),
                  _full_spec(bsel), _full_spec(bwin), _full_spec(bcmp), _full_spec(e_mat)],
        out_specs=pl.BlockSpec((1, rows, HEAD_DIM), lambda b: (b, 0, 0)),
        out_shape=jax.ShapeDtypeStruct((db, rows, HEAD_DIM), F32),
        scratch_shapes=[pltpu.VMEM((rows, past + NEW_PAD), F32)],
        compiler_params=_cparams("parallel"),
        name="sample_attention",
    )(q_bd, gates, ksel, nsel, kwin, nwin, ck, cv, bsel, bwin, bcmp, e_mat)


def _prep_weights(P):
    W = {}
    cast = lambda a: a.astype(BF16)
    for k in ("w_pw1", "w_pw2", "w_kv", "w_o", "w_gate", "w_up", "w_down", "w_phi1", "w_phi2"):
        W[k] = cast(P[k])
    nq = N_HEADS * HEAD_DIM
    wqg = P["w_qg"]
    pad = 128 - (wqg.shape[2] - nq)
    W["w_qg"] = cast(jnp.pad(wqg, ((0, 0), (0, 0), (0, pad))))
    return W


def _trunk(x, mods, is_prompt, caches, P, W, bias_dist):
    bsz, t, d = x.shape
    m = bsz * t
    depth = P["w_gate"].shape[0]
    n_a = P["w_pw1"].shape[0]
    if is_prompt:
        tm, tpg = 256, t // 256
    else:
        tm, tpg = m, 1
    x2 = x.reshape(m, d)
    row2 = lambda a: a.reshape(1, -1)
    conv_states = []
    y = None
    for l in range(depth):
        shift, scale, gate = mods[(l, 0)]
        g0 = row2(P["norm_g"][l, 0])
        if l < n_a:
            u = pw1_glu(x2, g0, shift, scale, W["w_pw1"][l], row2(P["b_pw1"][l]), tm, tpg)
            u3 = u.reshape(bsz, t, d)
            if is_prompt:
                hist = jnp.zeros((bsz, CONV_PAD, d), F32)
            else:
                hist = jnp.pad(caches["state_conv"][l], ((0, 0), (CONV_PAD - CONV_W + 1, 0), (0, 0)))
            full = jnp.concatenate([hist, u3], axis=1)
            conv_states.append(full[:, -(CONV_W - 1):])
            wdw = jnp.pad(P["w_dw"][l], ((0, CONV_PAD - CONV_W), (0, 0)))
            args = (wdw, row2(P["b_dw"][l]), row2(P["ln_g"][l]), row2(P["ln_b"][l]), W["w_pw2"][l],
                    row2(P["b_pw2"][l]))
            if is_prompt:
                x2 = conv_pw2_residual(full, *args, x2, gate, tm, tpg)
            else:
                x2 = conv_pw2_residual_small(full, *args, x2.reshape(bsz, t, d),
                                             gate.reshape(bsz, t, d)[:, 0:1]).reshape(m, d)
        else:
            if l == n_a:
                rows = kv_proj(x2, row2(P["g_kv"]), W["w_kv"], tm)
                rows6 = rows.reshape(bsz, t, 3, 2, KV_HEADS, HEAD_DIM)
                if is_prompt:
                    att = _prompt_attention_setup(rows6, P, W, bias_dist)
                    win_state = rows6[:, -min(WINDOW, t):, 2]
                else:
                    att = _sample_attention_setup(rows6, caches, P, W, bias_dist)
                    wl = caches["cache_kv_win"].shape[1]
                    win_state = jnp.concatenate([caches["cache_kv_win"], rows6[:, :, 2]], axis=1)[:, -wl:]
            lb = l - n_a
            q, gates = qg_proj(x2, g0, shift, scale, W["w_qg"][lb], tm, tpg)
            if is_prompt:
                o = _prompt_attention_layer(q.reshape(bsz, t, -1), gates.reshape(bsz, t, -1), att)
            else:
                o = _sample_attention_layer(q.reshape(bsz, t, -1), gates.reshape(bsz, t, -1), att)
            x2 = o_proj_residual(o.reshape(m, -1), W["w_o"][lb], x2, gate, tm, tpg)
        shift, scale, gate = mods[(l, 1)]
        fg = row2(P["final_g"]) if l == depth - 1 else None
        out = ffn(x2, row2(P["norm_g"][l, 1]), shift, scale, gate, W["w_gate"][l], W["w_up"][l],
                  W["w_down"][l], tm, tpg, final_g=fg)
        if l == depth - 1:
            y = out
        else:
            x2 = out
    return (y.reshape(bsz, t, d), rows6[:, :, 0], rows6[:, :, 1], win_state, jnp.stack(conv_states))


def _compress_weights(P, W):
    pe = P["pe_cmp"]
    pe_slab = jnp.repeat(jnp.tile(pe, (1, 1, HEADS_PER_SLAB)), KV_HEADS // HEADS_PER_SLAB, axis=0)
    w1 = W["w_phi1"].reshape(2, CMP_BLOCK, HEAD_DIM, -1)
    zero = jnp.zeros_like(w1)
    w1_bd = jnp.concatenate([jnp.concatenate([w1, zero], axis=-1), jnp.concatenate([zero, w1], axis=-1)],
                            axis=2)
    b1_slab = jnp.tile(P["b_phi1"], (1, HEADS_PER_SLAB))[:, None, :]
    return pe_slab, w1_bd, b1_slab, W["w_phi2"], P["b_phi2"][:, None, :]


def _prompt_attention_setup(rows6, P, W, bias_dist):
    bsz, t = rows6.shape[:2]
    nb = t // SEL_BLOCK
    nc = t // CMP_BLOCK
    nq = t // QT
    cmp = compress(rows6.reshape(bsz, t, -1), nc, CMP_BLOCK, *_compress_weights(P, W))
    ck = cmp[:, 0].astype(BF16)
    cvt = cmp[:, 1].transpose(0, 1, 3, 2).astype(BF16)
    onehot = (jnp.arange(t)[:, None] // SEL_BLOCK == jnp.arange(nb)[None, :]).astype(BF16)
    ksel = rows6[:, :, 1, 0].transpose(0, 2, 1, 3).astype(BF16)
    kaug = jnp.concatenate([ksel, jnp.broadcast_to(onehot, (bsz, KV_HEADS, t, nb))], axis=-1)
    vst = rows6[:, :, 1, 1].transpose(0, 2, 3, 1).astype(BF16)
    kwin = rows6[:, :, 2, 0].transpose(0, 2, 1, 3).astype(BF16)
    vwt = rows6[:, :, 2, 1].transpose(0, 2, 3, 1).astype(BF16)
    per_qt = QT // CMP_BLOCK
    n_u = nc + per_qt * (nq - 1)
    frames = bias_dist.frames(-(CMP_BLOCK - 1) - CMP_BLOCK * (nc - 1), n_u, CMP_BLOCK, QT)[:, ::-1]
    bias_cmp = jnp.stack([frames[:, per_qt * (nq - 1 - i):per_qt * (nq - 1 - i) + nc] for i in range(nq)],
                         axis=1)
    bias_near = bias_dist.toeplitz(0, NEAR_SPAN, QT, QT)
    bias_far = bias_dist.at(NEAR_TILES * QT)
    return dict(kaug=kaug, vst=vst, kwin=kwin, vwt=vwt, ck=ck, cvt=cvt, bias_cmp=bias_cmp,
                bias_near=bias_near, bias_far=bias_far)


def _prompt_attention_layer(q, gates, att):
    bsz, t, _ = q.shape
    gates_t = gates[:, :, :3 * N_HEADS].reshape(bsz, t, KV_HEADS, 3 * GROUP).transpose(0, 2, 3, 1)
    return prompt_attention(q, gates_t, att["kaug"], att["vst"], att["kwin"], att["vwt"], att["ck"],
                            att["cvt"], att["bias_cmp"], att["bias_near"], att["bias_far"])


def _sample_attention_setup(rows6, caches, P, W, bias_dist):
    db, ts = rows6.shape[:2]
    pt = caches["page_table"]
    n_pages = pt.shape[1]
    page = caches["cache_kv_cmp"].shape[1]
    past = n_pages * page
    c = 2 * KV_HEADS * HEAD_DIM
    cmp_g = gather_pages(caches["cache_kv_cmp"].reshape(-1, page, c), pt, pitch=CMP_PITCH)
    sel_g = gather_pages(caches["cache_kv_sel"].reshape(-1, page, c), pt)
    nc = past // CMP_BLOCK
    assert (past + ts) // CMP_BLOCK == nc and past % KEY_CHUNK == 0
    cmp = compress(cmp_g, nc, CMP_PITCH, *_compress_weights(P, W))
    ck = cmp[:, 0].transpose(0, 2, 1, 3).reshape(db, nc, KV_HEADS * HEAD_DIM).astype(BF16)
    cv = cmp[:, 1].transpose(0, 2, 1, 3).reshape(db, nc, KV_HEADS * HEAD_DIM).astype(BF16)
    pad_new = lambda a: jnp.pad(a.reshape(db, ts, c), ((0, 0), (0, NEW_PAD - ts), (0, 0)))
    nsel = pad_new(rows6[:, :, 1])
    nwin = pad_new(rows6[:, :, 2])
    kwin = caches["cache_kv_win"].reshape(db, -1, c)
    wlen = kwin.shape[1]
    rows = GROUP * KV_HEADS * ts
    def by_row(f):
        per_q = jnp.stack([f(q) for q in range(ts)], axis=0)
        per_q = per_q.reshape(ts, KV_HEADS, GROUP, -1).transpose(2, 1, 0, 3)
        return per_q.reshape(rows, -1)

    bsel = by_row(lambda q: bias_dist.falling(past + q, past + NEW_PAD))
    bwin = by_row(lambda q: jnp.concatenate([bias_dist.falling(wlen + q, wlen),
                                             bias_dist.falling(q, NEW_PAD)], axis=1))
    bcmp = by_row(lambda q: bias_dist.falling(past + q - (CMP_BLOCK - 1), nc, step=CMP_BLOCK))
    ratio = SEL_BLOCK // CMP_BLOCK
    lane_blk = np.where(np.arange(nc) % ratio == 0, np.arange(nc) // ratio, -1)
    e_mat = jnp.asarray(lane_blk[:, None] == (np.arange(past) // SEL_BLOCK)[None, :], BF16)
    return dict(ksel=sel_g, nsel=nsel, kwin=kwin, nwin=nwin, ck=ck, cv=cv, bsel=bsel, bwin=bwin,
                bcmp=bcmp, e_mat=e_mat, past=past)


def _sample_attention_layer(q, gates, att):
    db, ts, _ = q.shape
    rows = GROUP * KV_HEADS * ts
    q5 = q.reshape(db, ts, KV_HEADS, GROUP, HEAD_DIM).transpose(0, 3, 2, 1, 4)
    eye = jnp.eye(KV_HEADS, dtype=q.dtype)
    q_bd = (q5[:, :, :, :, None, :] * eye[None, None, :, None, :, None]).reshape(db, rows, KV_HEADS * HEAD_DIM)
    g5 = gates[:, :, :3 * N_HEADS].reshape(db, ts, KV_HEADS, GROUP, 3).transpose(0, 3, 2, 1, 4)
    g_rows = jnp.pad(g5.reshape(db, rows, 3), ((0, 0), (0, 0), (0, 125)))
    o = sample_attention(q_bd, g_rows, att["ksel"], att["nsel"], att["kwin"], att["nwin"], att["ck"],
                         att["cv"], att["bsel"], att["bwin"], att["bcmp"], att["e_mat"], att["past"], ts)
    o = o.reshape(db, GROUP, KV_HEADS, ts, HEAD_DIM).transpose(0, 3, 2, 1, 4)
    return o.reshape(db, ts, N_HEADS * HEAD_DIM).astype(BF16)


def kernel(x_prompt, x_sample, c_prompt, c_sample, cache_kv_cmp, cache_kv_sel, cache_kv_win, state_conv, page_table, w_ada, b_ada, norm_g, w_pw1, b_pw1, w_dw, b_dw, ln_g, ln_b, w_pw2, b_pw2, g_kv, w_kv, w_phi1, b_phi1, w_phi2, b_phi2, pe_cmp, w_qg, w_o, rel_table, w_gate, w_up, w_down, final_g):
    P = dict(w_ada=w_ada, b_ada=b_ada, norm_g=norm_g, w_pw1=w_pw1, b_pw1=b_pw1, w_dw=w_dw, b_dw=b_dw,
             ln_g=ln_g, ln_b=ln_b, w_pw2=w_pw2, b_pw2=b_pw2, g_kv=g_kv, w_kv=w_kv, w_phi1=w_phi1,
             b_phi1=b_phi1, w_phi2=w_phi2, b_phi2=b_phi2, pe_cmp=pe_cmp, w_qg=w_qg, w_o=w_o,
             rel_table=rel_table, w_gate=w_gate, w_up=w_up, w_down=w_down, final_g=final_g)
    W = _prep_weights(P)
    bp, tp, d = x_prompt.shape
    db, ts, _ = x_sample.shape
    depth = w_ada.shape[0]

    n_c = bp + db
    r_pad = -(-n_c // 8) * 8
    c_all = jnp.pad(jnp.concatenate([c_prompt, c_sample], axis=0), ((0, r_pad - n_c), (0, 0)))
    mod = ada_modulation(c_all, w_ada.reshape(depth * 2, d, 3 * d), b_ada.reshape(depth * 2, 1, 3 * d))
    mods_p, mods_s = {}, {}
    for l in range(depth):
        for j in range(2):
            parts = [mod[l * 2 + j, :, k * d:(k + 1) * d] for k in range(3)]
            mods_p[(l, j)] = tuple(p[:bp, None, :] for p in parts)
            mods_s[(l, j)] = tuple(jnp.repeat(p[bp:n_c], ts, axis=0)[None] for p in parts)

    past = page_table.shape[1] * cache_kv_cmp.shape[1]
    bias_dist = _BiasTable(rel_table, max(tp, past + 2 * NEW_PAD), pad=max(tp, NEW_PAD))
    caches = dict(cache_kv_cmp=cache_kv_cmp, cache_kv_sel=cache_kv_sel, cache_kv_win=cache_kv_win,
                  state_conv=state_conv, page_table=page_table)
    y_p, cmp_p, sel_p, win_p, conv_p = _trunk(x_prompt, mods_p, True, None, P, W, bias_dist)
    y_s, cmp_s, sel_s, win_s, conv_s = _trunk(x_sample, mods_s, False, caches, P, W, bias_dist)
    return (y_p, y_s, cmp_p, cmp_s, sel_p, sel_s, win_p, win_s, conv_p, conv_s)
```

```python
import functools
import math

import numpy as np
import jax
import jax.numpy as jnp
from jax import lax
from jax.experimental import pallas as pl
from jax.experimental.pallas import tpu as pltpu

F32 = jnp.float32
BF16 = jnp.bfloat16

N_HEADS = 16
KV_HEADS = 4
GROUP = N_HEADS // KV_HEADS
HEAD_DIM = 64
CONV_W = 31
CONV_PAD = 32
SUBLANES = 8
CMP_BLOCK = 32
SEL_BLOCK = 64
N_SEL = 16
WINDOW = 512
N_BUCKETS = 32
MAX_DISTANCE = 1024
EPS = 1e-6
FORCE = 1e4
MASK_NEG = -1e9
M_INIT = -1e30
QT = 128
NEAR_TILES = 8
KEY_STEP = 512
NEAR_SPAN = NEAR_TILES + KEY_STEP // QT - 1
V7X_VMEM_LIMIT_BYTES = 56 * 1024 * 1024


def _cparams(*sem):
    return pltpu.CompilerParams(dimension_semantics=sem, vmem_limit_bytes=V7X_VMEM_LIMIT_BYTES)


def _full_spec(a):
    nd = a.ndim
    return pl.BlockSpec(a.shape, lambda *_: (0,) * nd)


def _silu(x):
    return x * jax.nn.sigmoid(x)


def _norm_mod(x, g, shift, scale):
    ms = jnp.mean(x * x, axis=-1, keepdims=True)
    return x * lax.rsqrt(ms + EPS) * g * (1.0 + scale) + shift


def _dot(a, b):
    return jnp.dot(a, b, preferred_element_type=F32)


def _dot_nt(a, b):
    return lax.dot_general(a, b, (((1,), (1,)), ((), ())), preferred_element_type=F32)


def _ada_kernel(c_ref, w_ref, b_ref, o_ref):
    c = c_ref[...]
    o_ref[0] = _dot(_silu(c), w_ref[0]) + b_ref[0]


def ada_modulation(c_all, w_ada, b_ada):
    nl, d, n3 = w_ada.shape
    r = c_all.shape[0]
    tn = n3 // 2
    return pl.pallas_call(
        _ada_kernel,
        grid=(nl, n3 // tn),
        in_specs=[pl.BlockSpec((r, d), lambda l, j: (0, 0)),
                  pl.BlockSpec((1, d, tn), lambda l, j: (l, 0, j)),
                  pl.BlockSpec((1, 1, tn), lambda l, j: (l, 0, j))],
        out_specs=pl.BlockSpec((1, r, tn), lambda l, j: (l, 0, j)),
        out_shape=jax.ShapeDtypeStruct((nl, r, n3), F32),
        compiler_params=_cparams("parallel", "parallel"),
        name="ada_modulation",
    )(c_all, w_ada, b_ada)


def _row_spec(tm, width):
    return pl.BlockSpec((tm, width), lambda i: (i, 0))


def _mod_spec(mod, tiles_per_group):
    _, r, d = mod.shape
    return pl.BlockSpec((None, r, d), lambda i: (i // tiles_per_group, 0, 0))


def _pw1_kernel(x_ref, g_ref, sh_ref, sc_ref, w_ref, b_ref, u_ref):
    d = x_ref.shape[1]
    h = _norm_mod(x_ref[...], g_ref[...], sh_ref[...], sc_ref[...]).astype(BF16)
    z = _dot(h, w_ref[...]) + b_ref[...]
    u_ref[...] = z[:, :d] * jax.nn.sigmoid(z[:, d:])


def pw1_glu(x, g, shift, scale, w, b, tm, tpg):
    m, d = x.shape
    return pl.pallas_call(
        _pw1_kernel,
        grid=(m // tm,),
        in_specs=[_row_spec(tm, d), _full_spec(g), _mod_spec(shift, tpg), _mod_spec(scale, tpg),
                  _full_spec(w), _full_spec(b)],
        out_specs=_row_spec(tm, d),
        out_shape=jax.ShapeDtypeStruct((m, d), F32),
        compiler_params=_cparams("parallel"),
        name="pw1_glu",
    )(x, g, shift, scale, w, b)


def _conv_tail(rows_at, wdw_ref, bdw_ref, lg_ref, lb_ref, w2_ref, b2_ref):
    acc = None
    for k in range(CONV_W):
        term = rows_at(k + CONV_PAD - CONV_W + 1) * wdw_ref[k:k + 1, :]
        acc = term if acc is None else acc + term
    acc = acc + bdw_ref[...]
    mu = jnp.mean(acc, axis=-1, keepdims=True)
    cen = acc - mu
    var = jnp.mean(cen * cen, axis=-1, keepdims=True)
    yn = cen * lax.rsqrt(var + EPS) * lg_ref[...] + lb_ref[...]
    return _dot(_silu(yn).astype(BF16), w2_ref[...]) + b2_ref[...]


def _conv_kernel(fa_ref, fb_ref, wdw_ref, bdw_ref, lg_ref, lb_ref, w2_ref, b2_ref, x_ref, gate_ref,
                 o_ref, buf, shifted):
    tm = x_ref.shape[0]
    buf[0:tm, :] = fa_ref[0]
    buf[tm:tm + CONV_PAD, :] = fb_ref[0]
    span = tm + CONV_PAD - SUBLANES
    for s in range(1, SUBLANES):
        shifted[s - 1] = buf[pl.ds(s, span), :]

    def rows_at(o):
        a, s = divmod(o, SUBLANES)
        if s == 0:
            return buf[pl.ds(o, tm), :]
        return shifted[s - 1, pl.ds(a * SUBLANES, tm), :]

    out = _conv_tail(rows_at, wdw_ref, bdw_ref, lg_ref, lb_ref, w2_ref, b2_ref)
    o_ref[...] = x_ref[...] + gate_ref[...] * out


def conv_pw2_residual(full, wdw, bdw, lg, lb, w2, b2, x, gate, tm, tpg):
    m, d = x.shape
    bsz, tp, _ = full.shape
    t = tp - CONV_PAD
    assert t % tm == 0 and tm % CONV_PAD == 0 and m == bsz * t
    tps = t // tm
    return pl.pallas_call(
        _conv_kernel,
        grid=(m // tm,),
        in_specs=[pl.BlockSpec((1, tm, d), lambda i: (i // tps, i % tps, 0)),
                  pl.BlockSpec((1, CONV_PAD, d),
                               lambda i: (i // tps, (i % tps + 1) * (tm // CONV_PAD), 0)),
                  _full_spec(wdw), _full_spec(bdw), _full_spec(lg), _full_spec(lb),
                  _full_spec(w2), _full_spec(b2), _row_spec(tm, d), _mod_spec(gate, tpg)],
        out_specs=_row_spec(tm, d),
        out_shape=jax.ShapeDtypeStruct((m, d), F32),
        scratch_shapes=[pltpu.VMEM((tm + CONV_PAD, d), F32),
                        pltpu.VMEM((SUBLANES - 1, tm + CONV_PAD - SUBLANES, d), F32)],
        compiler_params=_cparams("parallel"),
        name="conv_pw2_residual",
    )(full, full, wdw, bdw, lg, lb, w2, b2, x, gate)


def _conv_small_kernel(full_ref, wdw_ref, bdw_ref, lg_ref, lb_ref, w2_ref, b2_ref, x_ref, gate_ref,
                       o_ref):
    tm = x_ref.shape[1]
    rows_at = lambda o: full_ref[0, pl.ds(o, tm), :]
    out = _conv_tail(rows_at, wdw_ref, bdw_ref, lg_ref, lb_ref, w2_ref, b2_ref)
    o_ref[0] = x_ref[0] + gate_ref[0] * out


def conv_pw2_residual_small(full, wdw, bdw, lg, lb, w2, b2, x, gate):
    bsz, ts, d = x.shape
    seq = lambda a: pl.BlockSpec((1,) + a.shape[1:], lambda i: (i, 0, 0))
    return pl.pallas_call(
        _conv_small_kernel,
        grid=(bsz,),
        in_specs=[seq(full), _full_spec(wdw), _full_spec(bdw), _full_spec(lg), _full_spec(lb),
                  _full_spec(w2), _full_spec(b2), seq(x), seq(gate)],
        out_specs=seq(x),
        out_shape=jax.ShapeDtypeStruct(x.shape, F32),
        compiler_params=_cparams("parallel"),
        name="conv_pw2_residual_small",
    )(full, wdw, bdw, lg, lb, w2, b2, x, gate)


def _ffn_kernel(x_ref, g_ref, sh_ref, sc_ref, gate_ref, wg_ref, wu_ref, wd_ref, *rest, final):
    x = x_ref[...]
    h = _norm_mod(x, g_ref[...], sh_ref[...], sc_ref[...]).astype(BF16)
    a = _dot(h, wg_ref[...])
    b = _dot(h, wu_ref[...])
    act = (_silu(a) * b).astype(BF16)
    xn = x + gate_ref[...] * _dot(act, wd_ref[...])
    if final:
        fg_ref, o_ref = rest
        ms = jnp.mean(xn * xn, axis=-1, keepdims=True)
        o_ref[...] = xn * lax.rsqrt(ms + EPS) * fg_ref[...]
    else:
        (o_ref,) = rest
        o_ref[...] = xn


def ffn(x, g, shift, scale, gate, wg, wu, wd, tm, tpg, final_g=None):
    m, d = x.shape
    final = final_g is not None
    ins = [x, g, shift, scale, gate, wg, wu, wd] + ([final_g] if final else [])
    specs = [_row_spec(tm, d), _full_spec(g), _mod_spec(shift, tpg), _mod_spec(scale, tpg),
             _mod_spec(gate, tpg), _full_spec(wg), _full_spec(wu), _full_spec(wd)]
    if final:
        specs.append(_full_spec(final_g))
    return pl.pallas_call(
        functools.partial(_ffn_kernel, final=final),
        grid=(m // tm,),
        in_specs=specs,
        out_specs=_row_spec(tm, d),
        out_shape=jax.ShapeDtypeStruct((m, d), F32),
        compiler_params=_cparams("parallel"),
        name="ffn_final" if final else "ffn",
    )(*ins)


def _kvproj_kernel(x_ref, g_ref, w_ref, o_ref):
    x = x_ref[...]
    ms = jnp.mean(x * x, axis=-1, keepdims=True)
    h = (x * lax.rsqrt(ms + EPS) * g_ref[...]).astype(BF16)
    o_ref[...] = _dot(h, w_ref[...])


def kv_proj(x, g, w, tm):
    m, d = x.shape
    n = w.shape[1]
    return pl.pallas_call(
        _kvproj_kernel,
        grid=(m // tm,),
        in_specs=[_row_spec(tm, d), _full_spec(g), _full_spec(w)],
        out_specs=_row_spec(tm, n),
        out_shape=jax.ShapeDtypeStruct((m, n), F32),
        compiler_params=_cparams("parallel"),
        name="kv_proj",
    )(x, g, w)


def _qg_kernel(x_ref, g_ref, sh_ref, sc_ref, w_ref, q_ref, gates_ref):
    nq = q_ref.shape[1]
    h = _norm_mod(x_ref[...], g_ref[...], sh_ref[...], sc_ref[...]).astype(BF16)
    z = _dot(h, w_ref[...])
    q_ref[...] = (z[:, :nq] * (HEAD_DIM ** -0.5)).astype(BF16)
    gates_ref[...] = jax.nn.sigmoid(z[:, nq:])


def qg_proj(x, g, shift, scale, w, tm, tpg):
    m, d = x.shape
    nq = N_HEADS * HEAD_DIM
    ng = w.shape[1] - nq
    return pl.pallas_call(
        _qg_kernel,
        grid=(m // tm,),
        in_specs=[_row_spec(tm, d), _full_spec(g), _mod_spec(shift, tpg), _mod_spec(scale, tpg),
                  _full_spec(w)],
        out_specs=[_row_spec(tm, nq), _row_spec(tm, ng)],
        out_shape=[jax.ShapeDtypeStruct((m, nq), BF16), jax.ShapeDtypeStruct((m, ng), F32)],
        compiler_params=_cparams("parallel"),
        name="qg_proj",
    )(x, g, shift, scale, w)


def _oproj_kernel(o_ref, w_ref, x_ref, gate_ref, out_ref):
    out_ref[...] = x_ref[...] + gate_ref[...] * _dot(o_ref[...], w_ref[...])


def o_proj_residual(o, w, x, gate, tm, tpg):
    m, d = x.shape
    return pl.pallas_call(
        _oproj_kernel,
        grid=(m // tm,),
        in_specs=[_row_spec(tm, o.shape[1]), _full_spec(w), _row_spec(tm, d), _mod_spec(gate, tpg)],
        out_specs=_row_spec(tm, d),
        out_shape=jax.ShapeDtypeStruct((m, d), F32),
        compiler_params=_cparams("parallel"),
        name="o_proj_residual",
    )(o, w, x, gate)


SLAB = 128
HEADS_PER_SLAB = SLAB // HEAD_DIM


def _compress_kernel(x_ref, pe_ref, w1_ref, b1_ref, w2_ref, b2_ref, o_ref, *, nblk, pitch):
    hidden = w2_ref.shape[1]
    acc = None
    for r in range(CMP_BLOCK):
        xr = (x_ref[0, pl.ds(r, nblk, stride=pitch), :] + pe_ref[0, r:r + 1, :]).astype(BF16)
        term = _dot(xr, w1_ref[0, r])
        acc = term if acc is None else acc + term
    hid = _silu(acc + b1_ref[0])
    for hh in range(HEADS_PER_SLAB):
        o_ref[0, 0, hh] = _dot(hid[:, hh * hidden:(hh + 1) * hidden].astype(BF16), w2_ref[0]) + b2_ref[0]


def compress(x, nblk, pitch, pe_slab, w1_bd, b1_slab, w2, b2):
    bc, rows, _ = x.shape
    n_slab = 2 * KV_HEADS // HEADS_PER_SLAB
    slabs_per_e = n_slab // 2
    per_e = lambda a: pl.BlockSpec((1,) + a.shape[1:], lambda b, s: (s // slabs_per_e,) + (0,) * (a.ndim - 1))
    return pl.pallas_call(
        functools.partial(_compress_kernel, nblk=nblk, pitch=pitch),
        grid=(bc, n_slab),
        in_specs=[pl.BlockSpec((1, rows, SLAB), lambda b, s: (b, 0, s)),
                  pl.BlockSpec((1, CMP_BLOCK, SLAB), lambda b, s: (s, 0, 0)),
                  per_e(w1_bd), per_e(b1_slab), per_e(w2), per_e(b2)],
        out_specs=pl.BlockSpec((1, 1, HEADS_PER_SLAB, nblk, HEAD_DIM),
                               lambda b, s: (b, s // slabs_per_e, s % slabs_per_e, 0, 0)),
        out_shape=jax.ShapeDtypeStruct((bc, 2, KV_HEADS, nblk, HEAD_DIM), F32),
        compiler_params=_cparams("parallel", "parallel"),
        name="compress",
    )(x, pe_slab, w1_bd, b1_slab, w2, b2)


GATHER_PAGES_PER_STEP = 8
CMP_PITCH = 40


def _gather_kernel(pt_ref, *refs, pitch):
    o_ref = refs[-1]
    page = refs[0].shape[1]
    for k, r in enumerate(refs[:-1]):
        if pitch == CMP_BLOCK:
            o_ref[0, k * page:(k + 1) * page, :] = r[0]
        else:
            for blk in range(page // CMP_BLOCK):
                base = (k * (page // CMP_BLOCK) + blk) * pitch
                o_ref[0, base:base + CMP_BLOCK, :] = r[0, blk * CMP_BLOCK:(blk + 1) * CMP_BLOCK, :]
                o_ref[0, base + CMP_BLOCK:base + pitch, :] = jnp.zeros((pitch - CMP_BLOCK, o_ref.shape[2]),
                                                                        o_ref.dtype)


def gather_pages(cache, page_table, pitch=CMP_BLOCK):
    n_pool, page, c = cache.shape
    db, n_pages = page_table.shape
    pps = GATHER_PAGES_PER_STEP
    assert n_pages % pps == 0 and page % CMP_BLOCK == 0
    out_page = page // CMP_BLOCK * pitch

    def page_spec(k):
        return pl.BlockSpec((1, page, c), lambda b, j, pt: (pt[b, j * pps + k], 0, 0))

    return pl.pallas_call(
        functools.partial(_gather_kernel, pitch=pitch),
        grid_spec=pltpu.PrefetchScalarGridSpec(
            num_scalar_prefetch=1,
            grid=(db, n_pages // pps),
            in_specs=[page_spec(k) for k in range(pps)],
            out_specs=pl.BlockSpec((1, pps * out_page, c), lambda b, j, pt: (b, j, 0)),
        ),
        out_shape=jax.ShapeDtypeStruct((db, n_pages * out_page, c), cache.dtype),
        compiler_params=_cparams("parallel", "parallel"),
        name="gather_pages",
    )(page_table, *([cache] * pps))


def _rel_bucket(dist):
    n = jnp.maximum(dist, 0)
    exact = N_BUCKETS // 2
    nf = jnp.maximum(n, 1).astype(F32)
    large = exact + (jnp.log(nf / exact) / math.log(MAX_DISTANCE / exact) * (N_BUCKETS - exact)).astype(jnp.int32)
    large = jnp.minimum(large, N_BUCKETS - 1)
    return jnp.where(n < exact, n, large)


class _BiasTable:
    def __init__(self, table, max_dist, pad):
        tab = table.astype(F32)

        def lookup(dist):
            bucket = _rel_bucket(dist)
            out = jnp.zeros((tab.shape[1], dist.shape[0]), F32)
            for k in range(N_BUCKETS):
                out = jnp.where(bucket[None, :] == k, tab[k][:, None], out)
            return out

        self.pad = pad
        self.n = max_dist
        self.ext = lookup(jnp.arange(-pad, max_dist))
        self.rev = lookup(max_dist - 1 - jnp.arange(pad + max_dist))

    def at(self, d):
        return self.ext[:, self.pad + d]

    def rising(self, d0, length):
        return lax.slice_in_dim(self.ext, self.pad + d0, self.pad + d0 + length, axis=1)

    def falling(self, d0, length, step=1):
        start = self.n - 1 - d0
        return lax.slice(self.rev, (0, start), (self.rev.shape[0], start + step * (length - 1) + 1), (1, step))

    def frames(self, d0, count, hop, length):
        reps = length // hop
        seq = self.rising(d0, hop * (count + reps - 1)).reshape(-1, count + reps - 1, hop)
        return jnp.concatenate([seq[:, a:a + count] for a in range(reps)], axis=-1)

    def toeplitz(self, d0, count, hop, size):
        span = 2 * size - 1
        base = self.rising(d0 - (size - 1), hop * (count + 1)).reshape(-1, count + 1, hop)
        w = jnp.concatenate([base[:, :count], base[:, 1:, :span - hop]], axis=-1)
        hank = jnp.tile(w, (1, 1, size + 1))[:, :, :size * (span + 1)]
        hank = hank.reshape(w.shape[0], count, size, span + 1)[..., :size]
        return hank[:, :, ::-1, :]


def _softmax_step(s, vt, m, l, acc):
    m_new = jnp.maximum(m, jnp.max(s, axis=0, keepdims=True))
    alpha = jnp.exp(m - m_new)
    p = jnp.exp(s - m_new)
    l = alpha * l + jnp.sum(p, axis=0, keepdims=True)
    acc = alpha * acc + _dot(vt, p.astype(BF16))
    return m_new, l, acc


def _attn_kernel(bfar_ref, q_ref, gt_ref, kaug_ref, vst_ref, kw_ref, vwt_ref, ck_ref, cvt_ref,
                 bc_ref, bn_ref, o_ref, imp_ref, sc_ref, s_ref, *, nb, nc):
    g = pl.program_id(0)
    qt = pl.program_id(2)
    qs = qt * QT
    gq = GROUP * QT

    lane_cat = lambda parts: jnp.concatenate(parts, axis=1)
    q_t = q_ref[0].astype(F32).T.astype(BF16)
    q_heads = [q_t[h * HEAD_DIM:(h + 1) * HEAD_DIM] for h in range(GROUP)]
    q_plain = lane_cat(q_heads)

    col_i = lax.broadcasted_iota(jnp.int32, (1, gq), 1) % QT

    s_c = _dot(ck_ref[0, 0], q_plain) + lane_cat([bc_ref[h, 0] for h in range(GROUP)])
    n_idx = lax.broadcasted_iota(jnp.int32, (nc, gq), 0)
    valid_c = (n_idx * CMP_BLOCK + CMP_BLOCK - 1) <= (qs + col_i)
    s_c = jnp.where(valid_c, s_c, M_INIT)
    m_c = jnp.max(s_c, axis=0, keepdims=True)
    p_c = jnp.where(valid_c, jnp.exp(s_c - m_c), 0.0)
    l_c = jnp.sum(p_c, axis=0, keepdims=True)
    p_c = p_c * jnp.where(l_c > 0.0, 1.0 / l_c, 0.0)
    o_c = _dot(cvt_ref[0, 0], p_c.astype(BF16))

    imp = p_c[:, 0:QT]
    for h in range(1, GROUP):
        imp = imp + p_c[:, h * QT:(h + 1) * QT]
    imp_ref[...] = imp
    ratio = SEL_BLOCK // CMP_BLOCK
    imp2 = imp_ref[pl.ds(0, nb, stride=ratio), :]
    for r in range(1, ratio):
        imp2 = imp2 + imp_ref[pl.ds(r, nb, stride=ratio), :]
    j_idx = lax.broadcasted_iota(jnp.int32, (nb, QT), 0)
    i_idx = lax.broadcasted_iota(jnp.int32, (nb, QT), 1)
    jq = (qs + i_idx) // SEL_BLOCK
    forced = (j_idx == 0) | (j_idx == jq) | (j_idx == jq - 1)
    score = jnp.where(j_idx <= jq, jnp.where(forced, FORCE, imp2), -1.0)
    sc_ref[...] = score

    sub8 = lax.broadcasted_iota(jnp.int32, (8, QT), 0)
    groups = [score[8 * r:8 * r + 8] for r in range(nb // 8)]
    counts = [jnp.zeros((8, QT), F32) for _ in groups]
    for c in range(nb):
        row = sc_ref[c:c + 1, :]
        for r, grp_score in enumerate(groups):
            if r > c // 8:
                beats = row >= grp_score
            elif r < c // 8:
                beats = row > grp_score
            else:
                tie = jnp.where(row == grp_score, 1.0, 0.0) * jnp.where(sub8 > c % 8, 1.0, 0.0)
                counts[r] = counts[r] + tie
                beats = row > grp_score
            counts[r] = counts[r] + jnp.where(beats, 1.0, 0.0)
    cnt = jnp.concatenate(counts, axis=0)
    selected = (cnt < float(min(N_SEL, nb))) & (score >= 0.0)
    mask_feat = jnp.where(selected, 0.0, MASK_NEG).astype(BF16)
    q_aug = lane_cat([jnp.concatenate([qh, mask_feat], axis=0) for qh in q_heads])

    far_row = lane_cat([jnp.full((1, QT), bfar_ref[g * GROUP + h], F32) for h in range(GROUP)])

    def near_bias(kt0, n_sub):
        parts = []
        for sub in range(n_sub):
            dm = jnp.clip(qt - (kt0 + sub), 0, NEAR_SPAN - 1)
            parts.append(lane_cat([bn_ref[h, dm] for h in range(GROUP)]))
        return jnp.concatenate(parts, axis=0)

    init = (jnp.full((1, gq), M_INIT, F32), jnp.zeros((1, gq), F32), jnp.zeros((HEAD_DIM, gq), F32))
    sub_per_chunk = KEY_STEP // QT
    row_k = lax.broadcasted_iota(jnp.int32, (KEY_STEP, gq), 0)

    last_chunk = qt // sub_per_chunk

    def raw_scores(c):
        ks = pl.multiple_of(jnp.minimum(c, last_chunk) * KEY_STEP, KEY_STEP)
        return _dot(kaug_ref[0, 0, pl.ds(ks, KEY_STEP), :], q_aug)

    def pipelined(finish):
        def body(c, carry):
            nxt = raw_scores(c + 1)
            cur = s_ref[c % 2]
            s_ref[(c + 1) % 2] = nxt
            ks = pl.multiple_of(c * KEY_STEP, KEY_STEP)
            return _softmax_step(finish(c, cur, ks), vst_ref[0, 0, :, pl.ds(ks, KEY_STEP)], *carry)
        return body

    def far_finish(c, s, ks):
        return s + far_row

    def near_finish(c, s, ks):
        return s + near_bias(c * sub_per_chunk, sub_per_chunk)

    s_ref[0] = raw_scores(0)
    n_far = jnp.maximum(qt - (NEAR_TILES - 1), 0) // sub_per_chunk
    carry = lax.fori_loop(0, n_far, pipelined(far_finish), init)
    carry = lax.fori_loop(n_far, last_chunk, pipelined(near_finish), carry)
    ks = pl.multiple_of(last_chunk * KEY_STEP, KEY_STEP)
    s = near_finish(last_chunk, s_ref[last_chunk % 2], ks)
    s = jnp.where(ks + row_k <= qs + col_i, s, MASK_NEG)
    m_s, l_s, acc_s = _softmax_step(s, vst_ref[0, 0, :, pl.ds(ks, KEY_STEP)], *carry)
    o_s = acc_s / l_s

    wt0 = jnp.maximum(qt - WINDOW // QT, 0)
    ws = pl.multiple_of(wt0 * QT, QT)
    wk = WINDOW + QT
    s_w = _dot(kw_ref[0, 0, pl.ds(ws, wk), :], q_plain) + near_bias(wt0, wk // QT)
    dist = (qs + col_i) - (ws + lax.broadcasted_iota(jnp.int32, (wk, gq), 0))
    s_w = jnp.where((dist >= 0) & (dist < WINDOW), s_w, MASK_NEG)
    p_w = jnp.exp(s_w - jnp.max(s_w, axis=0, keepdims=True))
    l_w = jnp.sum(p_w, axis=0, keepdims=True)
    o_w = _dot(vwt_ref[0, 0, :, pl.ds(ws, wk)], p_w.astype(BF16)) / l_w

    gate = lambda br: lane_cat([gt_ref[0, 0, h * 3 + br:h * 3 + br + 1, :] for h in range(GROUP)])
    o_t = gate(0) * o_c + gate(1) * o_s + gate(2) * o_w
    o_rows = jnp.concatenate([o_t[:, h * QT:(h + 1) * QT] for h in range(GROUP)], axis=0)
    o_ref[0] = o_rows.T.astype(BF16)


def prompt_attention(q, gates_t, kaug, vst, kwin, vwt, ck, cvt, bias_cmp, bias_near, bias_far):
    bsz, t, dq = q.shape
    nb = t // SEL_BLOCK
    nc = t // CMP_BLOCK
    nq = t // QT
    gd = GROUP * HEAD_DIM
    per_bg = lambda a: pl.BlockSpec((1, 1) + a.shape[2:], lambda g, b, i, *_: (b, g, 0, 0))
    grid_spec = pltpu.PrefetchScalarGridSpec(
        num_scalar_prefetch=1,
        grid=(KV_HEADS, bsz, nq),
        in_specs=[pl.BlockSpec((1, QT, gd), lambda g, b, i, *_: (b, i, g)),
                  pl.BlockSpec((1, 1, 3 * GROUP, QT), lambda g, b, i, *_: (b, g, 0, i)),
                  per_bg(kaug), per_bg(vst), per_bg(kwin), per_bg(vwt), per_bg(ck), per_bg(cvt),
                  pl.BlockSpec((GROUP, 1, nc, QT), lambda g, b, i, *_: (g, i, 0, 0)),
                  pl.BlockSpec((GROUP, NEAR_SPAN, QT, QT), lambda g, b, i, *_: (g, 0, 0, 0))],
        out_specs=pl.BlockSpec((1, QT, gd), lambda g, b, i, *_: (b, i, g)),
        scratch_shapes=[pltpu.VMEM((nc, QT), F32), pltpu.VMEM((nb, QT), F32),
                        pltpu.VMEM((2, KEY_STEP, GROUP * QT), F32)],
    )
    return pl.pallas_call(
        functools.partial(_attn_kernel, nb=nb, nc=nc),
        grid_spec=grid_spec,
        out_shape=jax.ShapeDtypeStruct((bsz, t, dq), BF16),
        compiler_params=_cparams("parallel", "parallel", "parallel"),
        name="prompt_attention",
    )(bias_far, q, gates_t, kaug, vst, kwin, vwt, ck, cvt, bias_cmp, bias_near)


KEY_CHUNK = 1024
NEW_PAD = 128


def _diag_heads(full, row_g):
    out = None
    for g in range(KV_HEADS):
        part = jnp.where(row_g == g, full[:, g * HEAD_DIM:(g + 1) * HEAD_DIM], 0.0)
        out = part if out is None else out + part
    return out


def _sample_attn_kernel(q_ref, gates_ref, ksel_ref, nsel_ref, kwin_ref, nwin_ref, ck_ref, cv_ref,
                        bsel_ref, bwin_ref, bcmp_ref, e_ref, o_ref, s_scr, *, past, ts):
    kvd = KV_HEADS * HEAD_DIM
    rows = GROUP * KV_HEADS * ts
    nc = ck_ref.shape[1]
    q = q_ref[0]
    r_idx = lax.broadcasted_iota(jnp.int32, (rows, 1), 0)
    row_q = r_idx % ts
    row_g = (r_idx // ts) % KV_HEADS

    s_c = _dot_nt(q, ck_ref[0]) + bcmp_ref[...]
    p_c = jnp.exp(s_c - jnp.max(s_c, axis=1, keepdims=True))
    p_c = p_c / jnp.sum(p_c, axis=1, keepdims=True)
    o_c = _diag_heads(_dot(p_c.astype(BF16), cv_ref[0]), row_g)

    per = KV_HEADS * ts
    imp = p_c[0:per]
    for h in range(1, GROUP):
        imp = imp + p_c[h * per:(h + 1) * per]
    ratio = SEL_BLOCK // CMP_BLOCK
    imp2 = imp
    for r in range(1, ratio):
        imp2 = imp2 + pltpu.roll(imp, nc - r, axis=1)
    lane = lax.broadcasted_iota(jnp.int32, (per, nc), 1)
    blk = lane // ratio
    jq = past // SEL_BLOCK
    forced = (blk == 0) | (blk == jq - 1)
    score = jnp.where(lane % ratio == 0, jnp.where(forced, FORCE, imp2), -2.0)
    cnt = jnp.where(score < FORCE, 1.0, 0.0)
    for c in range(nc // ratio):
        col = score[:, c * ratio:c * ratio + 1]
        ge = jnp.where(col >= score, 1.0, 0.0)
        gt = jnp.where(col > score, 1.0, 0.0)
        cnt = cnt + jnp.where(lane > c * ratio, ge, gt)
    selected = (cnt < float(min(N_SEL, jq + 1))) & (score >= 0.0)
    mask_feat = jnp.where(selected, 0.0, MASK_NEG).astype(BF16)
    mask_feat = jnp.concatenate([mask_feat] * GROUP, axis=0)

    for c in range(past // KEY_CHUNK):
        sl = slice(c * KEY_CHUNK, (c + 1) * KEY_CHUNK)
        k = ksel_ref[0, sl, 0:kvd].astype(BF16)
        s_scr[:, sl] = _dot_nt(q, k) + bsel_ref[:, sl] + _dot(mask_feat, e_ref[:, sl])
    new_j = lax.broadcasted_iota(jnp.int32, (rows, NEW_PAD), 1)
    new_ok = new_j <= row_q
    s_new = _dot_nt(q, nsel_ref[0, :, 0:kvd].astype(BF16)) + bsel_ref[:, past:past + NEW_PAD]
    s_scr[:, past:past + NEW_PAD] = jnp.where(new_ok, s_new, MASK_NEG)
    s_all = s_scr[...]
    p_s = jnp.exp(s_all - jnp.max(s_all, axis=1, keepdims=True))
    l_s = jnp.sum(p_s, axis=1, keepdims=True)
    s_scr[...] = p_s
    acc = _dot(s_scr[:, past:past + NEW_PAD].astype(BF16), nsel_ref[0, :, kvd:2 * kvd].astype(BF16))
    for c in range(past // KEY_CHUNK):
        sl = slice(c * KEY_CHUNK, (c + 1) * KEY_CHUNK)
        acc = acc + _dot(s_scr[:, sl].astype(BF16), ksel_ref[0, sl, kvd:2 * kvd].astype(BF16))
    o_s = _diag_heads(acc, row_g) / l_s

    wlen = kwin_ref.shape[1]
    win_i = lax.broadcasted_iota(jnp.int32, (rows, wlen), 1)
    s_w1 = _dot_nt(q, kwin_ref[0, :, 0:kvd].astype(BF16)) + bwin_ref[:, 0:wlen]
    s_w1 = jnp.where(win_i + (WINDOW - wlen) > row_q, s_w1, MASK_NEG)
    s_w2 = _dot_nt(q, nwin_ref[0, :, 0:kvd].astype(BF16)) + bwin_ref[:, wlen:wlen + NEW_PAD]
    s_w2 = jnp.where(new_ok, s_w2, MASK_NEG)
    m_w = jnp.maximum(jnp.max(s_w1, axis=1, keepdims=True), jnp.max(s_w2, axis=1, keepdims=True))
    p_w1 = jnp.exp(s_w1 - m_w)
    p_w2 = jnp.exp(s_w2 - m_w)
    l_w = jnp.sum(p_w1, axis=1, keepdims=True) + jnp.sum(p_w2, axis=1, keepdims=True)
    acc_w = (_dot(p_w1.astype(BF16), kwin_ref[0, :, kvd:2 * kvd].astype(BF16))
             + _dot(p_w2.astype(BF16), nwin_ref[0, :, kvd:2 * kvd].astype(BF16)))
    o_w = _diag_heads(acc_w, row_g) / l_w

    gts = gates_ref[0]
    o_ref[0] = gts[:, 0:1] * o_c + gts[:, 1:2] * o_s + gts[:, 2:3] * o_w


def sample_attention(q_bd, gates, ksel, nsel, kwin, nwin, ck, cv, bsel, bwin, bcmp, e_mat, past, ts):
    db, rows, kvd = q_bd.shape
    seq = lambda a: pl.BlockSpec((1,) + a.shape[1:], lambda b: (b,) + (0,) * (a.ndim - 1))
    return pl.pallas_call(
        functools.partial(_sample_attn_kernel, past=past, ts=ts),
        grid=(db,),
        in_specs=[seq(q_bd), seq(gates), seq(ksel), seq(nsel), seq(kwin), seq(nwin), seq(ck), seq(cv),
                  _full_spec(bsel), _full_spec(bwin), _full_spec(bcmp), _full_spec(e_mat)],
        out_specs=pl.BlockSpec((1, rows, HEAD_DIM), lambda b: (b, 0, 0)),
        out_shape=jax.ShapeDtypeStruct((db, rows, HEAD_DIM), F32),
        scratch_shapes=[pltpu.VMEM((rows, past + NEW_PAD), F32)],
        compiler_params=_cparams("parallel"),
        name="sample_attention",
    )(q_bd, gates, ksel, nsel, kwin, nwin, ck, cv, bsel, bwin, bcmp, e_mat)


def _prep_weights(P):
    W = {}
    cast = lambda a: a.astype(BF16)
    for k in ("w_pw1", "w_pw2", "w_kv", "w_o", "w_gate", "w_up", "w_down", "w_phi1", "w_phi2"):
        W[k] = cast(P[k])
    nq = N_HEADS * HEAD_DIM
    wqg = P["w_qg"]
    pad = 128 - (wqg.shape[2] - nq)
    W["w_qg"] = cast(jnp.pad(wqg, ((0, 0), (0, 0), (0, pad))))
    return W


def _trunk(x, mods, is_prompt, caches, P, W, bias_dist):
    bsz, t, d = x.shape
    m = bsz * t
    depth = P["w_gate"].shape[0]
    n_a = P["w_pw1"].shape[0]
    if is_prompt:
        tm, tpg = 256, t // 256
    else:
        tm, tpg = m, 1
    x2 = x.reshape(m, d)
    row2 = lambda a: a.reshape(1, -1)
    conv_states = []
    y = None
    for l in range(depth):
        shift, scale, gate = mods[(l, 0)]
        g0 = row2(P["norm_g"][l, 0])
        if l < n_a:
            u = pw1_glu(x2, g0, shift, scale, W["w_pw1"][l], row2(P["b_pw1"][l]), tm, tpg)
            u3 = u.reshape(bsz, t, d)
            if is_prompt:
                hist = jnp.zeros((bsz, CONV_PAD, d), F32)
            else:
                hist = jnp.pad(caches["state_conv"][l], ((0, 0), (CONV_PAD - CONV_W + 1, 0), (0, 0)))
            full = jnp.concatenate([hist, u3], axis=1)
            conv_states.append(full[:, -(CONV_W - 1):])
            wdw = jnp.pad(P["w_dw"][l], ((0, CONV_PAD - CONV_W), (0, 0)))
            args = (wdw, row2(P["b_dw"][l]), row2(P["ln_g"][l]), row2(P["ln_b"][l]), W["w_pw2"][l],
                    row2(P["b_pw2"][l]))
            if is_prompt:
                x2 = conv_pw2_residual(full, *args, x2, gate, tm, tpg)
            else:
                x2 = conv_pw2_residual_small(full, *args, x2.reshape(bsz, t, d),
                                             gate.reshape(bsz, t, d)[:, 0:1]).reshape(m, d)
        else:
            if l == n_a:
                rows = kv_proj(x2, row2(P["g_kv"]), W["w_kv"], tm)
                rows6 = rows.reshape(bsz, t, 3, 2, KV_HEADS, HEAD_DIM)
                if is_prompt:
                    att = _prompt_attention_setup(rows6, P, W, bias_dist)
                    win_state = rows6[:, -min(WINDOW, t):, 2]
                else:
                    att = _sample_attention_setup(rows6, caches, P, W, bias_dist)
                    wl = caches["cache_kv_win"].shape[1]
                    win_state = jnp.concatenate([caches["cache_kv_win"], rows6[:, :, 2]], axis=1)[:, -wl:]
            lb = l - n_a
            q, gates = qg_proj(x2, g0, shift, scale, W["w_qg"][lb], tm, tpg)
            if is_prompt:
                o = _prompt_attention_layer(q.reshape(bsz, t, -1), gates.reshape(bsz, t, -1), att)
            else:
                o = _sample_attention_layer(q.reshape(bsz, t, -1), gates.reshape(bsz, t, -1), att)
            x2 = o_proj_residual(o.reshape(m, -1), W["w_o"][lb], x2, gate, tm, tpg)
        shift, scale, gate = mods[(l, 1)]
        fg = row2(P["final_g"]) if l == depth - 1 else None
        out = ffn(x2, row2(P["norm_g"][l, 1]), shift, scale, gate, W["w_gate"][l], W["w_up"][l],
                  W["w_down"][l], tm, tpg, final_g=fg)
        if l == depth - 1:
            y = out
        else:
            x2 = out
    return (y.reshape(bsz, t, d), rows6[:, :, 0], rows6[:, :, 1], win_state, jnp.stack(conv_states))


def _compress_weights(P, W):
    pe = P["pe_cmp"]
    pe_slab = jnp.repeat(jnp.tile(pe, (1, 1, HEADS_PER_SLAB)), KV_HEADS // HEADS_PER_SLAB, axis=0)
    w1 = W["w_phi1"].reshape(2, CMP_BLOCK, HEAD_DIM, -1)
    zero = jnp.zeros_like(w1)
    w1_bd = jnp.concatenate([jnp.concatenate([w1, zero], axis=-1), jnp.concatenate([zero, w1], axis=-1)],
                            axis=2)
    b1_slab = jnp.tile(P["b_phi1"], (1, HEADS_PER_SLAB))[:, None, :]
    return pe_slab, w1_bd, b1_slab, W["w_phi2"], P["b_phi2"][:, None, :]


def _prompt_attention_setup(rows6, P, W, bias_dist):
    bsz, t = rows6.shape[:2]
    nb = t // SEL_BLOCK
    nc = t // CMP_BLOCK
    nq = t // QT
    cmp = compress(rows6.reshape(bsz, t, -1), nc, CMP_BLOCK, *_compress_weights(P, W))
    ck = cmp[:, 0].astype(BF16)
    cvt = cmp[:, 1].transpose(0, 1, 3, 2).astype(BF16)
    onehot = (jnp.arange(t)[:, None] // SEL_BLOCK == jnp.arange(nb)[None, :]).astype(BF16)
    ksel = rows6[:, :, 1, 0].transpose(0, 2, 1, 3).astype(BF16)
    kaug = jnp.concatenate([ksel, jnp.broadcast_to(onehot, (bsz, KV_HEADS, t, nb))], axis=-1)
    vst = rows6[:, :, 1, 1].transpose(0, 2, 3, 1).astype(BF16)
    kwin = rows6[:, :, 2, 0].transpose(0, 2, 1, 3).astype(BF16)
    vwt = rows6[:, :, 2, 1].transpose(0, 2, 3, 1).astype(BF16)
    per_qt = QT // CMP_BLOCK
    n_u = nc + per_qt * (nq - 1)
    frames = bias_dist.frames(-(CMP_BLOCK - 1) - CMP_BLOCK * (nc - 1), n_u, CMP_BLOCK, QT)[:, ::-1]
    bias_cmp = jnp.stack([frames[:, per_qt * (nq - 1 - i):per_qt * (nq - 1 - i) + nc] for i in range(nq)],
                         axis=1)
    bias_near = bias_dist.toeplitz(0, NEAR_SPAN, QT, QT)
    bias_far = bias_dist.at(NEAR_TILES * QT)
    return dict(kaug=kaug, vst=vst, kwin=kwin, vwt=vwt, ck=ck, cvt=cvt, bias_cmp=bias_cmp,
                bias_near=bias_near, bias_far=bias_far)


def _prompt_attention_layer(q, gates, att):
    bsz, t, _ = q.shape
    gates_t = gates[:, :, :3 * N_HEADS].reshape(bsz, t, KV_HEADS, 3 * GROUP).transpose(0, 2, 3, 1)
    return prompt_attention(q, gates_t, att["kaug"], att["vst"], att["kwin"], att["vwt"], att["ck"],
                            att["cvt"], att["bias_cmp"], att["bias_near"], att["bias_far"])


def _sample_attention_setup(rows6, caches, P, W, bias_dist):
    db, ts = rows6.shape[:2]
    pt = caches["page_table"]
    n_pages = pt.shape[1]
    page = caches["cache_kv_cmp"].shape[1]
    past = n_pages * page
    c = 2 * KV_HEADS * HEAD_DIM
    cmp_g = gather_pages(caches["cache_kv_cmp"].reshape(-1, page, c), pt, pitch=CMP_PITCH)
    sel_g = gather_pages(caches["cache_kv_sel"].reshape(-1, page, c), pt)
    nc = past // CMP_BLOCK
    assert (past + ts) // CMP_BLOCK == nc and past % KEY_CHUNK == 0
    cmp = compress(cmp_g, nc, CMP_PITCH, *_compress_weights(P, W))
    ck = cmp[:, 0].transpose(0, 2, 1, 3).reshape(db, nc, KV_HEADS * HEAD_DIM).astype(BF16)
    cv = cmp[:, 1].transpose(0, 2, 1, 3).reshape(db, nc, KV_HEADS * HEAD_DIM).astype(BF16)
    pad_new = lambda a: jnp.pad(a.reshape(db, ts, c), ((0, 0), (0, NEW_PAD - ts), (0, 0)))
    nsel = pad_new(rows6[:, :, 1])
    nwin = pad_new(rows6[:, :, 2])
    kwin = caches["cache_kv_win"].reshape(db, -1, c)
    wlen = kwin.shape[1]
    rows = GROUP * KV_HEADS * ts
    def by_row(f):
        per_q = jnp.stack([f(q) for q in range(ts)], axis=0)
        per_q = per_q.reshape(ts, KV_HEADS, GROUP, -1).transpose(2, 1, 0, 3)
        return per_q.reshape(rows, -1)

    bsel = by_row(lambda q: bias_dist.falling(past + q, past + NEW_PAD))
    bwin = by_row(lambda q: jnp.concatenate([bias_dist.falling(wlen + q, wlen),
                                             bias_dist.falling(q, NEW_PAD)], axis=1))
    bcmp = by_row(lambda q: bias_dist.falling(past + q - (CMP_BLOCK - 1), nc, step=CMP_BLOCK))
    ratio = SEL_BLOCK // CMP_BLOCK
    lane_blk = np.where(np.arange(nc) % ratio == 0, np.arange(nc) // ratio, -1)
    e_mat = jnp.asarray(lane_blk[:, None] == (np.arange(past) // SEL_BLOCK)[None, :], BF16)
    return dict(ksel=sel_g, nsel=nsel, kwin=kwin, nwin=nwin, ck=ck, cv=cv, bsel=bsel, bwin=bwin,
                bcmp=bcmp, e_mat=e_mat, past=past)


def _sample_attention_layer(q, gates, att):
    db, ts, _ = q.shape
    rows = GROUP * KV_HEADS * ts
    q5 = q.reshape(db, ts, KV_HEADS, GROUP, HEAD_DIM).transpose(0, 3, 2, 1, 4)
    eye = jnp.eye(KV_HEADS, dtype=q.dtype)
    q_bd = (q5[:, :, :, :, None, :] * eye[None, None, :, None, :, None]).reshape(db, rows, KV_HEADS * HEAD_DIM)
    g5 = gates[:, :, :3 * N_HEADS].reshape(db, ts, KV_HEADS, GROUP, 3).transpose(0, 3, 2, 1, 4)
    g_rows = jnp.pad(g5.reshape(db, rows, 3), ((0, 0), (0, 0), (0, 125)))
    o = sample_attention(q_bd, g_rows, att["ksel"], att["nsel"], att["kwin"], att["nwin"], att["ck"],
                         att["cv"], att["bsel"], att["bwin"], att["bcmp"], att["e_mat"], att["past"], ts)
    o = o.reshape(db, GROUP, KV_HEADS, ts, HEAD_DIM).transpose(0, 3, 2, 1, 4)
    return o.reshape(db, ts, N_HEADS * HEAD_DIM).astype(BF16)


def kernel(x_prompt, x_sample, c_prompt, c_sample, cache_kv_cmp, cache_kv_sel, cache_kv_win, state_conv, page_table, w_ada, b_ada, norm_g, w_pw1, b_pw1, w_dw, b_dw, ln_g, ln_b, w_pw2, b_pw2, g_kv, w_kv, w_phi1, b_phi1, w_phi2, b_phi2, pe_cmp, w_qg, w_o, rel_table, w_gate, w_up, w_down, final_g):
    P = dict(w_ada=w_ada, b_ada=b_ada, norm_g=norm_g, w_pw1=w_pw1, b_pw1=b_pw1, w_dw=w_dw, b_dw=b_dw,
             ln_g=ln_g, ln_b=ln_b, w_pw2=w_pw2, b_pw2=b_pw2, g_kv=g_kv, w_kv=w_kv, w_phi1=w_phi1,
             b_phi1=b_phi1, w_phi2=w_phi2, b_phi2=b_phi2, pe_cmp=pe_cmp, w_qg=w_qg, w_o=w_o,
             rel_table=rel_table, w_gate=w_gate, w_up=w_up, w_down=w_down, final_g=final_g)
    W = _prep_weights(P)
    bp, tp, d = x_prompt.shape
    db, ts, _ = x_sample.shape
    depth = w_ada.shape[0]

    n_c = bp + db
    r_pad = -(-n_c // 8) * 8
    c_all = jnp.pad(jnp.concatenate([c_prompt, c_sample], axis=0), ((0, r_pad - n_c), (0, 0)))
    mod = ada_modulation(c_all, w_ada.reshape(depth * 2, d, 3 * d), b_ada.reshape(depth * 2, 1, 3 * d))
    mods_p, mods_s = {}, {}
    for l in range(depth):
        for j in range(2):
            parts = [mod[l * 2 + j, :, k * d:(k + 1) * d] for k in range(3)]
            mods_p[(l, j)] = tuple(p[:bp, None, :] for p in parts)
            mods_s[(l, j)] = tuple(jnp.repeat(p[bp:n_c], ts, axis=0)[None] for p in parts)

    past = page_table.shape[1] * cache_kv_cmp.shape[1]
    bias_dist = _BiasTable(rel_table, max(tp, past + 2 * NEW_PAD), pad=max(tp, NEW_PAD))
    caches = dict(cache_kv_cmp=cache_kv_cmp, cache_kv_sel=cache_kv_sel, cache_kv_win=cache_kv_win,
                  state_conv=state_conv, page_table=page_table)
    y_p, cmp_p, sel_p, win_p, conv_p = _trunk(x_prompt, mods_p, True, None, P, W, bias_dist)
    y_s, cmp_s, sel_s, win_s, conv_s = _trunk(x_sample, mods_s, False, caches, P, W, bias_dist)
    return (y_p, y_s, cmp_p, cmp_s, sel_p, sel_s, win_p, win_s, conv_p, conv_s)
```

```python
import functools
import math

import numpy as np
import jax
import jax.numpy as jnp
from jax import lax
from jax.experimental import pallas as pl
from jax.experimental.pallas import tpu as pltpu

F32 = jnp.float32
BF16 = jnp.bfloat16

N_HEADS = 16
KV_HEADS = 4
GROUP = N_HEADS // KV_HEADS
HEAD_DIM = 64
CONV_W = 31
CONV_PAD = 32
SUBLANES = 8
CMP_BLOCK = 32
SEL_BLOCK = 64
N_SEL = 16
WINDOW = 512
N_BUCKETS = 32
MAX_DISTANCE = 1024
EPS = 1e-6
FORCE = 1e4
MASK_NEG = -1e9
M_INIT = -1e30
QT = 128
NEAR_TILES = 8
KEY_STEP = 512
NEAR_SPAN = NEAR_TILES + KEY_STEP // QT - 1
V7X_VMEM_LIMIT_BYTES = 56 * 1024 * 1024


def _cparams(*sem):
    return pltpu.CompilerParams(dimension_semantics=sem, vmem_limit_bytes=V7X_VMEM_LIMIT_BYTES)


def _full_spec(a):
    nd = a.ndim
    return pl.BlockSpec(a.shape, lambda *_: (0,) * nd)


def _silu(x):
    return x * jax.nn.sigmoid(x)


def _norm_mod(x, g, shift, scale):
    ms = jnp.mean(x * x, axis=-1, keepdims=True)
    return x * lax.rsqrt(ms + EPS) * g * (1.0 + scale) + shift


def _dot(a, b):
    return jnp.dot(a, b, preferred_element_type=F32)


def _dot_nt(a, b):
    return lax.dot_general(a, b, (((1,), (1,)), ((), ())), preferred_element_type=F32)


def _ada_kernel(c_ref, w_ref, b_ref, o_ref):
    c = c_ref[...]
    o_ref[0] = _dot(_silu(c), w_ref[0]) + b_ref[0]


def ada_modulation(c_all, w_ada, b_ada):
    nl, d, n3 = w_ada.shape
    r = c_all.shape[0]
    tn = n3 // 2
    return pl.pallas_call(
        _ada_kernel,
        grid=(nl, n3 // tn),
        in_specs=[pl.BlockSpec((r, d), lambda l, j: (0, 0)),
                  pl.BlockSpec((1, d, tn), lambda l, j: (l, 0, j)),
                  pl.BlockSpec((1, 1, tn), lambda l, j: (l, 0, j))],
        out_specs=pl.BlockSpec((1, r, tn), lambda l, j: (l, 0, j)),
        out_shape=jax.ShapeDtypeStruct((nl, r, n3), F32),
        compiler_params=_cparams("parallel", "parallel"),
        name="ada_modulation",
    )(c_all, w_ada, b_ada)


def _row_spec(tm, width):
    return pl.BlockSpec((tm, width), lambda i: (i, 0))


def _mod_spec(mod, tiles_per_group):
    _, r, d = mod.shape
    return pl.BlockSpec((None, r, d), lambda i: (i // tiles_per_group, 0, 0))


def _pw1_kernel(x_ref, g_ref, sh_ref, sc_ref, w_ref, b_ref, u_ref):
    d = x_ref.shape[1]
    h = _norm_mod(x_ref[...], g_ref[...], sh_ref[...], sc_ref[...]).astype(BF16)
    z = _dot(h, w_ref[...]) + b_ref[...]
    u_ref[...] = z[:, :d] * jax.nn.sigmoid(z[:, d:])


def pw1_glu(x, g, shift, scale, w, b, tm, tpg):
    m, d = x.shape
    return pl.pallas_call(
        _pw1_kernel,
        grid=(m // tm,),
        in_specs=[_row_spec(tm, d), _full_spec(g), _mod_spec(shift, tpg), _mod_spec(scale, tpg),
                  _full_spec(w), _full_spec(b)],
        out_specs=_row_spec(tm, d),
        out_shape=jax.ShapeDtypeStruct((m, d), F32),
        compiler_params=_cparams("parallel"),
        name="pw1_glu",
    )(x, g, shift, scale, w, b)


def _conv_tail(rows_at, wdw_ref, bdw_ref, lg_ref, lb_ref, w2_ref, b2_ref):
    acc = None
    for k in range(CONV_W):
        term = rows_at(k + CONV_PAD - CONV_W + 1) * wdw_ref[k:k + 1, :]
        acc = term if acc is None else acc + term
    acc = acc + bdw_ref[...]
    mu = jnp.mean(acc, axis=-1, keepdims=True)
    cen = acc - mu
    var = jnp.mean(cen * cen, axis=-1, keepdims=True)
    yn = cen * lax.rsqrt(var + EPS) * lg_ref[...] + lb_ref[...]
    return _dot(_silu(yn).astype(BF16), w2_ref[...]) + b2_ref[...]


def _conv_kernel(halo_ref, u_ref, wdw_ref, bdw_ref, lg_ref, lb_ref, w2_ref, b2_ref, x_ref, gate_ref,
                 o_ref, buf, shifted, *, tiles_per_seq):
    tm = x_ref.shape[0]
    first = pl.program_id(0) % tiles_per_seq == 0
    buf[0:CONV_PAD, :] = jnp.where(first, 0.0, halo_ref[0])
    buf[CONV_PAD:CONV_PAD + tm, :] = u_ref[0]
    span = tm + CONV_PAD - SUBLANES
    for s in range(1, SUBLANES):
        shifted[s - 1] = buf[pl.ds(s, span), :]

    def rows_at(o):
        a, s = divmod(o, SUBLANES)
        if s == 0:
            return buf[pl.ds(o, tm), :]
        return shifted[s - 1, pl.ds(a * SUBLANES, tm), :]

    out = _conv_tail(rows_at, wdw_ref, bdw_ref, lg_ref, lb_ref, w2_ref, b2_ref)
    o_ref[...] = x_ref[...] + gate_ref[...] * out


def conv_pw2_residual(u, wdw, bdw, lg, lb, w2, b2, x, gate, tm, tpg):
    m, d = x.shape
    bsz, t, _ = u.shape
    assert t % tm == 0 and tm % CONV_PAD == 0 and m == bsz * t
    tps = t // tm
    return pl.pallas_call(
        functools.partial(_conv_kernel, tiles_per_seq=tps),
        grid=(m // tm,),
        in_specs=[pl.BlockSpec((1, CONV_PAD, d),
                               lambda i: (i // tps, jnp.maximum((i % tps) * (tm // CONV_PAD) - 1, 0), 0)),
                  pl.BlockSpec((1, tm, d), lambda i: (i // tps, i % tps, 0)),
                  _full_spec(wdw), _full_spec(bdw), _full_spec(lg), _full_spec(lb),
                  _full_spec(w2), _full_spec(b2), _row_spec(tm, d), _mod_spec(gate, tpg)],
        out_specs=_row_spec(tm, d),
        out_shape=jax.ShapeDtypeStruct((m, d), F32),
        scratch_shapes=[pltpu.VMEM((tm + CONV_PAD, d), F32),
                        pltpu.VMEM((SUBLANES - 1, tm + CONV_PAD - SUBLANES, d), F32)],
        compiler_params=_cparams("parallel"),
        name="conv_pw2_residual",
    )(u, u, wdw, bdw, lg, lb, w2, b2, x, gate)


def _conv_small_kernel(full_ref, wdw_ref, bdw_ref, lg_ref, lb_ref, w2_ref, b2_ref, x_ref, gate_ref,
                       o_ref):
    tm = x_ref.shape[1]
    rows_at = lambda o: full_ref[0, pl.ds(o, tm), :]
    out = _conv_tail(rows_at, wdw_ref, bdw_ref, lg_ref, lb_ref, w2_ref, b2_ref)
    o_ref[0] = x_ref[0] + gate_ref[0] * out


def conv_pw2_residual_small(full, wdw, bdw, lg, lb, w2, b2, x, gate):
    bsz, ts, d = x.shape
    seq = lambda a: pl.BlockSpec((1,) + a.shape[1:], lambda i: (i, 0, 0))
    return pl.pallas_call(
        _conv_small_kernel,
        grid=(bsz,),
        in_specs=[seq(full), _full_spec(wdw), _full_spec(bdw), _full_spec(lg), _full_spec(lb),
                  _full_spec(w2), _full_spec(b2), seq(x), seq(gate)],
        out_specs=seq(x),
        out_shape=jax.ShapeDtypeStruct(x.shape, F32),
        compiler_params=_cparams("parallel"),
        name="conv_pw2_residual_small",
    )(full, wdw, bdw, lg, lb, w2, b2, x, gate)


def _ffn_kernel(x_ref, g_ref, sh_ref, sc_ref, gate_ref, wg_ref, wu_ref, wd_ref, *rest, final):
    x = x_ref[...]
    h = _norm_mod(x, g_ref[...], sh_ref[...], sc_ref[...]).astype(BF16)
    a = _dot(h, wg_ref[...])
    b = _dot(h, wu_ref[...])
    act = (_silu(a) * b).astype(BF16)
    xn = x + gate_ref[...] * _dot(act, wd_ref[...])
    if final:
        fg_ref, o_ref = rest
        ms = jnp.mean(xn * xn, axis=-1, keepdims=True)
        o_ref[...] = xn * lax.rsqrt(ms + EPS) * fg_ref[...]
    else:
        (o_ref,) = rest
        o_ref[...] = xn


def ffn(x, g, shift, scale, gate, wg, wu, wd, tm, tpg, final_g=None):
    m, d = x.shape
    final = final_g is not None
    ins = [x, g, shift, scale, gate, wg, wu, wd] + ([final_g] if final else [])
    specs = [_row_spec(tm, d), _full_spec(g), _mod_spec(shift, tpg), _mod_spec(scale, tpg),
             _mod_spec(gate, tpg), _full_spec(wg), _full_spec(wu), _full_spec(wd)]
    if final:
        specs.append(_full_spec(final_g))
    return pl.pallas_call(
        functools.partial(_ffn_kernel, final=final),
        grid=(m // tm,),
        in_specs=specs,
        out_specs=_row_spec(tm, d),
        out_shape=jax.ShapeDtypeStruct((m, d), F32),
        compiler_params=_cparams("parallel"),
        name="ffn_final" if final else "ffn",
    )(*ins)


def _kvproj_kernel(x_ref, g_ref, w_ref, o_ref):
    x = x_ref[...]
    ms = jnp.mean(x * x, axis=-1, keepdims=True)
    h = (x * lax.rsqrt(ms + EPS) * g_ref[...]).astype(BF16)
    o_ref[...] = _dot(h, w_ref[...])


def kv_proj(x, g, w, tm):
    m, d = x.shape
    n = w.shape[1]
    return pl.pallas_call(
        _kvproj_kernel,
        grid=(m // tm,),
        in_specs=[_row_spec(tm, d), _full_spec(g), _full_spec(w)],
        out_specs=_row_spec(tm, n),
        out_shape=jax.ShapeDtypeStruct((m, n), F32),
        compiler_params=_cparams("parallel"),
        name="kv_proj",
    )(x, g, w)


def _qg_kernel(x_ref, g_ref, sh_ref, sc_ref, w_ref, q_ref, gates_ref):
    nq = q_ref.shape[1]
    h = _norm_mod(x_ref[...], g_ref[...], sh_ref[...], sc_ref[...]).astype(BF16)
    z = _dot(h, w_ref[...])
    q_ref[...] = (z[:, :nq] * (HEAD_DIM ** -0.5)).astype(BF16)
    gates_ref[...] = jax.nn.sigmoid(z[:, nq:])


def qg_proj(x, g, shift, scale, w, tm, tpg):
    m, d = x.shape
    nq = N_HEADS * HEAD_DIM
    ng = w.shape[1] - nq
    return pl.pallas_call(
        _qg_kernel,
        grid=(m // tm,),
        in_specs=[_row_spec(tm, d), _full_spec(g), _mod_spec(shift, tpg), _mod_spec(scale, tpg),
                  _full_spec(w)],
        out_specs=[_row_spec(tm, nq), _row_spec(tm, ng)],
        out_shape=[jax.ShapeDtypeStruct((m, nq), BF16), jax.ShapeDtypeStruct((m, ng), F32)],
        compiler_params=_cparams("parallel"),
        name="qg_proj",
    )(x, g, shift, scale, w)


def _oproj_kernel(o_ref, w_ref, x_ref, gate_ref, out_ref):
    out_ref[...] = x_ref[...] + gate_ref[...] * _dot(o_ref[...], w_ref[...])


def o_proj_residual(o, w, x, gate, tm, tpg):
    m, d = x.shape
    return pl.pallas_call(
        _oproj_kernel,
        grid=(m // tm,),
        in_specs=[_row_spec(tm, o.shape[1]), _full_spec(w), _row_spec(tm, d), _mod_spec(gate, tpg)],
        out_specs=_row_spec(tm, d),
        out_shape=jax.ShapeDtypeStruct((m, d), F32),
        compiler_params=_cparams("parallel"),
        name="o_proj_residual",
    )(o, w, x, gate)


SLAB = 128
HEADS_PER_SLAB = SLAB // HEAD_DIM


def _compress_kernel(x_ref, pe_ref, w1_ref, b1_ref, w2_ref, b2_ref, o_ref, *, nblk, pitch):
    hidden = w2_ref.shape[1]
    acc = None
    for r in range(CMP_BLOCK):
        xr = (x_ref[0, pl.ds(r, nblk, stride=pitch), :] + pe_ref[0, r:r + 1, :]).astype(BF16)
        term = _dot(xr, w1_ref[0, r])
        acc = term if acc is None else acc + term
    hid = _silu(acc + b1_ref[0])
    for hh in range(HEADS_PER_SLAB):
        o_ref[0, 0, hh] = _dot(hid[:, hh * hidden:(hh + 1) * hidden].astype(BF16), w2_ref[0]) + b2_ref[0]


def compress(x, nblk, pitch, pe_slab, w1_bd, b1_slab, w2, b2):
    bc, rows, _ = x.shape
    n_slab = 2 * KV_HEADS // HEADS_PER_SLAB
    slabs_per_e = n_slab // 2
    per_e = lambda a: pl.BlockSpec((1,) + a.shape[1:], lambda b, s: (s // slabs_per_e,) + (0,) * (a.ndim - 1))
    return pl.pallas_call(
        functools.partial(_compress_kernel, nblk=nblk, pitch=pitch),
        grid=(bc, n_slab),
        in_specs=[pl.BlockSpec((1, rows, SLAB), lambda b, s: (b, 0, s)),
                  pl.BlockSpec((1, CMP_BLOCK, SLAB), lambda b, s: (s, 0, 0)),
                  per_e(w1_bd), per_e(b1_slab), per_e(w2), per_e(b2)],
        out_specs=pl.BlockSpec((1, 1, HEADS_PER_SLAB, nblk, HEAD_DIM),
                               lambda b, s: (b, s // slabs_per_e, s % slabs_per_e, 0, 0)),
        out_shape=jax.ShapeDtypeStruct((bc, 2, KV_HEADS, nblk, HEAD_DIM), F32),
        compiler_params=_cparams("parallel", "parallel"),
        name="compress",
    )(x, pe_slab, w1_bd, b1_slab, w2, b2)


GATHER_PAGES_PER_STEP = 8
CMP_PITCH = 40


def _gather_kernel(pt_ref, *refs, pitch):
    o_ref = refs[-1]
    page = refs[0].shape[1]
    for k, r in enumerate(refs[:-1]):
        if pitch == CMP_BLOCK:
            o_ref[0, k * page:(k + 1) * page, :] = r[0]
        else:
            for blk in range(page // CMP_BLOCK):
                base = (k * (page // CMP_BLOCK) + blk) * pitch
                o_ref[0, base:base + CMP_BLOCK, :] = r[0, blk * CMP_BLOCK:(blk + 1) * CMP_BLOCK, :]
                o_ref[0, base + CMP_BLOCK:base + pitch, :] = jnp.zeros((pitch - CMP_BLOCK, o_ref.shape[2]),
                                                                        o_ref.dtype)


def gather_pages(cache, page_table, pitch=CMP_BLOCK):
    n_pool, page, c = cache.shape
    db, n_pages = page_table.shape
    pps = GATHER_PAGES_PER_STEP
    assert n_pages % pps == 0 and page % CMP_BLOCK == 0
    out_page = page // CMP_BLOCK * pitch

    def page_spec(k):
        return pl.BlockSpec((1, page, c), lambda b, j, pt: (pt[b, j * pps + k], 0, 0))

    return pl.pallas_call(
        functools.partial(_gather_kernel, pitch=pitch),
        grid_spec=pltpu.PrefetchScalarGridSpec(
            num_scalar_prefetch=1,
            grid=(db, n_pages // pps),
            in_specs=[page_spec(k) for k in range(pps)],
            out_specs=pl.BlockSpec((1, pps * out_page, c), lambda b, j, pt: (b, j, 0)),
        ),
        out_shape=jax.ShapeDtypeStruct((db, n_pages * out_page, c), cache.dtype),
        compiler_params=_cparams("parallel", "parallel"),
        name="gather_pages",
    )(page_table, *([cache] * pps))


def _rel_bucket(dist):
    n = jnp.maximum(dist, 0)
    exact = N_BUCKETS // 2
    nf = jnp.maximum(n, 1).astype(F32)
    large = exact + (jnp.log(nf / exact) / math.log(MAX_DISTANCE / exact) * (N_BUCKETS - exact)).astype(jnp.int32)
    large = jnp.minimum(large, N_BUCKETS - 1)
    return jnp.where(n < exact, n, large)


class _BiasTable:
    def __init__(self, table, max_dist, pad):
        tab = table.astype(F32)

        def lookup(dist):
            bucket = _rel_bucket(dist)
            out = jnp.zeros((tab.shape[1], dist.shape[0]), F32)
            for k in range(N_BUCKETS):
                out = jnp.where(bucket[None, :] == k, tab[k][:, None], out)
            return out

        self.pad = pad
        self.n = max_dist
        self.ext = lookup(jnp.arange(-pad, max_dist))
        self.rev = lookup(max_dist - 1 - jnp.arange(pad + max_dist))

    def at(self, d):
        return self.ext[:, self.pad + d]

    def rising(self, d0, length):
        return lax.slice_in_dim(self.ext, self.pad + d0, self.pad + d0 + length, axis=1)

    def falling(self, d0, length, step=1):
        start = self.n - 1 - d0
        return lax.slice(self.rev, (0, start), (self.rev.shape[0], start + step * (length - 1) + 1), (1, step))

    def frames(self, d0, count, hop, length):
        reps = length // hop
        seq = self.rising(d0, hop * (count + reps - 1)).reshape(-1, count + reps - 1, hop)
        return jnp.concatenate([seq[:, a:a + count] for a in range(reps)], axis=-1)

    def toeplitz(self, d0, count, hop, size):
        assert hop == size
        period = 2 * size - 1
        base = self.rising(d0 - (size - 1), hop * (count + 1)).reshape(-1, count + 1, hop)
        w = jnp.concatenate([base[:, :count, size - 1:], base[:, 1:, :size - 1],
                             base[:, :count, :size - 1]], axis=-1)
        rows = jnp.tile(w, (1, 1, size))[:, :, :size * (period - 1)]
        return rows.reshape(w.shape[0], count, size, period - 1)[..., :size]


def _softmax_step(s, vt, m, l, acc):
    m_new = jnp.maximum(m, jnp.max(s, axis=0, keepdims=True))
    alpha = jnp.exp(m - m_new)
    p = jnp.exp(s - m_new)
    l = alpha * l + jnp.sum(p, axis=0, keepdims=True)
    acc = alpha * acc + _dot(vt, p.astype(BF16))
    return m_new, l, acc


def _attn_kernel(bfar_ref, q_ref, gt_ref, kaug_ref, vst_ref, kw_ref, vwt_ref, ck_ref, cvt_ref,
                 bc_ref, bn_ref, o_ref, imp_ref, sc_ref, s_ref, *, nb, nc):
    g = pl.program_id(0)
    qt = pl.program_id(2)
    qs = qt * QT
    gq = GROUP * QT

    lane_cat = lambda parts: jnp.concatenate(parts, axis=1)
    q_t = q_ref[0].astype(F32).T.astype(BF16)
    q_heads = [q_t[h * HEAD_DIM:(h + 1) * HEAD_DIM] for h in range(GROUP)]
    q_plain = lane_cat(q_heads)

    col_i = lax.broadcasted_iota(jnp.int32, (1, gq), 1) % QT

    s_c = _dot(ck_ref[0, 0], q_plain) + lane_cat([bc_ref[h, 0] for h in range(GROUP)])
    n_idx = lax.broadcasted_iota(jnp.int32, (nc, gq), 0)
    valid_c = (n_idx * CMP_BLOCK + CMP_BLOCK - 1) <= (qs + col_i)
    s_c = jnp.where(valid_c, s_c, M_INIT)
    m_c = jnp.max(s_c, axis=0, keepdims=True)
    p_c = jnp.where(valid_c, jnp.exp(s_c - m_c), 0.0)
    l_c = jnp.sum(p_c, axis=0, keepdims=True)
    p_c = p_c * jnp.where(l_c > 0.0, 1.0 / l_c, 0.0)
    o_c = _dot(cvt_ref[0, 0], p_c.astype(BF16))

    imp = p_c[:, 0:QT]
    for h in range(1, GROUP):
        imp = imp + p_c[:, h * QT:(h + 1) * QT]
    imp_ref[...] = imp
    ratio = SEL_BLOCK // CMP_BLOCK
    imp2 = imp_ref[pl.ds(0, nb, stride=ratio), :]
    for r in range(1, ratio):
        imp2 = imp2 + imp_ref[pl.ds(r, nb, stride=ratio), :]
    j_idx = lax.broadcasted_iota(jnp.int32, (nb, QT), 0)
    i_idx = lax.broadcasted_iota(jnp.int32, (nb, QT), 1)
    jq = (qs + i_idx) // SEL_BLOCK
    forced = (j_idx == 0) | (j_idx == jq) | (j_idx == jq - 1)
    score = jnp.where(j_idx <= jq, jnp.where(forced, FORCE, imp2), -1.0)
    sc_ref[...] = score

    sub8 = lax.broadcasted_iota(jnp.int32, (8, QT), 0)
    groups = [score[8 * r:8 * r + 8] for r in range(nb // 8)]
    counts = [jnp.zeros((8, QT), F32) for _ in groups]
    for c in range(nb):
        row = sc_ref[c:c + 1, :]
        for r, grp_score in enumerate(groups):
            if r > c // 8:
                beats = row >= grp_score
            elif r < c // 8:
                beats = row > grp_score
            else:
                tie = jnp.where(row == grp_score, 1.0, 0.0) * jnp.where(sub8 > c % 8, 1.0, 0.0)
                counts[r] = counts[r] + tie
                beats = row > grp_score
            counts[r] = counts[r] + jnp.where(beats, 1.0, 0.0)
    cnt = jnp.concatenate(counts, axis=0)
    selected = (cnt < float(min(N_SEL, nb))) & (score >= 0.0)
    mask_feat = jnp.where(selected, 0.0, MASK_NEG).astype(BF16)
    q_aug = lane_cat([jnp.concatenate([qh, mask_feat], axis=0) for qh in q_heads])

    far_row = lane_cat([jnp.full((1, QT), bfar_ref[g * GROUP + h], F32) for h in range(GROUP)])

    def near_bias(kt0, n_sub):
        parts = []
        for sub in range(n_sub):
            dm = jnp.clip(qt - (kt0 + sub), 0, NEAR_SPAN - 1)
            parts.append(lane_cat([bn_ref[h, dm] for h in range(GROUP)]))
        return jnp.concatenate(parts, axis=0)

    init = (jnp.full((1, gq), M_INIT, F32), jnp.zeros((1, gq), F32), jnp.zeros((HEAD_DIM, gq), F32))
    sub_per_chunk = KEY_STEP // QT
    row_k = lax.broadcasted_iota(jnp.int32, (KEY_STEP, gq), 0)

    last_chunk = qt // sub_per_chunk

    def raw_scores(c):
        ks = pl.multiple_of(jnp.minimum(c, last_chunk) * KEY_STEP, KEY_STEP)
        return _dot(kaug_ref[0, 0, pl.ds(ks, KEY_STEP), :], q_aug)

    def pipelined(finish):
        def body(c, carry):
            nxt = raw_scores(c + 1)
            cur = s_ref[c % 2]
            s_ref[(c + 1) % 2] = nxt
            ks = pl.multiple_of(c * KEY_STEP, KEY_STEP)
            return _softmax_step(finish(c, cur, ks), vst_ref[0, 0, :, pl.ds(ks, KEY_STEP)], *carry)
        return body

    def far_finish(c, s, ks):
        return s + far_row

    def near_finish(c, s, ks):
        return s + near_bias(c * sub_per_chunk, sub_per_chunk)

    s_ref[0] = raw_scores(0)
    n_far = jnp.maximum(qt - (NEAR_TILES - 1), 0) // sub_per_chunk
    carry = lax.fori_loop(0, n_far, pipelined(far_finish), init)
    carry = lax.fori_loop(n_far, last_chunk, pipelined(near_finish), carry)
    ks = pl.multiple_of(last_chunk * KEY_STEP, KEY_STEP)
    s = near_finish(last_chunk, s_ref[last_chunk % 2], ks)
    s = jnp.where(ks + row_k <= qs + col_i, s, MASK_NEG)
    m_s, l_s, acc_s = _softmax_step(s, vst_ref[0, 0, :, pl.ds(ks, KEY_STEP)], *carry)
    o_s = acc_s / l_s

    wt0 = jnp.maximum(qt - WINDOW // QT, 0)
    ws = pl.multiple_of(wt0 * QT, QT)
    wk = WINDOW + QT
    s_w = _dot(kw_ref[0, 0, pl.ds(ws, wk), :], q_plain) + near_bias(wt0, wk // QT)
    dist = (qs + col_i) - (ws + lax.broadcasted_iota(jnp.int32, (wk, gq), 0))
    s_w = jnp.where((dist >= 0) & (dist < WINDOW), s_w, MASK_NEG)
    p_w = jnp.exp(s_w - jnp.max(s_w, axis=0, keepdims=True))
    l_w = jnp.sum(p_w, axis=0, keepdims=True)
    o_w = _dot(vwt_ref[0, 0, :, pl.ds(ws, wk)], p_w.astype(BF16)) / l_w

    gate = lambda br: lane_cat([gt_ref[0, 0, h * 3 + br:h * 3 + br + 1, :] for h in range(GROUP)])
    o_t = gate(0) * o_c + gate(1) * o_s + gate(2) * o_w
    o_rows = jnp.concatenate([o_t[:, h * QT:(h + 1) * QT] for h in range(GROUP)], axis=0)
    o_ref[0] = o_rows.T.astype(BF16)


def prompt_attention(q, gates_t, kaug, vst, kwin, vwt, ck, cvt, bias_cmp, bias_near, bias_far):
    bsz, t, dq = q.shape
    nb = t // SEL_BLOCK
    nc = t // CMP_BLOCK
    nq = t // QT
    gd = GROUP * HEAD_DIM
    per_bg = lambda a: pl.BlockSpec((1, 1) + a.shape[2:], lambda g, b, i, *_: (b, g, 0, 0))
    grid_spec = pltpu.PrefetchScalarGridSpec(
        num_scalar_prefetch=1,
        grid=(KV_HEADS, bsz, nq),
        in_specs=[pl.BlockSpec((1, QT, gd), lambda g, b, i, *_: (b, i, g)),
                  pl.BlockSpec((1, 1, 3 * GROUP, QT), lambda g, b, i, *_: (b, g, 0, i)),
                  per_bg(kaug), per_bg(vst), per_bg(kwin), per_bg(vwt), per_bg(ck), per_bg(cvt),
                  pl.BlockSpec((GROUP, 1, nc, QT), lambda g, b, i, *_: (g, i, 0, 0)),
                  pl.BlockSpec((GROUP, NEAR_SPAN, QT, QT), lambda g, b, i, *_: (g, 0, 0, 0))],
        out_specs=pl.BlockSpec((1, QT, gd), lambda g, b, i, *_: (b, i, g)),
        scratch_shapes=[pltpu.VMEM((nc, QT), F32), pltpu.VMEM((nb, QT), F32),
                        pltpu.VMEM((2, KEY_STEP, GROUP * QT), F32)],
    )
    return pl.pallas_call(
        functools.partial(_attn_kernel, nb=nb, nc=nc),
        grid_spec=grid_spec,
        out_shape=jax.ShapeDtypeStruct((bsz, t, dq), BF16),
        compiler_params=_cparams("parallel", "parallel", "parallel"),
        name="prompt_attention",
    )(bias_far, q, gates_t, kaug, vst, kwin, vwt, ck, cvt, bias_cmp, bias_near)


KEY_CHUNK = 1024
NEW_PAD = 128


def _diag_heads(full, row_g):
    out = None
    for g in range(KV_HEADS):
        part = jnp.where(row_g == g, full[:, g * HEAD_DIM:(g + 1) * HEAD_DIM], 0.0)
        out = part if out is None else out + part
    return out


def _sample_attn_kernel(pt_ref, q_ref, gates_ref, *refs, past, ts, n_pages):
    page_refs = refs[:n_pages]
    (nsel_ref, kwin_ref, nwin_ref, ck_ref, cv_ref, bsel_ref, bwin_ref, bcmp_ref, e_ref, o_ref,
     s_scr) = refs[n_pages:]
    kvd = KV_HEADS * HEAD_DIM
    page = page_refs[0].shape[1]
    pages_per_chunk = KEY_CHUNK // page

    def cached(c, lo, hi):
        parts = [page_refs[c * pages_per_chunk + j][0, :, lo:hi] for j in range(pages_per_chunk)]
        return jnp.concatenate(parts, axis=0).astype(BF16)

    rows = GROUP * KV_HEADS * ts
    nc = ck_ref.shape[1]
    q = q_ref[0]
    r_idx = lax.broadcasted_iota(jnp.int32, (rows, 1), 0)
    row_q = r_idx % ts
    row_g = (r_idx // ts) % KV_HEADS

    s_c = _dot_nt(q, ck_ref[0]) + bcmp_ref[...]
    p_c = jnp.exp(s_c - jnp.max(s_c, axis=1, keepdims=True))
    p_c = p_c / jnp.sum(p_c, axis=1, keepdims=True)
    o_c = _diag_heads(_dot(p_c.astype(BF16), cv_ref[0]), row_g)

    per = KV_HEADS * ts
    imp = p_c[0:per]
    for h in range(1, GROUP):
        imp = imp + p_c[h * per:(h + 1) * per]
    ratio = SEL_BLOCK // CMP_BLOCK
    imp2 = imp
    for r in range(1, ratio):
        imp2 = imp2 + pltpu.roll(imp, nc - r, axis=1)
    lane = lax.broadcasted_iota(jnp.int32, (per, nc), 1)
    blk = lane // ratio
    jq = past // SEL_BLOCK
    forced = (blk == 0) | (blk == jq - 1)
    score = jnp.where(lane % ratio == 0, jnp.where(forced, FORCE, imp2), -2.0)
    cnt = jnp.where(score < FORCE, 1.0, 0.0)
    for c in range(nc // ratio):
        col = score[:, c * ratio:c * ratio + 1]
        ge = jnp.where(col >= score, 1.0, 0.0)
        gt = jnp.where(col > score, 1.0, 0.0)
        cnt = cnt + jnp.where(lane > c * ratio, ge, gt)
    selected = (cnt < float(min(N_SEL, jq + 1))) & (score >= 0.0)
    mask_feat = jnp.where(selected, 0.0, MASK_NEG).astype(BF16)
    mask_feat = jnp.concatenate([mask_feat] * GROUP, axis=0)

    for c in range(past // KEY_CHUNK):
        sl = slice(c * KEY_CHUNK, (c + 1) * KEY_CHUNK)
        s_scr[:, sl] = _dot_nt(q, cached(c, 0, kvd)) + bsel_ref[:, sl] + _dot(mask_feat, e_ref[:, sl])
    new_j = lax.broadcasted_iota(jnp.int32, (rows, NEW_PAD), 1)
    new_ok = new_j <= row_q
    s_new = _dot_nt(q, nsel_ref[0, :, 0:kvd].astype(BF16)) + bsel_ref[:, past:past + NEW_PAD]
    s_scr[:, past:past + NEW_PAD] = jnp.where(new_ok, s_new, MASK_NEG)
    s_all = s_scr[...]
    p_s = jnp.exp(s_all - jnp.max(s_all, axis=1, keepdims=True))
    l_s = jnp.sum(p_s, axis=1, keepdims=True)
    s_scr[...] = p_s
    acc = _dot(s_scr[:, past:past + NEW_PAD].astype(BF16), nsel_ref[0, :, kvd:2 * kvd].astype(BF16))
    for c in range(past // KEY_CHUNK):
        sl = slice(c * KEY_CHUNK, (c + 1) * KEY_CHUNK)
        acc = acc + _dot(s_scr[:, sl].astype(BF16), cached(c, kvd, 2 * kvd))
    o_s = _diag_heads(acc, row_g) / l_s

    wlen = kwin_ref.shape[1]
    win_i = lax.broadcasted_iota(jnp.int32, (rows, wlen), 1)
    s_w1 = _dot_nt(q, kwin_ref[0, :, 0:kvd].astype(BF16)) + bwin_ref[:, 0:wlen]
    s_w1 = jnp.where(win_i + (WINDOW - wlen) > row_q, s_w1, MASK_NEG)
    s_w2 = _dot_nt(q, nwin_ref[0, :, 0:kvd].astype(BF16)) + bwin_ref[:, wlen:wlen + NEW_PAD]
    s_w2 = jnp.where(new_ok, s_w2, MASK_NEG)
    m_w = jnp.maximum(jnp.max(s_w1, axis=1, keepdims=True), jnp.max(s_w2, axis=1, keepdims=True))
    p_w1 = jnp.exp(s_w1 - m_w)
    p_w2 = jnp.exp(s_w2 - m_w)
    l_w = jnp.sum(p_w1, axis=1, keepdims=True) + jnp.sum(p_w2, axis=1, keepdims=True)
    acc_w = (_dot(p_w1.astype(BF16), kwin_ref[0, :, kvd:2 * kvd].astype(BF16))
             + _dot(p_w2.astype(BF16), nwin_ref[0, :, kvd:2 * kvd].astype(BF16)))
    o_w = _diag_heads(acc_w, row_g) / l_w

    gts = gates_ref[0]
    o_ref[0] = gts[:, 0:1] * o_c + gts[:, 1:2] * o_s + gts[:, 2:3] * o_w


def sample_attention(page_table, q_bd, gates, cache, nsel, kwin, nwin, ck, cv, bsel, bwin, bcmp, e_mat, ts):
    db, rows, kvd = q_bd.shape
    n_pages = page_table.shape[1]
    page = cache.shape[1]
    past = n_pages * page
    assert KEY_CHUNK % page == 0 and past % KEY_CHUNK == 0
    seq = lambda a: pl.BlockSpec((1,) + a.shape[1:], lambda b, pt: (b,) + (0,) * (a.ndim - 1))
    const = lambda a: pl.BlockSpec(a.shape, lambda b, pt: (0,) * a.ndim)

    def page_spec(k):
        return pl.BlockSpec((1,) + cache.shape[1:], lambda b, pt: (pt[b, k], 0, 0))

    return pl.pallas_call(
        functools.partial(_sample_attn_kernel, past=past, ts=ts, n_pages=n_pages),
        grid_spec=pltpu.PrefetchScalarGridSpec(
            num_scalar_prefetch=1,
            grid=(db,),
            in_specs=[seq(q_bd), seq(gates)] + [page_spec(k) for k in range(n_pages)]
            + [seq(nsel), seq(kwin), seq(nwin), seq(ck), seq(cv), const(bsel), const(bwin), const(bcmp),
               const(e_mat)],
            out_specs=pl.BlockSpec((1, rows, HEAD_DIM), lambda b, pt: (b, 0, 0)),
            scratch_shapes=[pltpu.VMEM((rows, past + NEW_PAD), F32)],
        ),
        out_shape=jax.ShapeDtypeStruct((db, rows, HEAD_DIM), F32),
        compiler_params=_cparams("parallel"),
        name="sample_attention",
    )(page_table, q_bd, gates, *([cache] * n_pages), nsel, kwin, nwin, ck, cv, bsel, bwin, bcmp, e_mat)


def _prep_weights(P):
    W = {}
    cast = lambda a: a.astype(BF16)
    for k in ("w_pw1", "w_pw2", "w_kv", "w_o", "w_gate", "w_up", "w_down", "w_phi1", "w_phi2"):
        W[k] = cast(P[k])
    nq = N_HEADS * HEAD_DIM
    wqg = P["w_qg"]
    pad = 128 - (wqg.shape[2] - nq)
    W["w_qg"] = cast(jnp.pad(wqg, ((0, 0), (0, 0), (0, pad))))
    return W


def _trunk(x, mods, is_prompt, caches, P, W, bias_dist):
    bsz, t, d = x.shape
    m = bsz * t
    depth = P["w_gate"].shape[0]
    n_a = P["w_pw1"].shape[0]
    if is_prompt:
        tm, tpg = 256, t // 256
    else:
        tm, tpg = m, 1
    x2 = x.reshape(m, d)
    row2 = lambda a: a.reshape(1, -1)
    conv_states = []
    y = None
    for l in range(depth):
        shift, scale, gate = mods[(l, 0)]
        g0 = row2(P["norm_g"][l, 0])
        if l < n_a:
            u = pw1_glu(x2, g0, shift, scale, W["w_pw1"][l], row2(P["b_pw1"][l]), tm, tpg)
            u3 = u.reshape(bsz, t, d)
            wdw = jnp.pad(P["w_dw"][l], ((0, CONV_PAD - CONV_W), (0, 0)))
            args = (wdw, row2(P["b_dw"][l]), row2(P["ln_g"][l]), row2(P["ln_b"][l]), W["w_pw2"][l],
                    row2(P["b_pw2"][l]))
            if is_prompt:
                conv_states.append(u3[:, -(CONV_W - 1):])
                x2 = conv_pw2_residual(u3, *args, x2, gate, tm, tpg)
            else:
                hist = jnp.pad(caches["state_conv"][l], ((0, 0), (CONV_PAD - CONV_W + 1, 0), (0, 0)))
                full = jnp.concatenate([hist, u3], axis=1)
                conv_states.append(full[:, -(CONV_W - 1):])
                x2 = conv_pw2_residual_small(full, *args, x2.reshape(bsz, t, d),
                                             gate.reshape(bsz, t, d)[:, 0:1]).reshape(m, d)
        else:
            if l == n_a:
                rows = kv_proj(x2, row2(P["g_kv"]), W["w_kv"], tm)
                rows6 = rows.reshape(bsz, t, 3, 2, KV_HEADS, HEAD_DIM)
                if is_prompt:
                    att = _prompt_attention_setup(rows6, P, W, bias_dist)
                    win_state = rows6[:, -min(WINDOW, t):, 2]
                else:
                    att = _sample_attention_setup(rows6, caches, P, W, bias_dist)
                    wl = caches["cache_kv_win"].shape[1]
                    win_state = jnp.concatenate([caches["cache_kv_win"], rows6[:, :, 2]], axis=1)[:, -wl:]
            lb = l - n_a
            q, gates = qg_proj(x2, g0, shift, scale, W["w_qg"][lb], tm, tpg)
            if is_prompt:
                o = _prompt_attention_layer(q.reshape(bsz, t, -1), gates.reshape(bsz, t, -1), att)
            else:
                o = _sample_attention_layer(q.reshape(bsz, t, -1), gates.reshape(bsz, t, -1), att)
            x2 = o_proj_residual(o.reshape(m, -1), W["w_o"][lb], x2, gate, tm, tpg)
        shift, scale, gate = mods[(l, 1)]
        fg = row2(P["final_g"]) if l == depth - 1 else None
        out = ffn(x2, row2(P["norm_g"][l, 1]), shift, scale, gate, W["w_gate"][l], W["w_up"][l],
                  W["w_down"][l], tm, tpg, final_g=fg)
        if l == depth - 1:
            y = out
        else:
            x2 = out
    return (y.reshape(bsz, t, d), rows6[:, :, 0], rows6[:, :, 1], win_state, jnp.stack(conv_states))


def _compress_weights(P, W):
    pe = P["pe_cmp"]
    pe_slab = jnp.repeat(jnp.tile(pe, (1, 1, HEADS_PER_SLAB)), KV_HEADS // HEADS_PER_SLAB, axis=0)
    w1 = W["w_phi1"].reshape(2, CMP_BLOCK, HEAD_DIM, -1)
    zero = jnp.zeros_like(w1)
    w1_bd = jnp.concatenate([jnp.concatenate([w1, zero], axis=-1), jnp.concatenate([zero, w1], axis=-1)],
                            axis=2)
    b1_slab = jnp.tile(P["b_phi1"], (1, HEADS_PER_SLAB))[:, None, :]
    return pe_slab, w1_bd, b1_slab, W["w_phi2"], P["b_phi2"][:, None, :]


def _prompt_attention_setup(rows6, P, W, bias_dist):
    bsz, t = rows6.shape[:2]
    nb = t // SEL_BLOCK
    nc = t // CMP_BLOCK
    nq = t // QT
    cmp = compress(rows6.reshape(bsz, t, -1), nc, CMP_BLOCK, *_compress_weights(P, W))
    ck = cmp[:, 0].astype(BF16)
    cvt = cmp[:, 1].transpose(0, 1, 3, 2).astype(BF16)
    onehot = (jnp.arange(t)[:, None] // SEL_BLOCK == jnp.arange(nb)[None, :]).astype(BF16)
    ksel = rows6[:, :, 1, 0].transpose(0, 2, 1, 3).astype(BF16)
    kaug = jnp.concatenate([ksel, jnp.broadcast_to(onehot, (bsz, KV_HEADS, t, nb))], axis=-1)
    vst = rows6[:, :, 1, 1].transpose(0, 2, 3, 1).astype(BF16)
    kwin = rows6[:, :, 2, 0].transpose(0, 2, 1, 3).astype(BF16)
    vwt = rows6[:, :, 2, 1].transpose(0, 2, 3, 1).astype(BF16)
    per_qt = QT // CMP_BLOCK
    n_u = nc + per_qt * (nq - 1)
    frames = bias_dist.frames(-(CMP_BLOCK - 1) - CMP_BLOCK * (nc - 1), n_u, CMP_BLOCK, QT)[:, ::-1]
    bias_cmp = jnp.stack([frames[:, per_qt * (nq - 1 - i):per_qt * (nq - 1 - i) + nc] for i in range(nq)],
                         axis=1)
    bias_near = bias_dist.toeplitz(0, NEAR_SPAN, QT, QT)
    bias_far = bias_dist.at(NEAR_TILES * QT)
    return dict(kaug=kaug, vst=vst, kwin=kwin, vwt=vwt, ck=ck, cvt=cvt, bias_cmp=bias_cmp,
                bias_near=bias_near, bias_far=bias_far)


def _prompt_attention_layer(q, gates, att):
    bsz, t, _ = q.shape
    gates_t = gates[:, :, :3 * N_HEADS].reshape(bsz, t, KV_HEADS, 3 * GROUP).transpose(0, 2, 3, 1)
    return prompt_attention(q, gates_t, att["kaug"], att["vst"], att["kwin"], att["vwt"], att["ck"],
                            att["cvt"], att["bias_cmp"], att["bias_near"], att["bias_far"])


def _sample_attention_setup(rows6, caches, P, W, bias_dist):
    db, ts = rows6.shape[:2]
    pt = caches["page_table"]
    n_pages = pt.shape[1]
    page = caches["cache_kv_cmp"].shape[1]
    past = n_pages * page
    c = 2 * KV_HEADS * HEAD_DIM
    cmp_g = gather_pages(caches["cache_kv_cmp"].reshape(-1, page, c), pt, pitch=CMP_PITCH)
    sel_cache = caches["cache_kv_sel"].reshape(-1, page, c)
    nc = past // CMP_BLOCK
    assert (past + ts) // CMP_BLOCK == nc and past % KEY_CHUNK == 0
    cmp = compress(cmp_g, nc, CMP_PITCH, *_compress_weights(P, W))
    ck = cmp[:, 0].transpose(0, 2, 1, 3).reshape(db, nc, KV_HEADS * HEAD_DIM).astype(BF16)
    cv = cmp[:, 1].transpose(0, 2, 1, 3).reshape(db, nc, KV_HEADS * HEAD_DIM).astype(BF16)
    pad_new = lambda a: jnp.pad(a.reshape(db, ts, c), ((0, 0), (0, NEW_PAD - ts), (0, 0)))
    nsel = pad_new(rows6[:, :, 1])
    nwin = pad_new(rows6[:, :, 2])
    kwin = caches["cache_kv_win"].reshape(db, -1, c)
    wlen = kwin.shape[1]
    rows = GROUP * KV_HEADS * ts
    def by_row(f):
        per_q = jnp.stack([f(q) for q in range(ts)], axis=0)
        per_q = per_q.reshape(ts, KV_HEADS, GROUP, -1).transpose(2, 1, 0, 3)
        return per_q.reshape(rows, -1)

    bsel = by_row(lambda q: bias_dist.falling(past + q, past + NEW_PAD))
    bwin = by_row(lambda q: jnp.concatenate([bias_dist.falling(wlen + q, wlen),
                                             bias_dist.falling(q, NEW_PAD)], axis=1))
    bcmp = by_row(lambda q: bias_dist.falling(past + q - (CMP_BLOCK - 1), nc, step=CMP_BLOCK))
    ratio = SEL_BLOCK // CMP_BLOCK
    lane_blk = np.where(np.arange(nc) % ratio == 0, np.arange(nc) // ratio, -1)
    e_mat = jnp.asarray(lane_blk[:, None] == (np.arange(past) // SEL_BLOCK)[None, :], BF16)
    return dict(page_table=pt, sel_cache=sel_cache, nsel=nsel, kwin=kwin, nwin=nwin, ck=ck, cv=cv, bsel=bsel, bwin=bwin,
                bcmp=bcmp, e_mat=e_mat, past=past)


def _sample_attention_layer(q, gates, att):
    db, ts, _ = q.shape
    rows = GROUP * KV_HEADS * ts
    q5 = q.reshape(db, ts, KV_HEADS, GROUP, HEAD_DIM).transpose(0, 3, 2, 1, 4)
    eye = jnp.eye(KV_HEADS, dtype=q.dtype)
    q_bd = (q5[:, :, :, :, None, :] * eye[None, None, :, None, :, None]).reshape(db, rows, KV_HEADS * HEAD_DIM)
    g5 = gates[:, :, :3 * N_HEADS].reshape(db, ts, KV_HEADS, GROUP, 3).transpose(0, 3, 2, 1, 4)
    g_rows = jnp.pad(g5.reshape(db, rows, 3), ((0, 0), (0, 0), (0, 125)))
    o = sample_attention(att["page_table"], q_bd, g_rows, att["sel_cache"], att["nsel"], att["kwin"],
                         att["nwin"], att["ck"], att["cv"], att["bsel"], att["bwin"], att["bcmp"],
                         att["e_mat"], ts)
    o = o.reshape(db, GROUP, KV_HEADS, ts, HEAD_DIM).transpose(0, 3, 2, 1, 4)
    return o.reshape(db, ts, N_HEADS * HEAD_DIM).astype(BF16)


def kernel(x_prompt, x_sample, c_prompt, c_sample, cache_kv_cmp, cache_kv_sel, cache_kv_win, state_conv, page_table, w_ada, b_ada, norm_g, w_pw1, b_pw1, w_dw, b_dw, ln_g, ln_b, w_pw2, b_pw2, g_kv, w_kv, w_phi1, b_phi1, w_phi2, b_phi2, pe_cmp, w_qg, w_o, rel_table, w_gate, w_up, w_down, final_g):
    P = dict(w_ada=w_ada, b_ada=b_ada, norm_g=norm_g, w_pw1=w_pw1, b_pw1=b_pw1, w_dw=w_dw, b_dw=b_dw,
             ln_g=ln_g, ln_b=ln_b, w_pw2=w_pw2, b_pw2=b_pw2, g_kv=g_kv, w_kv=w_kv, w_phi1=w_phi1,
             b_phi1=b_phi1, w_phi2=w_phi2, b_phi2=b_phi2, pe_cmp=pe_cmp, w_qg=w_qg, w_o=w_o,
             rel_table=rel_table, w_gate=w_gate, w_up=w_up, w_down=w_down, final_g=final_g)
    W = _prep_weights(P)
    bp, tp, d = x_prompt.shape
    db, ts, _ = x_sample.shape
    depth = w_ada.shape[0]

    n_c = bp + db
    r_pad = -(-n_c // 8) * 8
    c_all = jnp.pad(jnp.concatenate([c_prompt, c_sample], axis=0), ((0, r_pad - n_c), (0, 0)))
    mod = ada_modulation(c_all, w_ada.reshape(depth * 2, d, 3 * d), b_ada.reshape(depth * 2, 1, 3 * d))
    mods_p, mods_s = {}, {}
    for l in range(depth):
        for j in range(2):
            parts = [mod[l * 2 + j, :, k * d:(k + 1) * d] for k in range(3)]
            mods_p[(l, j)] = tuple(p[:bp, None, :] for p in parts)
            mods_s[(l, j)] = tuple(jnp.repeat(p[bp:n_c], ts, axis=0)[None] for p in parts)

    past = page_table.shape[1] * cache_kv_cmp.shape[1]
    bias_dist = _BiasTable(rel_table, max(tp, past + 2 * NEW_PAD), pad=max(tp, NEW_PAD))
    caches = dict(cache_kv_cmp=cache_kv_cmp, cache_kv_sel=cache_kv_sel, cache_kv_win=cache_kv_win,
                  state_conv=state_conv, page_table=page_table)
    y_p, cmp_p, sel_p, win_p, conv_p = _trunk(x_prompt, mods_p, True, None, P, W, bias_dist)
    y_s, cmp_s, sel_s, win_s, conv_s = _trunk(x_sample, mods_s, False, caches, P, W, bias_dist)
    return (y_p, y_s, cmp_p, cmp_s, sel_p, sel_s, win_p, win_s, conv_p, conv_s)
```

```python
import functools
import math

import numpy as np
import jax
import jax.numpy as jnp
from jax import lax
from jax.experimental import pallas as pl
from jax.experimental.pallas import tpu as pltpu

F32 = jnp.float32
BF16 = jnp.bfloat16

N_HEADS = 16
KV_HEADS = 4
GROUP = N_HEADS // KV_HEADS
HEAD_DIM = 64
CONV_W = 31
CONV_PAD = 32
SUBLANES = 8
CMP_BLOCK = 32
SEL_BLOCK = 64
N_SEL = 16
WINDOW = 512
N_BUCKETS = 32
MAX_DISTANCE = 1024
EPS = 1e-6
FORCE = 1e4
MASK_NEG = -1e9
M_INIT = -1e30
QT = 128
NEAR_TILES = 8
KEY_STEP = 512
NEAR_SPAN = NEAR_TILES + KEY_STEP // QT - 1
V7X_VMEM_LIMIT_BYTES = 56 * 1024 * 1024


def _cparams(*sem):
    return pltpu.CompilerParams(dimension_semantics=sem, vmem_limit_bytes=V7X_VMEM_LIMIT_BYTES)


def _full_spec(a):
    nd = a.ndim
    return pl.BlockSpec(a.shape, lambda *_: (0,) * nd)


def _silu(x):
    return x * jax.nn.sigmoid(x)


def _norm_mod(x, g, shift, scale):
    ms = jnp.mean(x * x, axis=-1, keepdims=True)
    return x * lax.rsqrt(ms + EPS) * g * (1.0 + scale) + shift


def _dot(a, b):
    return jnp.dot(a, b, preferred_element_type=F32)


def _dot_nt(a, b):
    return lax.dot_general(a, b, (((1,), (1,)), ((), ())), preferred_element_type=F32)


def _ada_kernel(c_ref, w_ref, b_ref, o_ref):
    c = c_ref[...]
    o_ref[0] = _dot(_silu(c), w_ref[0]) + b_ref[0]


def ada_modulation(c_all, w_ada, b_ada):
    nl, d, n3 = w_ada.shape
    r = c_all.shape[0]
    tn = n3 // 2
    return pl.pallas_call(
        _ada_kernel,
        grid=(nl, n3 // tn),
        in_specs=[pl.BlockSpec((r, d), lambda l, j: (0, 0)),
                  pl.BlockSpec((1, d, tn), lambda l, j: (l, 0, j)),
                  pl.BlockSpec((1, 1, tn), lambda l, j: (l, 0, j))],
        out_specs=pl.BlockSpec((1, r, tn), lambda l, j: (l, 0, j)),
        out_shape=jax.ShapeDtypeStruct((nl, r, n3), F32),
        compiler_params=_cparams("parallel", "parallel"),
        name="ada_modulation",
    )(c_all, w_ada, b_ada)


def _row_spec(tm, width):
    return pl.BlockSpec((tm, width), lambda i: (i, 0))


def _mod_spec(mod, tiles_per_group):
    _, r, d = mod.shape
    return pl.BlockSpec((None, r, d), lambda i: (i // tiles_per_group, 0, 0))


def _pw1_kernel(x_ref, g_ref, sh_ref, sc_ref, w_ref, b_ref, u_ref):
    d = x_ref.shape[1]
    h = _norm_mod(x_ref[...], g_ref[...], sh_ref[...], sc_ref[...]).astype(BF16)
    z = _dot(h, w_ref[...]) + b_ref[...]
    u_ref[...] = z[:, :d] * jax.nn.sigmoid(z[:, d:])


def pw1_glu(x, g, shift, scale, w, b, tm, tpg):
    m, d = x.shape
    return pl.pallas_call(
        _pw1_kernel,
        grid=(m // tm,),
        in_specs=[_row_spec(tm, d), _full_spec(g), _mod_spec(shift, tpg), _mod_spec(scale, tpg),
                  _full_spec(w), _full_spec(b)],
        out_specs=_row_spec(tm, d),
        out_shape=jax.ShapeDtypeStruct((m, d), F32),
        compiler_params=_cparams("parallel"),
        name="pw1_glu",
    )(x, g, shift, scale, w, b)


def _conv_tail(rows_at, wdw_ref, bdw_ref, lg_ref, lb_ref, w2_ref, b2_ref):
    acc = None
    for k in range(CONV_W):
        term = rows_at(k + CONV_PAD - CONV_W + 1) * wdw_ref[k:k + 1, :]
        acc = term if acc is None else acc + term
    acc = acc + bdw_ref[...]
    mu = jnp.mean(acc, axis=-1, keepdims=True)
    cen = acc - mu
    var = jnp.mean(cen * cen, axis=-1, keepdims=True)
    yn = cen * lax.rsqrt(var + EPS) * lg_ref[...] + lb_ref[...]
    return _dot(_silu(yn).astype(BF16), w2_ref[...]) + b2_ref[...]


def _conv_kernel(halo_ref, u_ref, wdw_ref, bdw_ref, lg_ref, lb_ref, w2_ref, b2_ref, x_ref, gate_ref,
                 o_ref, buf, shifted, *, tiles_per_seq):
    tm = x_ref.shape[0]
    first = pl.program_id(0) % tiles_per_seq == 0
    buf[0:CONV_PAD, :] = jnp.where(first, 0.0, halo_ref[0])
    buf[CONV_PAD:CONV_PAD + tm, :] = u_ref[0]
    span = tm + CONV_PAD - SUBLANES
    for s in range(1, SUBLANES):
        shifted[s - 1] = buf[pl.ds(s, span), :]

    def rows_at(o):
        a, s = divmod(o, SUBLANES)
        if s == 0:
            return buf[pl.ds(o, tm), :]
        return shifted[s - 1, pl.ds(a * SUBLANES, tm), :]

    out = _conv_tail(rows_at, wdw_ref, bdw_ref, lg_ref, lb_ref, w2_ref, b2_ref)
    o_ref[...] = x_ref[...] + gate_ref[...] * out


def conv_pw2_residual(u, wdw, bdw, lg, lb, w2, b2, x, gate, tm, tpg):
    m, d = x.shape
    bsz, t, _ = u.shape
    assert t % tm == 0 and tm % CONV_PAD == 0 and m == bsz * t
    tps = t // tm
    return pl.pallas_call(
        functools.partial(_conv_kernel, tiles_per_seq=tps),
        grid=(m // tm,),
        in_specs=[pl.BlockSpec((1, CONV_PAD, d),
                               lambda i: (i // tps, jnp.maximum((i % tps) * (tm // CONV_PAD) - 1, 0), 0)),
                  pl.BlockSpec((1, tm, d), lambda i: (i // tps, i % tps, 0)),
                  _full_spec(wdw), _full_spec(bdw), _full_spec(lg), _full_spec(lb),
                  _full_spec(w2), _full_spec(b2), _row_spec(tm, d), _mod_spec(gate, tpg)],
        out_specs=_row_spec(tm, d),
        out_shape=jax.ShapeDtypeStruct((m, d), F32),
        scratch_shapes=[pltpu.VMEM((tm + CONV_PAD, d), F32),
                        pltpu.VMEM((SUBLANES - 1, tm + CONV_PAD - SUBLANES, d), F32)],
        compiler_params=_cparams("parallel"),
        name="conv_pw2_residual",
    )(u, u, wdw, bdw, lg, lb, w2, b2, x, gate)


def _conv_small_kernel(full_ref, wdw_ref, bdw_ref, lg_ref, lb_ref, w2_ref, b2_ref, x_ref, gate_ref,
                       o_ref):
    tm = x_ref.shape[1]
    rows_at = lambda o: full_ref[0, pl.ds(o, tm), :]
    out = _conv_tail(rows_at, wdw_ref, bdw_ref, lg_ref, lb_ref, w2_ref, b2_ref)
    o_ref[0] = x_ref[0] + gate_ref[0] * out


def conv_pw2_residual_small(full, wdw, bdw, lg, lb, w2, b2, x, gate):
    bsz, ts, d = x.shape
    seq = lambda a: pl.BlockSpec((1,) + a.shape[1:], lambda i: (i, 0, 0))
    return pl.pallas_call(
        _conv_small_kernel,
        grid=(bsz,),
        in_specs=[seq(full), _full_spec(wdw), _full_spec(bdw), _full_spec(lg), _full_spec(lb),
                  _full_spec(w2), _full_spec(b2), seq(x), seq(gate)],
        out_specs=seq(x),
        out_shape=jax.ShapeDtypeStruct(x.shape, F32),
        compiler_params=_cparams("parallel"),
        name="conv_pw2_residual_small",
    )(full, wdw, bdw, lg, lb, w2, b2, x, gate)


def _ffn_kernel(x_ref, g_ref, sh_ref, sc_ref, gate_ref, wg_ref, wu_ref, wd_ref, *rest, final):
    x = x_ref[...]
    h = _norm_mod(x, g_ref[...], sh_ref[...], sc_ref[...]).astype(BF16)
    a = _dot(h, wg_ref[...])
    b = _dot(h, wu_ref[...])
    act = (_silu(a) * b).astype(BF16)
    xn = x + gate_ref[...] * _dot(act, wd_ref[...])
    if final:
        fg_ref, o_ref = rest
        ms = jnp.mean(xn * xn, axis=-1, keepdims=True)
        o_ref[...] = xn * lax.rsqrt(ms + EPS) * fg_ref[...]
    else:
        (o_ref,) = rest
        o_ref[...] = xn


def ffn(x, g, shift, scale, gate, wg, wu, wd, tm, tpg, final_g=None):
    m, d = x.shape
    final = final_g is not None
    ins = [x, g, shift, scale, gate, wg, wu, wd] + ([final_g] if final else [])
    specs = [_row_spec(tm, d), _full_spec(g), _mod_spec(shift, tpg), _mod_spec(scale, tpg),
             _mod_spec(gate, tpg), _full_spec(wg), _full_spec(wu), _full_spec(wd)]
    if final:
        specs.append(_full_spec(final_g))
    return pl.pallas_call(
        functools.partial(_ffn_kernel, final=final),
        grid=(m // tm,),
        in_specs=specs,
        out_specs=_row_spec(tm, d),
        out_shape=jax.ShapeDtypeStruct((m, d), F32),
        compiler_params=_cparams("parallel"),
        name="ffn_final" if final else "ffn",
    )(*ins)


def _kvproj_kernel(x_ref, g_ref, w_ref, o_ref):
    x = x_ref[...]
    ms = jnp.mean(x * x, axis=-1, keepdims=True)
    h = (x * lax.rsqrt(ms + EPS) * g_ref[...]).astype(BF16)
    o_ref[...] = _dot(h, w_ref[...])


def kv_proj(x, g, w, tm):
    m, d = x.shape
    n = w.shape[1]
    return pl.pallas_call(
        _kvproj_kernel,
        grid=(m // tm,),
        in_specs=[_row_spec(tm, d), _full_spec(g), _full_spec(w)],
        out_specs=_row_spec(tm, n),
        out_shape=jax.ShapeDtypeStruct((m, n), F32),
        compiler_params=_cparams("parallel"),
        name="kv_proj",
    )(x, g, w)


def _qg_kernel(x_ref, g_ref, sh_ref, sc_ref, w_ref, q_ref, gates_ref):
    nq = q_ref.shape[1]
    h = _norm_mod(x_ref[...], g_ref[...], sh_ref[...], sc_ref[...]).astype(BF16)
    z = _dot(h, w_ref[...])
    q_ref[...] = (z[:, :nq] * (HEAD_DIM ** -0.5)).astype(BF16)
    gates_ref[...] = jax.nn.sigmoid(z[:, nq:])


def qg_proj(x, g, shift, scale, w, tm, tpg):
    m, d = x.shape
    nq = N_HEADS * HEAD_DIM
    ng = w.shape[1] - nq
    return pl.pallas_call(
        _qg_kernel,
        grid=(m // tm,),
        in_specs=[_row_spec(tm, d), _full_spec(g), _mod_spec(shift, tpg), _mod_spec(scale, tpg),
                  _full_spec(w)],
        out_specs=[_row_spec(tm, nq), _row_spec(tm, ng)],
        out_shape=[jax.ShapeDtypeStruct((m, nq), BF16), jax.ShapeDtypeStruct((m, ng), F32)],
        compiler_params=_cparams("parallel"),
        name="qg_proj",
    )(x, g, shift, scale, w)


def _oproj_kernel(o_ref, w_ref, x_ref, gate_ref, out_ref):
    out_ref[...] = x_ref[...] + gate_ref[...] * _dot(o_ref[...], w_ref[...])


def o_proj_residual(o, w, x, gate, tm, tpg):
    m, d = x.shape
    return pl.pallas_call(
        _oproj_kernel,
        grid=(m // tm,),
        in_specs=[_row_spec(tm, o.shape[1]), _full_spec(w), _row_spec(tm, d), _mod_spec(gate, tpg)],
        out_specs=_row_spec(tm, d),
        out_shape=jax.ShapeDtypeStruct((m, d), F32),
        compiler_params=_cparams("parallel"),
        name="o_proj_residual",
    )(o, w, x, gate)


SLAB = 128
HEADS_PER_SLAB = SLAB // HEAD_DIM


R_PER_DOT = 2
CMP_PITCH = 40


def _compress_core(rows_at, pe_row, w1_at, b1, w2, b2):
    hidden = w2.shape[0]
    acc = None
    for i in range(CMP_BLOCK // R_PER_DOT):
        parts = [(rows_at(R_PER_DOT * i + j) + pe_row(R_PER_DOT * i + j)).astype(BF16) for j in range(R_PER_DOT)]
        term = _dot(jnp.concatenate(parts, axis=1), w1_at(i))
        acc = term if acc is None else acc + term
    hid = _silu(acc + b1)
    return [_dot(hid[:, hh * hidden:(hh + 1) * hidden].astype(BF16), w2) + b2 for hh in range(HEADS_PER_SLAB)]


def _compress_kernel(x_ref, pe_ref, w1_ref, b1_ref, w2_ref, b2_ref, o_ref, *, nblk):
    outs = _compress_core(lambda r: x_ref[0, pl.ds(r, nblk, stride=CMP_BLOCK), :],
                          lambda r: pe_ref[0, r:r + 1, :], lambda i: w1_ref[0, i],
                          b1_ref[0], w2_ref[0], b2_ref[0])
    for hh, out in enumerate(outs):
        o_ref[0, 0, hh] = out


def compress(x, nblk, pe_slab, w1_bd, b1_slab, w2, b2):
    bc, rows, _ = x.shape
    n_slab = 2 * KV_HEADS // HEADS_PER_SLAB
    slabs_per_e = n_slab // 2
    per_e = lambda a: pl.BlockSpec((1,) + a.shape[1:], lambda b, s: (s // slabs_per_e,) + (0,) * (a.ndim - 1))
    return pl.pallas_call(
        functools.partial(_compress_kernel, nblk=nblk),
        grid=(bc, n_slab),
        in_specs=[pl.BlockSpec((1, rows, SLAB), lambda b, s: (b, 0, s)),
                  pl.BlockSpec((1, CMP_BLOCK, SLAB), lambda b, s: (s, 0, 0)),
                  per_e(w1_bd), per_e(b1_slab), per_e(w2), per_e(b2)],
        out_specs=pl.BlockSpec((1, 1, HEADS_PER_SLAB, nblk, HEAD_DIM),
                               lambda b, s: (b, s // slabs_per_e, s % slabs_per_e, 0, 0)),
        out_shape=jax.ShapeDtypeStruct((bc, 2, KV_HEADS, nblk, HEAD_DIM), F32),
        compiler_params=_cparams("parallel", "parallel"),
        name="compress",
    )(x, pe_slab, w1_bd, b1_slab, w2, b2)


def _compress_paged_kernel(pt_ref, *refs, n_pages):
    page_refs = refs[:n_pages]
    pe_ref, w1_ref, b1_ref, w2_ref, b2_ref, o_ref, scr = refs[n_pages:]
    blocks_per_page = page_refs[0].shape[1] // CMP_BLOCK
    nblk = n_pages * blocks_per_page
    n_slab = pe_ref.shape[0]
    slabs_per_e = n_slab // 2
    for s in range(n_slab):
        e = s // slabs_per_e
        for p in range(n_pages):
            for j in range(blocks_per_page):
                base = (p * blocks_per_page + j) * CMP_PITCH
                scr[base:base + CMP_BLOCK, :] = page_refs[p][0, j * CMP_BLOCK:(j + 1) * CMP_BLOCK,
                                                             s * SLAB:(s + 1) * SLAB]
        outs = _compress_core(lambda r: scr[pl.ds(r, nblk, stride=CMP_PITCH), :],
                              lambda r: pe_ref[s, r:r + 1, :], lambda i: w1_ref[e, i],
                              b1_ref[e], w2_ref[e], b2_ref[e])
        for hh, out in enumerate(outs):
            o_ref[0, e, (s % slabs_per_e) * HEADS_PER_SLAB + hh] = out


def compress_paged(cache, page_table, pe_slab, w1_bd, b1_slab, w2, b2):
    n_pool, page, c = cache.shape
    db, n_pages = page_table.shape
    assert page % CMP_BLOCK == 0
    nblk = n_pages * page // CMP_BLOCK
    const = lambda a: pl.BlockSpec(a.shape, lambda b, pt: (0,) * a.ndim)

    def page_spec(k):
        return pl.BlockSpec((1, page, c), lambda b, pt: (pt[b, k], 0, 0))

    return pl.pallas_call(
        functools.partial(_compress_paged_kernel, n_pages=n_pages),
        grid_spec=pltpu.PrefetchScalarGridSpec(
            num_scalar_prefetch=1,
            grid=(db,),
            in_specs=[page_spec(k) for k in range(n_pages)]
            + [const(pe_slab), const(w1_bd), const(b1_slab), const(w2), const(b2)],
            out_specs=pl.BlockSpec((1, 2, KV_HEADS, nblk, HEAD_DIM), lambda b, pt: (b, 0, 0, 0, 0)),
            scratch_shapes=[pltpu.VMEM((nblk * CMP_PITCH, SLAB), F32)],
        ),
        out_shape=jax.ShapeDtypeStruct((db, 2, KV_HEADS, nblk, HEAD_DIM), F32),
        compiler_params=_cparams("parallel"),
        name="compress_paged",
    )(page_table, *([cache] * n_pages), pe_slab, w1_bd, b1_slab, w2, b2)


def _rel_bucket(dist):
    n = jnp.maximum(dist, 0)
    exact = N_BUCKETS // 2
    nf = jnp.maximum(n, 1).astype(F32)
    large = exact + (jnp.log(nf / exact) / math.log(MAX_DISTANCE / exact) * (N_BUCKETS - exact)).astype(jnp.int32)
    large = jnp.minimum(large, N_BUCKETS - 1)
    return jnp.where(n < exact, n, large)


class _BiasTable:
    def __init__(self, table, max_dist, pad):
        tab = table.astype(F32)

        def lookup(dist):
            bucket = _rel_bucket(dist)
            out = jnp.zeros((tab.shape[1], dist.shape[0]), F32)
            for k in range(N_BUCKETS):
                out = jnp.where(bucket[None, :] == k, tab[k][:, None], out)
            return out

        self.pad = pad
        self.n = max_dist
        self.ext = lookup(jnp.arange(-pad, max_dist))
        self.rev = lookup(max_dist - 1 - jnp.arange(pad + max_dist))

    def at(self, d):
        return self.ext[:, self.pad + d]

    def rising(self, d0, length):
        return lax.slice_in_dim(self.ext, self.pad + d0, self.pad + d0 + length, axis=1)

    def falling(self, d0, length, step=1):
        start = self.n - 1 - d0
        return lax.slice(self.rev, (0, start), (self.rev.shape[0], start + step * (length - 1) + 1), (1, step))

    def frames(self, d0, count, hop, length):
        reps = length // hop
        seq = self.rising(d0, hop * (count + reps - 1)).reshape(-1, count + reps - 1, hop)
        return jnp.concatenate([seq[:, a:a + count] for a in range(reps)], axis=-1)

    def toeplitz(self, d0, count, hop, size):
        assert hop == size
        period = 2 * size - 1
        base = self.rising(d0 - (size - 1), hop * (count + 1)).reshape(-1, count + 1, hop)
        w = jnp.concatenate([base[:, :count, size - 1:], base[:, 1:, :size - 1],
                             base[:, :count, :size - 1]], axis=-1)
        rows = jnp.tile(w, (1, 1, size))[:, :, :size * (period - 1)]
        return rows.reshape(w.shape[0], count, size, period - 1)[..., :size]


def _softmax_step(s, vt, m, l, acc):
    m_new = jnp.maximum(m, jnp.max(s, axis=0, keepdims=True))
    alpha = jnp.exp(m - m_new)
    p = jnp.exp(s - m_new)
    l = alpha * l + jnp.sum(p, axis=0, keepdims=True)
    acc = alpha * acc + _dot(vt, p.astype(BF16))
    return m_new, l, acc


def _attn_kernel(bfar_ref, q_ref, gt_ref, kaug_ref, vst_ref, kw_ref, vwt_ref, ck_ref, cvt_ref,
                 bc_ref, bn_ref, o_ref, imp_ref, sc_ref, s_ref, *, nb, nc):
    g = pl.program_id(0)
    qt = pl.program_id(2)
    qs = qt * QT
    gq = GROUP * QT

    lane_cat = lambda parts: jnp.concatenate(parts, axis=1)
    q_t = q_ref[0].astype(F32).T.astype(BF16)
    q_heads = [q_t[h * HEAD_DIM:(h + 1) * HEAD_DIM] for h in range(GROUP)]
    q_plain = lane_cat(q_heads)

    col_i = lax.broadcasted_iota(jnp.int32, (1, gq), 1) % QT

    s_c = _dot(ck_ref[0, 0], q_plain) + lane_cat([bc_ref[h, 0] for h in range(GROUP)])
    n_idx = lax.broadcasted_iota(jnp.int32, (nc, gq), 0)
    valid_c = (n_idx * CMP_BLOCK + CMP_BLOCK - 1) <= (qs + col_i)
    s_c = jnp.where(valid_c, s_c, M_INIT)
    m_c = jnp.max(s_c, axis=0, keepdims=True)
    p_c = jnp.where(valid_c, jnp.exp(s_c - m_c), 0.0)
    l_c = jnp.sum(p_c, axis=0, keepdims=True)
    p_c = p_c * jnp.where(l_c > 0.0, 1.0 / l_c, 0.0)
    o_c = _dot(cvt_ref[0, 0], p_c.astype(BF16))

    imp = p_c[:, 0:QT]
    for h in range(1, GROUP):
        imp = imp + p_c[:, h * QT:(h + 1) * QT]
    imp_ref[...] = imp
    ratio = SEL_BLOCK // CMP_BLOCK
    imp2 = imp_ref[pl.ds(0, nb, stride=ratio), :]
    for r in range(1, ratio):
        imp2 = imp2 + imp_ref[pl.ds(r, nb, stride=ratio), :]
    j_idx = lax.broadcasted_iota(jnp.int32, (nb, QT), 0)
    i_idx = lax.broadcasted_iota(jnp.int32, (nb, QT), 1)
    jq = (qs + i_idx) // SEL_BLOCK
    forced = (j_idx == 0) | (j_idx == jq) | (j_idx == jq - 1)
    score = jnp.where(j_idx <= jq, jnp.where(forced, FORCE, imp2), -1.0)
    sc_ref[...] = score

    sub8 = lax.broadcasted_iota(jnp.int32, (8, QT), 0)
    groups = [score[8 * r:8 * r + 8] for r in range(nb // 8)]
    counts = [jnp.zeros((8, QT), F32) for _ in groups]
    for c in range(nb):
        row = sc_ref[c:c + 1, :]
        for r, grp_score in enumerate(groups):
            if r > c // 8:
                beats = row >= grp_score
            elif r < c // 8:
                beats = row > grp_score
            else:
                tie = jnp.where(row == grp_score, 1.0, 0.0) * jnp.where(sub8 > c % 8, 1.0, 0.0)
                counts[r] = counts[r] + tie
                beats = row > grp_score
            counts[r] = counts[r] + jnp.where(beats, 1.0, 0.0)
    cnt = jnp.concatenate(counts, axis=0)
    selected = (cnt < float(min(N_SEL, nb))) & (score >= 0.0)
    mask_feat = jnp.where(selected, 0.0, MASK_NEG).astype(BF16)
    q_aug = lane_cat([jnp.concatenate([qh, mask_feat], axis=0) for qh in q_heads])

    far_row = lane_cat([jnp.full((1, QT), bfar_ref[g * GROUP + h], F32) for h in range(GROUP)])

    def near_bias(kt0, n_sub):
        parts = []
        for sub in range(n_sub):
            dm = jnp.clip(qt - (kt0 + sub), 0, NEAR_SPAN - 1)
            parts.append(lane_cat([bn_ref[h, dm] for h in range(GROUP)]))
        return jnp.concatenate(parts, axis=0)

    init = (jnp.full((1, gq), M_INIT, F32), jnp.zeros((1, gq), F32), jnp.zeros((HEAD_DIM, gq), F32))
    sub_per_chunk = KEY_STEP // QT
    row_k = lax.broadcasted_iota(jnp.int32, (KEY_STEP, gq), 0)

    last_chunk = qt // sub_per_chunk

    def raw_scores(c):
        ks = pl.multiple_of(jnp.minimum(c, last_chunk) * KEY_STEP, KEY_STEP)
        return _dot(kaug_ref[0, 0, pl.ds(ks, KEY_STEP), :], q_aug)

    def pipelined(finish):
        def body(c, carry):
            nxt = raw_scores(c + 1)
            cur = s_ref[c % 2]
            s_ref[(c + 1) % 2] = nxt
            ks = pl.multiple_of(c * KEY_STEP, KEY_STEP)
            return _softmax_step(finish(c, cur, ks), vst_ref[0, 0, :, pl.ds(ks, KEY_STEP)], *carry)
        return body

    def far_finish(c, s, ks):
        return s + far_row

    def near_finish(c, s, ks):
        return s + near_bias(c * sub_per_chunk, sub_per_chunk)

    s_ref[0] = raw_scores(0)
    n_far = jnp.maximum(qt - (NEAR_TILES - 1), 0) // sub_per_chunk
    carry = lax.fori_loop(0, n_far, pipelined(far_finish), init)
    carry = lax.fori_loop(n_far, last_chunk, pipelined(near_finish), carry)
    ks = pl.multiple_of(last_chunk * KEY_STEP, KEY_STEP)
    s = near_finish(last_chunk, s_ref[last_chunk % 2], ks)
    s = jnp.where(ks + row_k <= qs + col_i, s, MASK_NEG)
    m_s, l_s, acc_s = _softmax_step(s, vst_ref[0, 0, :, pl.ds(ks, KEY_STEP)], *carry)
    o_s = acc_s / l_s

    wt0 = jnp.maximum(qt - WINDOW // QT, 0)
    ws = pl.multiple_of(wt0 * QT, QT)
    wk = WINDOW + QT
    s_w = _dot(kw_ref[0, 0, pl.ds(ws, wk), :], q_plain) + near_bias(wt0, wk // QT)
    dist = (qs + col_i) - (ws + lax.broadcasted_iota(jnp.int32, (wk, gq), 0))
    s_w = jnp.where((dist >= 0) & (dist < WINDOW), s_w, MASK_NEG)
    p_w = jnp.exp(s_w - jnp.max(s_w, axis=0, keepdims=True))
    l_w = jnp.sum(p_w, axis=0, keepdims=True)
    o_w = _dot(vwt_ref[0, 0, :, pl.ds(ws, wk)], p_w.astype(BF16)) / l_w

    gate = lambda br: lane_cat([gt_ref[0, 0, h * 3 + br:h * 3 + br + 1, :] for h in range(GROUP)])
    o_t = gate(0) * o_c + gate(1) * o_s + gate(2) * o_w
    o_rows = jnp.concatenate([o_t[:, h * QT:(h + 1) * QT] for h in range(GROUP)], axis=0)
    o_ref[0] = o_rows.T.astype(BF16)


def prompt_attention(q, gates_t, kaug, vst, kwin, vwt, ck, cvt, bias_cmp, bias_near, bias_far):
    bsz, t, dq = q.shape
    nb = t // SEL_BLOCK
    nc = t // CMP_BLOCK
    nq = t // QT
    gd = GROUP * HEAD_DIM
    per_bg = lambda a: pl.BlockSpec((1, 1) + a.shape[2:], lambda g, b, i, *_: (b, g, 0, 0))
    grid_spec = pltpu.PrefetchScalarGridSpec(
        num_scalar_prefetch=1,
        grid=(KV_HEADS, bsz, nq),
        in_specs=[pl.BlockSpec((1, QT, gd), lambda g, b, i, *_: (b, i, g)),
                  pl.BlockSpec((1, 1, 3 * GROUP, QT), lambda g, b, i, *_: (b, g, 0, i)),
                  per_bg(kaug), per_bg(vst), per_bg(kwin), per_bg(vwt), per_bg(ck), per_bg(cvt),
                  pl.BlockSpec((GROUP, 1, nc, QT), lambda g, b, i, *_: (g, i, 0, 0)),
                  pl.BlockSpec((GROUP, NEAR_SPAN, QT, QT), lambda g, b, i, *_: (g, 0, 0, 0))],
        out_specs=pl.BlockSpec((1, QT, gd), lambda g, b, i, *_: (b, i, g)),
        scratch_shapes=[pltpu.VMEM((nc, QT), F32), pltpu.VMEM((nb, QT), F32),
                        pltpu.VMEM((2, KEY_STEP, GROUP * QT), F32)],
    )
    return pl.pallas_call(
        functools.partial(_attn_kernel, nb=nb, nc=nc),
        grid_spec=grid_spec,
        out_shape=jax.ShapeDtypeStruct((bsz, t, dq), BF16),
        compiler_params=_cparams("parallel", "parallel", "parallel"),
        name="prompt_attention",
    )(bias_far, q, gates_t, kaug, vst, kwin, vwt, ck, cvt, bias_cmp, bias_near)


KEY_CHUNK = 1024
NEW_PAD = 128


def _diag_heads(full, row_g):
    out = None
    for g in range(KV_HEADS):
        part = jnp.where(row_g == g, full[:, g * HEAD_DIM:(g + 1) * HEAD_DIM], 0.0)
        out = part if out is None else out + part
    return out


def _sample_attn_kernel(pt_ref, q_ref, gates_ref, *refs, past, ts, n_pages):
    page_refs = refs[:n_pages]
    (nsel_ref, kwin_ref, nwin_ref, ck_ref, cv_ref, bsel_ref, bwin_ref, bcmp_ref, e_ref, o_ref,
     s_scr) = refs[n_pages:]
    kvd = KV_HEADS * HEAD_DIM
    page = page_refs[0].shape[1]
    pages_per_chunk = KEY_CHUNK // page

    def cached(c, lo, hi):
        parts = [page_refs[c * pages_per_chunk + j][0, :, lo:hi] for j in range(pages_per_chunk)]
        return jnp.concatenate(parts, axis=0).astype(BF16)

    rows = GROUP * KV_HEADS * ts
    nc = ck_ref.shape[1]
    q = q_ref[0]
    r_idx = lax.broadcasted_iota(jnp.int32, (rows, 1), 0)
    row_q = r_idx % ts
    row_g = (r_idx // ts) % KV_HEADS

    s_c = _dot_nt(q, ck_ref[0]) + bcmp_ref[...]
    p_c = jnp.exp(s_c - jnp.max(s_c, axis=1, keepdims=True))
    p_c = p_c / jnp.sum(p_c, axis=1, keepdims=True)
    o_c = _diag_heads(_dot(p_c.astype(BF16), cv_ref[0]), row_g)

    per = KV_HEADS * ts
    imp = p_c[0:per]
    for h in range(1, GROUP):
        imp = imp + p_c[h * per:(h + 1) * per]
    ratio = SEL_BLOCK // CMP_BLOCK
    imp2 = imp
    for r in range(1, ratio):
        imp2 = imp2 + pltpu.roll(imp, nc - r, axis=1)
    lane = lax.broadcasted_iota(jnp.int32, (per, nc), 1)
    blk = lane // ratio
    jq = past // SEL_BLOCK
    forced = (blk == 0) | (blk == jq - 1)
    score = jnp.where(lane % ratio == 0, jnp.where(forced, FORCE, imp2), -2.0)
    cnt = jnp.where(score < FORCE, 1.0, 0.0)
    for c in range(nc // ratio):
        col = score[:, c * ratio:c * ratio + 1]
        ge = jnp.where(col >= score, 1.0, 0.0)
        gt = jnp.where(col > score, 1.0, 0.0)
        cnt = cnt + jnp.where(lane > c * ratio, ge, gt)
    selected = (cnt < float(min(N_SEL, jq + 1))) & (score >= 0.0)
    mask_feat = jnp.where(selected, 0.0, MASK_NEG).astype(BF16)
    mask_feat = jnp.concatenate([mask_feat] * GROUP, axis=0)

    for c in range(past // KEY_CHUNK):
        sl = slice(c * KEY_CHUNK, (c + 1) * KEY_CHUNK)
        s_scr[:, sl] = _dot_nt(q, cached(c, 0, kvd)) + bsel_ref[:, sl] + _dot(mask_feat, e_ref[:, sl])
    new_j = lax.broadcasted_iota(jnp.int32, (rows, NEW_PAD), 1)
    new_ok = new_j <= row_q
    s_new = _dot_nt(q, nsel_ref[0, :, 0:kvd].astype(BF16)) + bsel_ref[:, past:past + NEW_PAD]
    s_scr[:, past:past + NEW_PAD] = jnp.where(new_ok, s_new, MASK_NEG)
    s_all = s_scr[...]
    p_s = jnp.exp(s_all - jnp.max(s_all, axis=1, keepdims=True))
    l_s = jnp.sum(p_s, axis=1, keepdims=True)
    s_scr[...] = p_s
    acc = _dot(s_scr[:, past:past + NEW_PAD].astype(BF16), nsel_ref[0, :, kvd:2 * kvd].astype(BF16))
    for c in range(past // KEY_CHUNK):
        sl = slice(c * KEY_CHUNK, (c + 1) * KEY_CHUNK)
        acc = acc + _dot(s_scr[:, sl].astype(BF16), cached(c, kvd, 2 * kvd))
    o_s = _diag_heads(acc, row_g) / l_s

    wlen = kwin_ref.shape[1]
    win_i = lax.broadcasted_iota(jnp.int32, (rows, wlen), 1)
    s_w1 = _dot_nt(q, kwin_ref[0, :, 0:kvd].astype(BF16)) + bwin_ref[:, 0:wlen]
    s_w1 = jnp.where(win_i + (WINDOW - wlen) > row_q, s_w1, MASK_NEG)
    s_w2 = _dot_nt(q, nwin_ref[0, :, 0:kvd].astype(BF16)) + bwin_ref[:, wlen:wlen + NEW_PAD]
    s_w2 = jnp.where(new_ok, s_w2, MASK_NEG)
    m_w = jnp.maximum(jnp.max(s_w1, axis=1, keepdims=True), jnp.max(s_w2, axis=1, keepdims=True))
    p_w1 = jnp.exp(s_w1 - m_w)
    p_w2 = jnp.exp(s_w2 - m_w)
    l_w = jnp.sum(p_w1, axis=1, keepdims=True) + jnp.sum(p_w2, axis=1, keepdims=True)
    acc_w = (_dot(p_w1.astype(BF16), kwin_ref[0, :, kvd:2 * kvd].astype(BF16))
             + _dot(p_w2.astype(BF16), nwin_ref[0, :, kvd:2 * kvd].astype(BF16)))
    o_w = _diag_heads(acc_w, row_g) / l_w

    gts = gates_ref[0]
    o_ref[0] = gts[:, 0:1] * o_c + gts[:, 1:2] * o_s + gts[:, 2:3] * o_w


def sample_attention(page_table, q_bd, gates, cache, nsel, kwin, nwin, ck, cv, bsel, bwin, bcmp, e_mat, ts):
    db, rows, kvd = q_bd.shape
    n_pages = page_table.shape[1]
    page = cache.shape[1]
    past = n_pages * page
    assert KEY_CHUNK % page == 0 and past % KEY_CHUNK == 0
    seq = lambda a: pl.BlockSpec((1,) + a.shape[1:], lambda b, pt: (b,) + (0,) * (a.ndim - 1))
    const = lambda a: pl.BlockSpec(a.shape, lambda b, pt: (0,) * a.ndim)

    def page_spec(k):
        return pl.BlockSpec((1,) + cache.shape[1:], lambda b, pt: (pt[b, k], 0, 0))

    return pl.pallas_call(
        functools.partial(_sample_attn_kernel, past=past, ts=ts, n_pages=n_pages),
        grid_spec=pltpu.PrefetchScalarGridSpec(
            num_scalar_prefetch=1,
            grid=(db,),
            in_specs=[seq(q_bd), seq(gates)] + [page_spec(k) for k in range(n_pages)]
            + [seq(nsel), seq(kwin), seq(nwin), seq(ck), seq(cv), const(bsel), const(bwin), const(bcmp),
               const(e_mat)],
            out_specs=pl.BlockSpec((1, rows, HEAD_DIM), lambda b, pt: (b, 0, 0)),
            scratch_shapes=[pltpu.VMEM((rows, past + NEW_PAD), F32)],
        ),
        out_shape=jax.ShapeDtypeStruct((db, rows, HEAD_DIM), F32),
        compiler_params=_cparams("parallel"),
        name="sample_attention",
    )(page_table, q_bd, gates, *([cache] * n_pages), nsel, kwin, nwin, ck, cv, bsel, bwin, bcmp, e_mat)


def _prep_weights(P):
    W = {}
    cast = lambda a: a.astype(BF16)
    for k in ("w_pw1", "w_pw2", "w_kv", "w_o", "w_gate", "w_up", "w_down", "w_phi1", "w_phi2"):
        W[k] = cast(P[k])
    nq = N_HEADS * HEAD_DIM
    wqg = P["w_qg"]
    pad = 128 - (wqg.shape[2] - nq)
    W["w_qg"] = cast(jnp.pad(wqg, ((0, 0), (0, 0), (0, pad))))
    return W


def _trunk(x, mods, is_prompt, caches, P, W, bias_dist):
    bsz, t, d = x.shape
    m = bsz * t
    depth = P["w_gate"].shape[0]
    n_a = P["w_pw1"].shape[0]
    if is_prompt:
        tm, tpg = 256, t // 256
    else:
        tm, tpg = m, 1
    x2 = x.reshape(m, d)
    row2 = lambda a: a.reshape(1, -1)
    conv_states = []
    y = None
    for l in range(depth):
        shift, scale, gate = mods[(l, 0)]
        g0 = row2(P["norm_g"][l, 0])
        if l < n_a:
            u = pw1_glu(x2, g0, shift, scale, W["w_pw1"][l], row2(P["b_pw1"][l]), tm, tpg)
            u3 = u.reshape(bsz, t, d)
            wdw = jnp.pad(P["w_dw"][l], ((0, CONV_PAD - CONV_W), (0, 0)))
            args = (wdw, row2(P["b_dw"][l]), row2(P["ln_g"][l]), row2(P["ln_b"][l]), W["w_pw2"][l],
                    row2(P["b_pw2"][l]))
            if is_prompt:
                conv_states.append(u3[:, -(CONV_W - 1):])
                x2 = conv_pw2_residual(u3, *args, x2, gate, tm, tpg)
            else:
                hist = jnp.pad(caches["state_conv"][l], ((0, 0), (CONV_PAD - CONV_W + 1, 0), (0, 0)))
                full = jnp.concatenate([hist, u3], axis=1)
                conv_states.append(full[:, -(CONV_W - 1):])
                x2 = conv_pw2_residual_small(full, *args, x2.reshape(bsz, t, d),
                                             gate.reshape(bsz, t, d)[:, 0:1]).reshape(m, d)
        else:
            if l == n_a:
                rows = kv_proj(x2, row2(P["g_kv"]), W["w_kv"], tm)
                rows6 = rows.reshape(bsz, t, 3, 2, KV_HEADS, HEAD_DIM)
                if is_prompt:
                    att = _prompt_attention_setup(rows6, P, W, bias_dist)
                    win_state = rows6[:, -min(WINDOW, t):, 2]
                else:
                    att = _sample_attention_setup(rows6, caches, P, W, bias_dist)
                    wl = caches["cache_kv_win"].shape[1]
                    win_state = jnp.concatenate([caches["cache_kv_win"], rows6[:, :, 2]], axis=1)[:, -wl:]
            lb = l - n_a
            q, gates = qg_proj(x2, g0, shift, scale, W["w_qg"][lb], tm, tpg)
            if is_prompt:
                o = _prompt_attention_layer(q.reshape(bsz, t, -1), gates.reshape(bsz, t, -1), att)
            else:
                o = _sample_attention_layer(q.reshape(bsz, t, -1), gates.reshape(bsz, t, -1), att)
            x2 = o_proj_residual(o.reshape(m, -1), W["w_o"][lb], x2, gate, tm, tpg)
        shift, scale, gate = mods[(l, 1)]
        fg = row2(P["final_g"]) if l == depth - 1 else None
        out = ffn(x2, row2(P["norm_g"][l, 1]), shift, scale, gate, W["w_gate"][l], W["w_up"][l],
                  W["w_down"][l], tm, tpg, final_g=fg)
        if l == depth - 1:
            y = out
        else:
            x2 = out
    return (y.reshape(bsz, t, d), rows6[:, :, 0], rows6[:, :, 1], win_state, jnp.stack(conv_states))


def _compress_weights(P, W):
    pe = P["pe_cmp"]
    pe_slab = jnp.repeat(jnp.tile(pe, (1, 1, HEADS_PER_SLAB)), KV_HEADS // HEADS_PER_SLAB, axis=0)
    w1 = W["w_phi1"].reshape(2, CMP_BLOCK, HEAD_DIM, -1)
    zero = jnp.zeros_like(w1)
    w1_bd = jnp.concatenate([jnp.concatenate([w1, zero], axis=-1), jnp.concatenate([zero, w1], axis=-1)],
                            axis=2)
    w1_bd = w1_bd.reshape(2, CMP_BLOCK // R_PER_DOT, R_PER_DOT * SLAB, -1)
    b1_slab = jnp.tile(P["b_phi1"], (1, HEADS_PER_SLAB))[:, None, :]
    return pe_slab, w1_bd, b1_slab, W["w_phi2"], P["b_phi2"][:, None, :]


def _prompt_attention_setup(rows6, P, W, bias_dist):
    bsz, t = rows6.shape[:2]
    nb = t // SEL_BLOCK
    nc = t // CMP_BLOCK
    nq = t // QT
    cmp = compress(rows6.reshape(bsz, t, -1), nc, *_compress_weights(P, W))
    ck = cmp[:, 0].astype(BF16)
    cvt = cmp[:, 1].transpose(0, 1, 3, 2).astype(BF16)
    onehot = (jnp.arange(t)[:, None] // SEL_BLOCK == jnp.arange(nb)[None, :]).astype(BF16)
    ksel = rows6[:, :, 1, 0].transpose(0, 2, 1, 3).astype(BF16)
    kaug = jnp.concatenate([ksel, jnp.broadcast_to(onehot, (bsz, KV_HEADS, t, nb))], axis=-1)
    vst = rows6[:, :, 1, 1].transpose(0, 2, 3, 1).astype(BF16)
    kwin = rows6[:, :, 2, 0].transpose(0, 2, 1, 3).astype(BF16)
    vwt = rows6[:, :, 2, 1].transpose(0, 2, 3, 1).astype(BF16)
    per_qt = QT // CMP_BLOCK
    n_u = nc + per_qt * (nq - 1)
    frames = bias_dist.frames(-(CMP_BLOCK - 1) - CMP_BLOCK * (nc - 1), n_u, CMP_BLOCK, QT)[:, ::-1]
    bias_cmp = jnp.stack([frames[:, per_qt * (nq - 1 - i):per_qt * (nq - 1 - i) + nc] for i in range(nq)],
                         axis=1)
    bias_near = bias_dist.toeplitz(0, NEAR_SPAN, QT, QT)
    bias_far = bias_dist.at(NEAR_TILES * QT)
    return dict(kaug=kaug, vst=vst, kwin=kwin, vwt=vwt, ck=ck, cvt=cvt, bias_cmp=bias_cmp,
                bias_near=bias_near, bias_far=bias_far)


def _prompt_attention_layer(q, gates, att):
    bsz, t, _ = q.shape
    gates_t = gates[:, :, :3 * N_HEADS].reshape(bsz, t, KV_HEADS, 3 * GROUP).transpose(0, 2, 3, 1)
    return prompt_attention(q, gates_t, att["kaug"], att["vst"], att["kwin"], att["vwt"], att["ck"],
                            att["cvt"], att["bias_cmp"], att["bias_near"], att["bias_far"])


def _sample_attention_setup(rows6, caches, P, W, bias_dist):
    db, ts = rows6.shape[:2]
    pt = caches["page_table"]
    n_pages = pt.shape[1]
    page = caches["cache_kv_cmp"].shape[1]
    past = n_pages * page
    c = 2 * KV_HEADS * HEAD_DIM
    sel_cache = caches["cache_kv_sel"].reshape(-1, page, c).astype(BF16)
    nc = past // CMP_BLOCK
    assert (past + ts) // CMP_BLOCK == nc and past % KEY_CHUNK == 0
    cmp = compress_paged(caches["cache_kv_cmp"].reshape(-1, page, c), pt, *_compress_weights(P, W))
    ck = cmp[:, 0].transpose(0, 2, 1, 3).reshape(db, nc, KV_HEADS * HEAD_DIM).astype(BF16)
    cv = cmp[:, 1].transpose(0, 2, 1, 3).reshape(db, nc, KV_HEADS * HEAD_DIM).astype(BF16)
    pad_new = lambda a: jnp.pad(a.reshape(db, ts, c), ((0, 0), (0, NEW_PAD - ts), (0, 0)))
    nsel = pad_new(rows6[:, :, 1])
    nwin = pad_new(rows6[:, :, 2])
    kwin = caches["cache_kv_win"].reshape(db, -1, c)
    wlen = kwin.shape[1]
    rows = GROUP * KV_HEADS * ts
    def by_row(f):
        per_q = jnp.stack([f(q) for q in range(ts)], axis=0)
        per_q = per_q.reshape(ts, KV_HEADS, GROUP, -1).transpose(2, 1, 0, 3)
        return per_q.reshape(rows, -1)

    bsel = by_row(lambda q: bias_dist.falling(past + q, past + NEW_PAD))
    bwin = by_row(lambda q: jnp.concatenate([bias_dist.falling(wlen + q, wlen),
                                             bias_dist.falling(q, NEW_PAD)], axis=1))
    bcmp = by_row(lambda q: bias_dist.falling(past + q - (CMP_BLOCK - 1), nc, step=CMP_BLOCK))
    ratio = SEL_BLOCK // CMP_BLOCK
    lane_blk = np.where(np.arange(nc) % ratio == 0, np.arange(nc) // ratio, -1)
    e_mat = jnp.asarray(lane_blk[:, None] == (np.arange(past) // SEL_BLOCK)[None, :], BF16)
    return dict(page_table=pt, sel_cache=sel_cache, nsel=nsel, kwin=kwin, nwin=nwin, ck=ck, cv=cv, bsel=bsel, bwin=bwin,
                bcmp=bcmp, e_mat=e_mat, past=past)


def _sample_attention_layer(q, gates, att):
    db, ts, _ = q.shape
    rows = GROUP * KV_HEADS * ts
    q5 = q.reshape(db, ts, KV_HEADS, GROUP, HEAD_DIM).transpose(0, 3, 2, 1, 4)
    eye = jnp.eye(KV_HEADS, dtype=q.dtype)
    q_bd = (q5[:, :, :, :, None, :] * eye[None, None, :, None, :, None]).reshape(db, rows, KV_HEADS * HEAD_DIM)
    g5 = gates[:, :, :3 * N_HEADS].reshape(db, ts, KV_HEADS, GROUP, 3).transpose(0, 3, 2, 1, 4)
    g_rows = jnp.pad(g5.reshape(db, rows, 3), ((0, 0), (0, 0), (0, 125)))
    o = sample_attention(att["page_table"], q_bd, g_rows, att["sel_cache"], att["nsel"], att["kwin"],
                         att["nwin"], att["ck"], att["cv"], att["bsel"], att["bwin"], att["bcmp"],
                         att["e_mat"], ts)
    o = o.reshape(db, GROUP, KV_HEADS, ts, HEAD_DIM).transpose(0, 3, 2, 1, 4)
    return o.reshape(db, ts, N_HEADS * HEAD_DIM).astype(BF16)


def kernel(x_prompt, x_sample, c_prompt, c_sample, cache_kv_cmp, cache_kv_sel, cache_kv_win, state_conv, page_table, w_ada, b_ada, norm_g, w_pw1, b_pw1, w_dw, b_dw, ln_g, ln_b, w_pw2, b_pw2, g_kv, w_kv, w_phi1, b_phi1, w_phi2, b_phi2, pe_cmp, w_qg, w_o, rel_table, w_gate, w_up, w_down, final_g):
    P = dict(w_ada=w_ada, b_ada=b_ada, norm_g=norm_g, w_pw1=w_pw1, b_pw1=b_pw1, w_dw=w_dw, b_dw=b_dw,
             ln_g=ln_g, ln_b=ln_b, w_pw2=w_pw2, b_pw2=b_pw2, g_kv=g_kv, w_kv=w_kv, w_phi1=w_phi1,
             b_phi1=b_phi1, w_phi2=w_phi2, b_phi2=b_phi2, pe_cmp=pe_cmp, w_qg=w_qg, w_o=w_o,
             rel_table=rel_table, w_gate=w_gate, w_up=w_up, w_down=w_down, final_g=final_g)
    W = _prep_weights(P)
    bp, tp, d = x_prompt.shape
    db, ts, _ = x_sample.shape
    depth = w_ada.shape[0]

    n_c = bp + db
    r_pad = -(-n_c // 8) * 8
    c_all = jnp.pad(jnp.concatenate([c_prompt, c_sample], axis=0), ((0, r_pad - n_c), (0, 0)))
    mod = ada_modulation(c_all, w_ada.reshape(depth * 2, d, 3 * d), b_ada.reshape(depth * 2, 1, 3 * d))
    mods_p, mods_s = {}, {}
    for l in range(depth):
        for j in range(2):
            parts = [mod[l * 2 + j, :, k * d:(k + 1) * d] for k in range(3)]
            mods_p[(l, j)] = tuple(p[:bp, None, :] for p in parts)
            mods_s[(l, j)] = tuple(jnp.repeat(p[bp:n_c], ts, axis=0)[None] for p in parts)

    past = page_table.shape[1] * cache_kv_cmp.shape[1]
    bias_dist = _BiasTable(rel_table, max(tp, past + 2 * NEW_PAD), pad=max(tp, NEW_PAD))
    caches = dict(cache_kv_cmp=cache_kv_cmp, cache_kv_sel=cache_kv_sel, cache_kv_win=cache_kv_win,
                  state_conv=state_conv, page_table=page_table)
    y_p, cmp_p, sel_p, win_p, conv_p = _trunk(x_prompt, mods_p, True, None, P, W, bias_dist)
    y_s, cmp_s, sel_s, win_s, conv_s = _trunk(x_sample, mods_s, False, caches, P, W, bias_dist)
    return (y_p, y_s, cmp_p, cmp_s, sel_p, sel_s, win_p, win_s, conv_p, conv_s)
```

```python
import functools
import math

import numpy as np
import jax
import jax.numpy as jnp
from jax import lax
from jax.experimental import pallas as pl
from jax.experimental.pallas import tpu as pltpu

F32 = jnp.float32
BF16 = jnp.bfloat16

N_HEADS = 16
KV_HEADS = 4
GROUP = N_HEADS // KV_HEADS
HEAD_DIM = 64
CONV_W = 31
CONV_PAD = 32
SUBLANES = 8
CMP_BLOCK = 32
SEL_BLOCK = 64
N_SEL = 16
WINDOW = 512
N_BUCKETS = 32
MAX_DISTANCE = 1024
EPS = 1e-6
FORCE = 1e4
MASK_NEG = -1e9
M_INIT = -1e30
QT = 128
NEAR_TILES = 8
KEY_STEP = 512
NEAR_SPAN = NEAR_TILES + KEY_STEP // QT - 1
V7X_VMEM_LIMIT_BYTES = 56 * 1024 * 1024


def _cparams(*sem):
    return pltpu.CompilerParams(dimension_semantics=sem, vmem_limit_bytes=V7X_VMEM_LIMIT_BYTES)


class _Slice:
    def __init__(self, arr, block, index):
        self.arr, self.block, self.index = arr, block, index
        self.shape = tuple(b for b in block if b is not None)

    def spec(self, tiles_per_group=1):
        index = self.index
        return pl.BlockSpec(self.block, lambda i, *_: index(i // tiles_per_group))


def _layer(arr, l):
    nd = arr.ndim
    return _Slice(arr, (None,) + arr.shape[1:], lambda _: (l,) + (0,) * (nd - 1))


def _arr(a):
    return a.arr if isinstance(a, _Slice) else a


def _full_spec(a):
    if isinstance(a, _Slice):
        return a.spec()
    nd = a.ndim
    return pl.BlockSpec(a.shape, lambda *_: (0,) * nd)


def _silu(x):
    return x * jax.nn.sigmoid(x)


def _norm_mod(x, g, shift, scale):
    ms = jnp.mean(x * x, axis=-1, keepdims=True)
    return x * lax.rsqrt(ms + EPS) * g * (1.0 + scale) + shift


def _dot(a, b):
    return jnp.dot(a, b, preferred_element_type=F32)


def _dot_nt(a, b):
    return lax.dot_general(a, b, (((1,), (1,)), ((), ())), preferred_element_type=F32)


def _ada_kernel(c_ref, w_ref, b_ref, o_ref):
    c = c_ref[...]
    o_ref[0] = _dot(_silu(c), w_ref[0]) + b_ref[0]


def ada_modulation(c_all, w_ada, b_ada):
    nl, d, n3 = w_ada.shape
    r = c_all.shape[0]
    tn = n3 // 2
    return pl.pallas_call(
        _ada_kernel,
        grid=(nl, n3 // tn),
        in_specs=[pl.BlockSpec((r, d), lambda l, j: (0, 0)),
                  pl.BlockSpec((1, d, tn), lambda l, j: (l, 0, j)),
                  pl.BlockSpec((1, 1, tn), lambda l, j: (l, 0, j))],
        out_specs=pl.BlockSpec((1, r, tn), lambda l, j: (l, 0, j)),
        out_shape=jax.ShapeDtypeStruct((nl, r, n3), F32),
        compiler_params=_cparams("parallel", "parallel"),
        name="ada_modulation",
    )(c_all, w_ada, b_ada)


def _row_spec(tm, width):
    return pl.BlockSpec((tm, width), lambda i: (i, 0))


def _mod_spec(mod, tiles_per_group):
    return mod.spec(tiles_per_group)


def _pw1_kernel(x_ref, g_ref, sh_ref, sc_ref, w_ref, b_ref, u_ref):
    d = x_ref.shape[1]
    h = _norm_mod(x_ref[...], g_ref[...], sh_ref[...], sc_ref[...]).astype(BF16)
    z = _dot(h, w_ref[...]) + b_ref[...]
    u_ref[...] = z[:, :d] * jax.nn.sigmoid(z[:, d:])


def pw1_glu(x, g, shift, scale, w, b, tm, tpg):
    m, d = x.shape
    return pl.pallas_call(
        _pw1_kernel,
        grid=(m // tm,),
        in_specs=[_row_spec(tm, d), _full_spec(g), _mod_spec(shift, tpg), _mod_spec(scale, tpg),
                  _full_spec(w), _full_spec(b)],
        out_specs=_row_spec(tm, d),
        out_shape=jax.ShapeDtypeStruct((m, d), F32),
        compiler_params=_cparams("parallel"),
        name="pw1_glu",
    )(*map(_arr, (x, g, shift, scale, w, b)))


def _conv_tail(rows_at, wdw_ref, bdw_ref, lg_ref, lb_ref, w2_ref, b2_ref):
    acc = None
    for k in range(CONV_W):
        term = rows_at(k + CONV_PAD - CONV_W + 1) * wdw_ref[k:k + 1, :]
        acc = term if acc is None else acc + term
    acc = acc + bdw_ref[...]
    mu = jnp.mean(acc, axis=-1, keepdims=True)
    cen = acc - mu
    var = jnp.mean(cen * cen, axis=-1, keepdims=True)
    yn = cen * lax.rsqrt(var + EPS) * lg_ref[...] + lb_ref[...]
    return _dot(_silu(yn).astype(BF16), w2_ref[...]) + b2_ref[...]


def _conv_kernel(halo_ref, u_ref, wdw_ref, bdw_ref, lg_ref, lb_ref, w2_ref, b2_ref, x_ref, gate_ref,
                 o_ref, buf, shifted, *, tiles_per_seq):
    tm = x_ref.shape[0]
    first = pl.program_id(0) % tiles_per_seq == 0
    buf[0:CONV_PAD, :] = jnp.where(first, 0.0, halo_ref[0])
    buf[CONV_PAD:CONV_PAD + tm, :] = u_ref[0]
    span = tm + CONV_PAD - SUBLANES
    for s in range(1, SUBLANES):
        shifted[s - 1] = buf[pl.ds(s, span), :]

    def rows_at(o):
        a, s = divmod(o, SUBLANES)
        if s == 0:
            return buf[pl.ds(o, tm), :]
        return shifted[s - 1, pl.ds(a * SUBLANES, tm), :]

    out = _conv_tail(rows_at, wdw_ref, bdw_ref, lg_ref, lb_ref, w2_ref, b2_ref)
    o_ref[...] = x_ref[...] + gate_ref[...] * out


def conv_pw2_residual(u, wdw, bdw, lg, lb, w2, b2, x, gate, tm, tpg):
    m, d = x.shape
    bsz, t, _ = u.shape
    assert t % tm == 0 and tm % CONV_PAD == 0 and m == bsz * t
    tps = t // tm
    return pl.pallas_call(
        functools.partial(_conv_kernel, tiles_per_seq=tps),
        grid=(m // tm,),
        in_specs=[pl.BlockSpec((1, CONV_PAD, d),
                               lambda i: (i // tps, jnp.maximum((i % tps) * (tm // CONV_PAD) - 1, 0), 0)),
                  pl.BlockSpec((1, tm, d), lambda i: (i // tps, i % tps, 0)),
                  _full_spec(wdw), _full_spec(bdw), _full_spec(lg), _full_spec(lb),
                  _full_spec(w2), _full_spec(b2), _row_spec(tm, d), _mod_spec(gate, tpg)],
        out_specs=_row_spec(tm, d),
        out_shape=jax.ShapeDtypeStruct((m, d), F32),
        scratch_shapes=[pltpu.VMEM((tm + CONV_PAD, d), F32),
                        pltpu.VMEM((SUBLANES - 1, tm + CONV_PAD - SUBLANES, d), F32)],
        compiler_params=_cparams("parallel"),
        name="conv_pw2_residual",
    )(*map(_arr, (u, u, wdw, bdw, lg, lb, w2, b2, x, gate)))


def _conv_small_kernel(full_ref, wdw_ref, bdw_ref, lg_ref, lb_ref, w2_ref, b2_ref, x_ref, gate_ref,
                       o_ref):
    tm = x_ref.shape[1]
    rows_at = lambda o: full_ref[0, pl.ds(o, tm), :]
    out = _conv_tail(rows_at, wdw_ref, bdw_ref, lg_ref, lb_ref, w2_ref, b2_ref)
    o_ref[0] = x_ref[0] + gate_ref[0] * out


def conv_pw2_residual_small(full, wdw, bdw, lg, lb, w2, b2, x, gate):
    bsz, ts, d = x.shape
    seq = lambda a: pl.BlockSpec((1,) + a.shape[1:], lambda i: (i, 0, 0))
    return pl.pallas_call(
        _conv_small_kernel,
        grid=(bsz,),
        in_specs=[seq(full), _full_spec(wdw), _full_spec(bdw), _full_spec(lg), _full_spec(lb),
                  _full_spec(w2), _full_spec(b2), seq(x), seq(gate)],
        out_specs=seq(x),
        out_shape=jax.ShapeDtypeStruct(x.shape, F32),
        compiler_params=_cparams("parallel"),
        name="conv_pw2_residual_small",
    )(*map(_arr, (full, wdw, bdw, lg, lb, w2, b2, x, gate)))


def _ffn_kernel(x_ref, g_ref, sh_ref, sc_ref, gate_ref, wg_ref, wu_ref, wd_ref, *rest, final):
    x = x_ref[...]
    h = _norm_mod(x, g_ref[...], sh_ref[...], sc_ref[...]).astype(BF16)
    a = _dot(h, wg_ref[...])
    b = _dot(h, wu_ref[...])
    act = (_silu(a) * b).astype(BF16)
    xn = x + gate_ref[...] * _dot(act, wd_ref[...])
    if final:
        fg_ref, o_ref = rest
        ms = jnp.mean(xn * xn, axis=-1, keepdims=True)
        o_ref[...] = xn * lax.rsqrt(ms + EPS) * fg_ref[...]
    else:
        (o_ref,) = rest
        o_ref[...] = xn


def ffn(x, g, shift, scale, gate, wg, wu, wd, tm, tpg, final_g=None):
    m, d = x.shape
    final = final_g is not None
    ins = [x, g, shift, scale, gate, wg, wu, wd] + ([final_g] if final else [])
    specs = [_row_spec(tm, d), _full_spec(g), _mod_spec(shift, tpg), _mod_spec(scale, tpg),
             _mod_spec(gate, tpg), _full_spec(wg), _full_spec(wu), _full_spec(wd)]
    if final:
        specs.append(_full_spec(final_g))
    return pl.pallas_call(
        functools.partial(_ffn_kernel, final=final),
        grid=(m // tm,),
        in_specs=specs,
        out_specs=_row_spec(tm, d),
        out_shape=jax.ShapeDtypeStruct((m, d), F32),
        compiler_params=_cparams("parallel"),
        name="ffn_final" if final else "ffn",
    )(*map(_arr, ins))


def _kvproj_kernel(x_ref, g_ref, w_ref, o_ref):
    x = x_ref[...]
    ms = jnp.mean(x * x, axis=-1, keepdims=True)
    h = (x * lax.rsqrt(ms + EPS) * g_ref[...]).astype(BF16)
    o_ref[...] = _dot(h, w_ref[...])


def kv_proj(x, g, w, tm):
    m, d = x.shape
    n = w.shape[1]
    return pl.pallas_call(
        _kvproj_kernel,
        grid=(m // tm,),
        in_specs=[_row_spec(tm, d), _full_spec(g), _full_spec(w)],
        out_specs=_row_spec(tm, n),
        out_shape=jax.ShapeDtypeStruct((m, n), F32),
        compiler_params=_cparams("parallel"),
        name="kv_proj",
    )(x, g, w)


def _qg_kernel(x_ref, g_ref, sh_ref, sc_ref, w_ref, q_ref, gates_ref):
    nq = q_ref.shape[1]
    h = _norm_mod(x_ref[...], g_ref[...], sh_ref[...], sc_ref[...]).astype(BF16)
    z = _dot(h, w_ref[...])
    q_ref[...] = (z[:, :nq] * (HEAD_DIM ** -0.5)).astype(BF16)
    gates_ref[...] = jax.nn.sigmoid(z[:, nq:])


def qg_proj(x, g, shift, scale, w, tm, tpg):
    m, d = x.shape
    nq = N_HEADS * HEAD_DIM
    ng = w.shape[1] - nq
    return pl.pallas_call(
        _qg_kernel,
        grid=(m // tm,),
        in_specs=[_row_spec(tm, d), _full_spec(g), _mod_spec(shift, tpg), _mod_spec(scale, tpg),
                  _full_spec(w)],
        out_specs=[_row_spec(tm, nq), _row_spec(tm, ng)],
        out_shape=[jax.ShapeDtypeStruct((m, nq), BF16), jax.ShapeDtypeStruct((m, ng), F32)],
        compiler_params=_cparams("parallel"),
        name="qg_proj",
    )(*map(_arr, (x, g, shift, scale, w)))


def _oproj_kernel(o_ref, w_ref, x_ref, gate_ref, out_ref):
    out_ref[...] = x_ref[...] + gate_ref[...] * _dot(o_ref[...], w_ref[...])


def o_proj_residual(o, w, x, gate, tm, tpg):
    m, d = x.shape
    return pl.pallas_call(
        _oproj_kernel,
        grid=(m // tm,),
        in_specs=[_row_spec(tm, o.shape[1]), _full_spec(w), _row_spec(tm, d), _mod_spec(gate, tpg)],
        out_specs=_row_spec(tm, d),
        out_shape=jax.ShapeDtypeStruct((m, d), F32),
        compiler_params=_cparams("parallel"),
        name="o_proj_residual",
    )(*map(_arr, (o, w, x, gate)))


SLAB = 128
HEADS_PER_SLAB = SLAB // HEAD_DIM


R_PER_DOT = 2
CMP_PITCH = 40


def _compress_core(rows_at, pe_row, w1_at, b1, w2, b2):
    hidden = w2.shape[0]
    acc = None
    for i in range(CMP_BLOCK // R_PER_DOT):
        parts = [(rows_at(R_PER_DOT * i + j) + pe_row(R_PER_DOT * i + j)).astype(BF16) for j in range(R_PER_DOT)]
        term = _dot(jnp.concatenate(parts, axis=1), w1_at(i))
        acc = term if acc is None else acc + term
    hid = _silu(acc + b1)
    return [_dot(hid[:, hh * hidden:(hh + 1) * hidden].astype(BF16), w2) + b2 for hh in range(HEADS_PER_SLAB)]


def _compress_kernel(x_ref, pe_ref, w1_ref, b1_ref, w2_ref, b2_ref, o_ref, *, nblk):
    outs = _compress_core(lambda r: x_ref[0, pl.ds(r, nblk, stride=CMP_BLOCK), :],
                          lambda r: pe_ref[0, r:r + 1, :], lambda i: w1_ref[0, i],
                          b1_ref[0], w2_ref[0], b2_ref[0])
    for hh, out in enumerate(outs):
        o_ref[0, 0, hh] = out


def compress(x, nblk, pe_slab, w1_bd, b1_slab, w2, b2):
    bc, rows, _ = x.shape
    n_slab = 2 * KV_HEADS // HEADS_PER_SLAB
    slabs_per_e = n_slab // 2
    per_e = lambda a: pl.BlockSpec((1,) + a.shape[1:], lambda b, s: (s // slabs_per_e,) + (0,) * (a.ndim - 1))
    return pl.pallas_call(
        functools.partial(_compress_kernel, nblk=nblk),
        grid=(bc, n_slab),
        in_specs=[pl.BlockSpec((1, rows, SLAB), lambda b, s: (b, 0, s)),
                  pl.BlockSpec((1, CMP_BLOCK, SLAB), lambda b, s: (s, 0, 0)),
                  per_e(w1_bd), per_e(b1_slab), per_e(w2), per_e(b2)],
        out_specs=pl.BlockSpec((1, 1, HEADS_PER_SLAB, nblk, HEAD_DIM),
                               lambda b, s: (b, s // slabs_per_e, s % slabs_per_e, 0, 0)),
        out_shape=jax.ShapeDtypeStruct((bc, 2, KV_HEADS, nblk, HEAD_DIM), F32),
        compiler_params=_cparams("parallel", "parallel"),
        name="compress",
    )(x, pe_slab, w1_bd, b1_slab, w2, b2)


def _compress_paged_kernel(pt_ref, *refs, n_pages):
    page_refs = refs[:n_pages]
    pe_ref, w1_ref, b1_ref, w2_ref, b2_ref, o_ref, scr = refs[n_pages:]
    blocks_per_page = page_refs[0].shape[1] // CMP_BLOCK
    nblk = n_pages * blocks_per_page
    n_slab = pe_ref.shape[0]
    slabs_per_e = n_slab // 2
    for s in range(n_slab):
        e = s // slabs_per_e
        for p in range(n_pages):
            for j in range(blocks_per_page):
                base = (p * blocks_per_page + j) * CMP_PITCH
                scr[base:base + CMP_BLOCK, :] = page_refs[p][0, j * CMP_BLOCK:(j + 1) * CMP_BLOCK,
                                                             s * SLAB:(s + 1) * SLAB]
        outs = _compress_core(lambda r: scr[pl.ds(r, nblk, stride=CMP_PITCH), :],
                              lambda r: pe_ref[s, r:r + 1, :], lambda i: w1_ref[e, i],
                              b1_ref[e], w2_ref[e], b2_ref[e])
        for hh, out in enumerate(outs):
            o_ref[0, e, (s % slabs_per_e) * HEADS_PER_SLAB + hh] = out


def compress_paged(cache, page_table, pe_slab, w1_bd, b1_slab, w2, b2):
    n_pool, page, c = cache.shape
    db, n_pages = page_table.shape
    assert page % CMP_BLOCK == 0
    nblk = n_pages * page // CMP_BLOCK
    const = lambda a: pl.BlockSpec(a.shape, lambda b, pt: (0,) * a.ndim)

    def page_spec(k):
        return pl.BlockSpec((1, page, c), lambda b, pt: (pt[b, k], 0, 0))

    return pl.pallas_call(
        functools.partial(_compress_paged_kernel, n_pages=n_pages),
        grid_spec=pltpu.PrefetchScalarGridSpec(
            num_scalar_prefetch=1,
            grid=(db,),
            in_specs=[page_spec(k) for k in range(n_pages)]
            + [const(pe_slab), const(w1_bd), const(b1_slab), const(w2), const(b2)],
            out_specs=pl.BlockSpec((1, 2, KV_HEADS, nblk, HEAD_DIM), lambda b, pt: (b, 0, 0, 0, 0)),
            scratch_shapes=[pltpu.VMEM((nblk * CMP_PITCH, SLAB), F32)],
        ),
        out_shape=jax.ShapeDtypeStruct((db, 2, KV_HEADS, nblk, HEAD_DIM), F32),
        compiler_params=_cparams("parallel"),
        name="compress_paged",
    )(page_table, *([cache] * n_pages), pe_slab, w1_bd, b1_slab, w2, b2)


def _rel_bucket(dist):
    n = jnp.maximum(dist, 0)
    exact = N_BUCKETS // 2
    nf = jnp.maximum(n, 1).astype(F32)
    large = exact + (jnp.log(nf / exact) / math.log(MAX_DISTANCE / exact) * (N_BUCKETS - exact)).astype(jnp.int32)
    large = jnp.minimum(large, N_BUCKETS - 1)
    return jnp.where(n < exact, n, large)


class _BiasTable:
    def __init__(self, table, max_dist, pad):
        tab = table.astype(F32)

        def lookup(dist):
            bucket = _rel_bucket(dist)
            out = jnp.zeros((tab.shape[1], dist.shape[0]), F32)
            for k in range(N_BUCKETS):
                out = jnp.where(bucket[None, :] == k, tab[k][:, None], out)
            return out

        self.pad = pad
        self.n = max_dist
        self.ext = lookup(jnp.arange(-pad, max_dist))
        self.rev = lookup(max_dist - 1 - jnp.arange(pad + max_dist))

    def at(self, d):
        return self.ext[:, self.pad + d]

    def rising(self, d0, length):
        return lax.slice_in_dim(self.ext, self.pad + d0, self.pad + d0 + length, axis=1)

    def falling(self, d0, length, step=1):
        start = self.n - 1 - d0
        return lax.slice(self.rev, (0, start), (self.rev.shape[0], start + step * (length - 1) + 1), (1, step))

    def frames(self, d0, count, hop, length):
        reps = length // hop
        seq = self.rising(d0, hop * (count + reps - 1)).reshape(-1, count + reps - 1, hop)
        return jnp.concatenate([seq[:, a:a + count] for a in range(reps)], axis=-1)

    def toeplitz(self, d0, count, hop, size):
        assert hop == size
        period = 2 * size - 1
        base = self.rising(d0 - (size - 1), hop * (count + 1)).reshape(-1, count + 1, hop)
        w = jnp.concatenate([base[:, :count, size - 1:], base[:, 1:, :size - 1],
                             base[:, :count, :size - 1]], axis=-1)
        rows = jnp.tile(w, (1, 1, size))[:, :, :size * (period - 1)]
        return rows.reshape(w.shape[0], count, size, period - 1)[..., :size]


def _softmax_step(s, vt, m, l, acc):
    m_new = jnp.maximum(m, jnp.max(s, axis=0, keepdims=True))
    alpha = jnp.exp(m - m_new)
    p = jnp.exp(s - m_new)
    l = alpha * l + jnp.sum(p, axis=0, keepdims=True)
    acc = alpha * acc + _dot(vt, p.astype(BF16))
    return m_new, l, acc


def _attn_kernel(bfar_ref, q_ref, gt_ref, kaug_ref, vst_ref, kw_ref, vwt_ref, ck_ref, cvt_ref,
                 bc_ref, bn_ref, o_ref, imp_ref, sc_ref, s_ref, *, nb, nc):
    g = pl.program_id(0)
    qt = pl.program_id(2)
    qs = qt * QT
    gq = GROUP * QT

    lane_cat = lambda parts: jnp.concatenate(parts, axis=1)
    q_t = q_ref[0].astype(F32).T.astype(BF16)
    q_heads = [q_t[h * HEAD_DIM:(h + 1) * HEAD_DIM] for h in range(GROUP)]
    q_plain = lane_cat(q_heads)

    col_i = lax.broadcasted_iota(jnp.int32, (1, gq), 1) % QT

    s_c = _dot(ck_ref[0, 0], q_plain) + lane_cat([bc_ref[h, 0] for h in range(GROUP)])
    n_idx = lax.broadcasted_iota(jnp.int32, (nc, gq), 0)
    valid_c = (n_idx * CMP_BLOCK + CMP_BLOCK - 1) <= (qs + col_i)
    s_c = jnp.where(valid_c, s_c, M_INIT)
    m_c = jnp.max(s_c, axis=0, keepdims=True)
    p_c = jnp.where(valid_c, jnp.exp(s_c - m_c), 0.0)
    l_c = jnp.sum(p_c, axis=0, keepdims=True)
    p_c = p_c * jnp.where(l_c > 0.0, 1.0 / l_c, 0.0)
    o_c = _dot(cvt_ref[0, 0], p_c.astype(BF16))

    imp = p_c[:, 0:QT]
    for h in range(1, GROUP):
        imp = imp + p_c[:, h * QT:(h + 1) * QT]
    imp_ref[...] = imp
    ratio = SEL_BLOCK // CMP_BLOCK
    imp2 = imp_ref[pl.ds(0, nb, stride=ratio), :]
    for r in range(1, ratio):
        imp2 = imp2 + imp_ref[pl.ds(r, nb, stride=ratio), :]
    j_idx = lax.broadcasted_iota(jnp.int32, (nb, QT), 0)
    i_idx = lax.broadcasted_iota(jnp.int32, (nb, QT), 1)
    jq = (qs + i_idx) // SEL_BLOCK
    forced = (j_idx == 0) | (j_idx == jq) | (j_idx == jq - 1)
    score = jnp.where(j_idx <= jq, jnp.where(forced, FORCE, imp2), -1.0)
    sc_ref[...] = score

    sub8 = lax.broadcasted_iota(jnp.int32, (8, QT), 0)
    groups = [score[8 * r:8 * r + 8] for r in range(nb // 8)]
    counts = [jnp.zeros((8, QT), F32) for _ in groups]
    for c in range(nb):
        row = sc_ref[c:c + 1, :]
        for r, grp_score in enumerate(groups):
            if r > c // 8:
                beats = row >= grp_score
            elif r < c // 8:
                beats = row > grp_score
            else:
                tie = jnp.where(row == grp_score, 1.0, 0.0) * jnp.where(sub8 > c % 8, 1.0, 0.0)
                counts[r] = counts[r] + tie
                beats = row > grp_score
            counts[r] = counts[r] + jnp.where(beats, 1.0, 0.0)
    cnt = jnp.concatenate(counts, axis=0)
    selected = (cnt < float(min(N_SEL, nb))) & (score >= 0.0)
    mask_feat = jnp.where(selected, 0.0, MASK_NEG).astype(BF16)
    q_aug = lane_cat([jnp.concatenate([qh, mask_feat], axis=0) for qh in q_heads])

    far_row = lane_cat([jnp.full((1, QT), bfar_ref[g * GROUP + h], F32) for h in range(GROUP)])

    def near_bias(kt0, n_sub):
        parts = []
        for sub in range(n_sub):
            dm = jnp.clip(qt - (kt0 + sub), 0, NEAR_SPAN - 1)
            parts.append(lane_cat([bn_ref[h, dm] for h in range(GROUP)]))
        return jnp.concatenate(parts, axis=0)

    init = (jnp.full((1, gq), M_INIT, F32), jnp.zeros((1, gq), F32), jnp.zeros((HEAD_DIM, gq), F32))
    sub_per_chunk = KEY_STEP // QT
    row_k = lax.broadcasted_iota(jnp.int32, (KEY_STEP, gq), 0)

    last_chunk = qt // sub_per_chunk

    def raw_scores(c):
        ks = pl.multiple_of(jnp.minimum(c, last_chunk) * KEY_STEP, KEY_STEP)
        return _dot(kaug_ref[0, 0, pl.ds(ks, KEY_STEP), :], q_aug)

    def pipelined(finish):
        def body(c, carry):
            cur = s_ref[c % 2]
            s_ref[(c + 1) % 2] = raw_scores(c + 1)
            ks = pl.multiple_of(c * KEY_STEP, KEY_STEP)
            return _softmax_step(finish(c, cur, ks), vst_ref[0, 0, :, pl.ds(ks, KEY_STEP)], *carry)
        return body

    def far_finish(c, s, ks):
        return s + far_row

    def near_finish(c, s, ks):
        return s + near_bias(c * sub_per_chunk, sub_per_chunk)

    s_ref[0] = raw_scores(0)
    n_far = jnp.maximum(qt - (NEAR_TILES - 1), 0) // sub_per_chunk
    carry = lax.fori_loop(0, n_far, pipelined(far_finish), init)
    carry = lax.fori_loop(n_far, last_chunk, pipelined(near_finish), carry)
    ks = pl.multiple_of(last_chunk * KEY_STEP, KEY_STEP)
    s = near_finish(last_chunk, s_ref[last_chunk % 2], ks)
    s = jnp.where(ks + row_k <= qs + col_i, s, MASK_NEG)
    m_s, l_s, acc_s = _softmax_step(s, vst_ref[0, 0, :, pl.ds(ks, KEY_STEP)], *carry)
    o_s = acc_s / l_s

    wt0 = jnp.maximum(qt - WINDOW // QT, 0)
    ws = pl.multiple_of(wt0 * QT, QT)
    wk = WINDOW + QT
    s_w = _dot(kw_ref[0, 0, pl.ds(ws, wk), :], q_plain) + near_bias(wt0, wk // QT)
    dist = (qs + col_i) - (ws + lax.broadcasted_iota(jnp.int32, (wk, gq), 0))
    s_w = jnp.where((dist >= 0) & (dist < WINDOW), s_w, MASK_NEG)
    p_w = jnp.exp(s_w - jnp.max(s_w, axis=0, keepdims=True))
    l_w = jnp.sum(p_w, axis=0, keepdims=True)
    o_w = _dot(vwt_ref[0, 0, :, pl.ds(ws, wk)], p_w.astype(BF16)) / l_w

    gate = lambda br: lane_cat([gt_ref[0, 0, h * 3 + br:h * 3 + br + 1, :] for h in range(GROUP)])
    o_t = gate(0) * o_c + gate(1) * o_s + gate(2) * o_w
    o_rows = jnp.concatenate([o_t[:, h * QT:(h + 1) * QT] for h in range(GROUP)], axis=0)
    o_ref[0] = o_rows.T.astype(BF16)


def prompt_attention(q, gates_t, kaug, vst, kwin, vwt, ck, cvt, bias_cmp, bias_near, bias_far):
    bsz, t, dq = q.shape
    nb = t // SEL_BLOCK
    nc = t // CMP_BLOCK
    nq = t // QT
    gd = GROUP * HEAD_DIM
    per_bg = lambda a: pl.BlockSpec((1, 1) + a.shape[2:], lambda g, b, i, *_: (b, g, 0, 0))
    grid_spec = pltpu.PrefetchScalarGridSpec(
        num_scalar_prefetch=1,
        grid=(KV_HEADS, bsz, nq),
        in_specs=[pl.BlockSpec((1, QT, gd), lambda g, b, i, *_: (b, i, g)),
                  pl.BlockSpec((1, 1, 3 * GROUP, QT), lambda g, b, i, *_: (b, g, 0, i)),
                  per_bg(kaug), per_bg(vst), per_bg(kwin), per_bg(vwt), per_bg(ck), per_bg(cvt),
                  pl.BlockSpec((GROUP, 1, nc, QT), lambda g, b, i, *_: (g, i, 0, 0)),
                  pl.BlockSpec((GROUP, NEAR_SPAN, QT, QT), lambda g, b, i, *_: (g, 0, 0, 0))],
        out_specs=pl.BlockSpec((1, QT, gd), lambda g, b, i, *_: (b, i, g)),
        scratch_shapes=[pltpu.VMEM((nc, QT), F32), pltpu.VMEM((nb, QT), F32),
                        pltpu.VMEM((2, KEY_STEP, GROUP * QT), F32)],
    )
    return pl.pallas_call(
        functools.partial(_attn_kernel, nb=nb, nc=nc),
        grid_spec=grid_spec,
        out_shape=jax.ShapeDtypeStruct((bsz, t, dq), BF16),
        compiler_params=_cparams("parallel", "parallel", "parallel"),
        name="prompt_attention",
    )(bias_far, q, gates_t, kaug, vst, kwin, vwt, ck, cvt, bias_cmp, bias_near)


KEY_CHUNK = 1024
NEW_PAD = 128


def _diag_heads(full, row_g):
    out = None
    for g in range(KV_HEADS):
        part = jnp.where(row_g == g, full[:, g * HEAD_DIM:(g + 1) * HEAD_DIM], 0.0)
        out = part if out is None else out + part
    return out


def _sample_attn_kernel(pt_ref, q_ref, gates_ref, *refs, past, ts, n_pages):
    page_refs = refs[:n_pages]
    (nsel_ref, kwin_ref, nwin_ref, ck_ref, cv_ref, bsel_ref, bwin_ref, bcmp_ref, e_ref, o_ref,
     s_scr) = refs[n_pages:]
    kvd = KV_HEADS * HEAD_DIM
    page = page_refs[0].shape[1]
    pages_per_chunk = KEY_CHUNK // page

    def cached(c, lo, hi):
        parts = [page_refs[c * pages_per_chunk + j][0, :, lo:hi] for j in range(pages_per_chunk)]
        return jnp.concatenate(parts, axis=0).astype(BF16)

    rows = GROUP * KV_HEADS * ts
    nc = ck_ref.shape[1]
    q = q_ref[0]
    r_idx = lax.broadcasted_iota(jnp.int32, (rows, 1), 0)
    row_q = r_idx % ts
    row_g = (r_idx // ts) % KV_HEADS

    s_c = _dot_nt(q, ck_ref[0]) + bcmp_ref[...]
    p_c = jnp.exp(s_c - jnp.max(s_c, axis=1, keepdims=True))
    p_c = p_c / jnp.sum(p_c, axis=1, keepdims=True)
    o_c = _diag_heads(_dot(p_c.astype(BF16), cv_ref[0]), row_g)

    per = KV_HEADS * ts
    imp = p_c[0:per]
    for h in range(1, GROUP):
        imp = imp + p_c[h * per:(h + 1) * per]
    ratio = SEL_BLOCK // CMP_BLOCK
    imp2 = imp
    for r in range(1, ratio):
        imp2 = imp2 + pltpu.roll(imp, nc - r, axis=1)
    lane = lax.broadcasted_iota(jnp.int32, (per, nc), 1)
    blk = lane // ratio
    jq = past // SEL_BLOCK
    forced = (blk == 0) | (blk == jq - 1)
    score = jnp.where(lane % ratio == 0, jnp.where(forced, FORCE, imp2), -2.0)
    cnt = jnp.where(score < FORCE, 1.0, 0.0)
    for c in range(nc // ratio):
        col = score[:, c * ratio:c * ratio + 1]
        ge = jnp.where(col >= score, 1.0, 0.0)
        gt = jnp.where(col > score, 1.0, 0.0)
        cnt = cnt + jnp.where(lane > c * ratio, ge, gt)
    selected = (cnt < float(min(N_SEL, jq + 1))) & (score >= 0.0)
    mask_feat = jnp.where(selected, 0.0, MASK_NEG).astype(BF16)
    mask_feat = jnp.concatenate([mask_feat] * GROUP, axis=0)

    for c in range(past // KEY_CHUNK):
        sl = slice(c * KEY_CHUNK, (c + 1) * KEY_CHUNK)
        s_scr[:, sl] = _dot_nt(q, cached(c, 0, kvd)) + bsel_ref[:, sl] + _dot(mask_feat, e_ref[:, sl])
    new_j = lax.broadcasted_iota(jnp.int32, (rows, NEW_PAD), 1)
    new_ok = new_j <= row_q
    s_new = _dot_nt(q, nsel_ref[0, :, 0:kvd].astype(BF16)) + bsel_ref[:, past:past + NEW_PAD]
    s_scr[:, past:past + NEW_PAD] = jnp.where(new_ok, s_new, MASK_NEG)
    s_all = s_scr[...]
    p_s = jnp.exp(s_all - jnp.max(s_all, axis=1, keepdims=True))
    l_s = jnp.sum(p_s, axis=1, keepdims=True)
    s_scr[...] = p_s
    acc = _dot(s_scr[:, past:past + NEW_PAD].astype(BF16), nsel_ref[0, :, kvd:2 * kvd].astype(BF16))
    for c in range(past // KEY_CHUNK):
        sl = slice(c * KEY_CHUNK, (c + 1) * KEY_CHUNK)
        acc = acc + _dot(s_scr[:, sl].astype(BF16), cached(c, kvd, 2 * kvd))
    o_s = _diag_heads(acc, row_g) / l_s

    wlen = kwin_ref.shape[1]
    win_i = lax.broadcasted_iota(jnp.int32, (rows, wlen), 1)
    s_w1 = _dot_nt(q, kwin_ref[0, :, 0:kvd].astype(BF16)) + bwin_ref[:, 0:wlen]
    s_w1 = jnp.where(win_i + (WINDOW - wlen) > row_q, s_w1, MASK_NEG)
    s_w2 = _dot_nt(q, nwin_ref[0, :, 0:kvd].astype(BF16)) + bwin_ref[:, wlen:wlen + NEW_PAD]
    s_w2 = jnp.where(new_ok, s_w2, MASK_NEG)
    m_w = jnp.maximum(jnp.max(s_w1, axis=1, keepdims=True), jnp.max(s_w2, axis=1, keepdims=True))
    p_w1 = jnp.exp(s_w1 - m_w)
    p_w2 = jnp.exp(s_w2 - m_w)
    l_w = jnp.sum(p_w1, axis=1, keepdims=True) + jnp.sum(p_w2, axis=1, keepdims=True)
    acc_w = (_dot(p_w1.astype(BF16), kwin_ref[0, :, kvd:2 * kvd].astype(BF16))
             + _dot(p_w2.astype(BF16), nwin_ref[0, :, kvd:2 * kvd].astype(BF16)))
    o_w = _diag_heads(acc_w, row_g) / l_w

    gts = gates_ref[0]
    o_ref[0] = gts[:, 0:1] * o_c + gts[:, 1:2] * o_s + gts[:, 2:3] * o_w


def sample_attention(page_table, q_bd, gates, cache, nsel, kwin, nwin, ck, cv, bsel, bwin, bcmp, e_mat, ts):
    db, rows, kvd = q_bd.shape
    n_pages = page_table.shape[1]
    page = cache.shape[1]
    past = n_pages * page
    assert KEY_CHUNK % page == 0 and past % KEY_CHUNK == 0
    seq = lambda a: pl.BlockSpec((1,) + a.shape[1:], lambda b, pt: (b,) + (0,) * (a.ndim - 1))
    const = lambda a: pl.BlockSpec(a.shape, lambda b, pt: (0,) * a.ndim)

    def page_spec(k):
        return pl.BlockSpec((1,) + cache.shape[1:], lambda b, pt: (pt[b, k], 0, 0))

    return pl.pallas_call(
        functools.partial(_sample_attn_kernel, past=past, ts=ts, n_pages=n_pages),
        grid_spec=pltpu.PrefetchScalarGridSpec(
            num_scalar_prefetch=1,
            grid=(db,),
            in_specs=[seq(q_bd), seq(gates)] + [page_spec(k) for k in range(n_pages)]
            + [seq(nsel), seq(kwin), seq(nwin), seq(ck), seq(cv), const(bsel), const(bwin), const(bcmp),
               const(e_mat)],
            out_specs=pl.BlockSpec((1, rows, HEAD_DIM), lambda b, pt: (b, 0, 0)),
            scratch_shapes=[pltpu.VMEM((rows, past + NEW_PAD), F32)],
        ),
        out_shape=jax.ShapeDtypeStruct((db, rows, HEAD_DIM), F32),
        compiler_params=_cparams("parallel"),
        name="sample_attention",
    )(page_table, q_bd, gates, *([cache] * n_pages), nsel, kwin, nwin, ck, cv, bsel, bwin, bcmp, e_mat)


def _prep_weights(P):
    W = {}
    cast = lambda a: a.astype(BF16)
    for k in ("w_pw1", "w_pw2", "w_kv", "w_o", "w_gate", "w_up", "w_down", "w_phi1", "w_phi2"):
        W[k] = cast(P[k])
    nq = N_HEADS * HEAD_DIM
    wqg = P["w_qg"]
    pad = 128 - (wqg.shape[2] - nq)
    W["w_qg"] = cast(jnp.pad(wqg, ((0, 0), (0, 0), (0, pad))))
    rows = lambda a: a.reshape(-1, 1, a.shape[-1])
    for k in ("norm_g", "b_pw1", "b_dw", "ln_g", "ln_b", "b_pw2"):
        W[k] = rows(P[k])
    W["w_dw"] = jnp.pad(P["w_dw"], ((0, 0), (0, CONV_PAD - CONV_W), (0, 0)))
    return W


def _trunk(x, mods, is_prompt, caches, P, W, bias_dist):
    bsz, t, d = x.shape
    m = bsz * t
    depth = P["w_gate"].shape[0]
    n_a = P["w_pw1"].shape[0]
    if is_prompt:
        tm, tpg = 256, t // 256
    else:
        tm, tpg = m, 1
    x2 = x.reshape(m, d)
    row2 = lambda a: a.reshape(1, -1)
    conv_states = []
    y = None
    for l in range(depth):
        shift, scale, gate = mods[(l, 0)][:3]
        g0 = _layer(W["norm_g"], 2 * l)
        if l < n_a:
            u = pw1_glu(x2, g0, shift, scale, _layer(W["w_pw1"], l), _layer(W["b_pw1"], l), tm, tpg)
            u3 = u.reshape(bsz, t, d)
            args = tuple(_layer(W[k], l) for k in ("w_dw", "b_dw", "ln_g", "ln_b", "w_pw2", "b_pw2"))
            if is_prompt:
                conv_states.append(u3[:, -(CONV_W - 1):])
                x2 = conv_pw2_residual(u3, *args, x2, gate, tm, tpg)
            else:
                hist = jnp.pad(caches["state_conv"][l], ((0, 0), (CONV_PAD - CONV_W + 1, 0), (0, 0)))
                full = jnp.concatenate([hist, u3], axis=1)
                conv_states.append(full[:, -(CONV_W - 1):])
                x2 = conv_pw2_residual_small(full, *args, x2.reshape(bsz, t, d),
                                             mods[(l, 0)][3]).reshape(m, d)
        else:
            if l == n_a:
                rows = kv_proj(x2, row2(P["g_kv"]), W["w_kv"], tm)
                rows6 = rows.reshape(bsz, t, 3, 2, KV_HEADS, HEAD_DIM)
                if is_prompt:
                    att = _prompt_attention_setup(rows6, P, W, bias_dist)
                    win_state = rows6[:, -min(WINDOW, t):, 2]
                else:
                    att = _sample_attention_setup(rows6, caches, P, W, bias_dist)
                    wl = caches["cache_kv_win"].shape[1]
                    win_state = jnp.concatenate([caches["cache_kv_win"], rows6[:, :, 2]], axis=1)[:, -wl:]
            lb = l - n_a
            q, gates = qg_proj(x2, g0, shift, scale, _layer(W["w_qg"], lb), tm, tpg)
            if is_prompt:
                o = _prompt_attention_layer(q.reshape(bsz, t, -1), gates.reshape(bsz, t, -1), att)
            else:
                o = _sample_attention_layer(q.reshape(bsz, t, -1), gates.reshape(bsz, t, -1), att)
            x2 = o_proj_residual(o.reshape(m, -1), _layer(W["w_o"], lb), x2, gate, tm, tpg)
        shift, scale, gate = mods[(l, 1)][:3]
        fg = row2(P["final_g"]) if l == depth - 1 else None
        out = ffn(x2, _layer(W["norm_g"], 2 * l + 1), shift, scale, gate, _layer(W["w_gate"], l),
                  _layer(W["w_up"], l), _layer(W["w_down"], l), tm, tpg, final_g=fg)
        if l == depth - 1:
            y = out
        else:
            x2 = out
    return (y.reshape(bsz, t, d), rows6[:, :, 0], rows6[:, :, 1], win_state, jnp.stack(conv_states))


def _compress_weights(P, W):
    pe = P["pe_cmp"]
    pe_slab = jnp.repeat(jnp.tile(pe, (1, 1, HEADS_PER_SLAB)), KV_HEADS // HEADS_PER_SLAB, axis=0)
    w1 = W["w_phi1"].reshape(2, CMP_BLOCK, HEAD_DIM, -1)
    zero = jnp.zeros_like(w1)
    w1_bd = jnp.concatenate([jnp.concatenate([w1, zero], axis=-1), jnp.concatenate([zero, w1], axis=-1)],
                            axis=2)
    w1_bd = w1_bd.reshape(2, CMP_BLOCK // R_PER_DOT, R_PER_DOT * SLAB, -1)
    b1_slab = jnp.tile(P["b_phi1"], (1, HEADS_PER_SLAB))[:, None, :]
    return pe_slab, w1_bd, b1_slab, W["w_phi2"], P["b_phi2"][:, None, :]


def _prompt_attention_setup(rows6, P, W, bias_dist):
    bsz, t = rows6.shape[:2]
    nb = t // SEL_BLOCK
    nc = t // CMP_BLOCK
    nq = t // QT
    cmp = compress(rows6.reshape(bsz, t, -1), nc, *_compress_weights(P, W))
    ck = cmp[:, 0].astype(BF16)
    cvt = cmp[:, 1].transpose(0, 1, 3, 2).astype(BF16)
    onehot = (jnp.arange(t)[:, None] // SEL_BLOCK == jnp.arange(nb)[None, :]).astype(BF16)
    ksel = rows6[:, :, 1, 0].transpose(0, 2, 1, 3).astype(BF16)
    kaug = jnp.concatenate([ksel, jnp.broadcast_to(onehot, (bsz, KV_HEADS, t, nb))], axis=-1)
    vst = rows6[:, :, 1, 1].transpose(0, 2, 3, 1).astype(BF16)
    kwin = rows6[:, :, 2, 0].transpose(0, 2, 1, 3).astype(BF16)
    vwt = rows6[:, :, 2, 1].transpose(0, 2, 3, 1).astype(BF16)
    per_qt = QT // CMP_BLOCK
    n_u = nc + per_qt * (nq - 1)
    frames = bias_dist.frames(-(CMP_BLOCK - 1) - CMP_BLOCK * (nc - 1), n_u, CMP_BLOCK, QT)[:, ::-1]
    bias_cmp = jnp.stack([frames[:, per_qt * (nq - 1 - i):per_qt * (nq - 1 - i) + nc] for i in range(nq)],
                         axis=1)
    bias_near = bias_dist.toeplitz(0, NEAR_SPAN, QT, QT)
    bias_far = bias_dist.at(NEAR_TILES * QT)
    return dict(kaug=kaug, vst=vst, kwin=kwin, vwt=vwt, ck=ck, cvt=cvt, bias_cmp=bias_cmp,
                bias_near=bias_near, bias_far=bias_far)


def _prompt_attention_layer(q, gates, att):
    bsz, t, _ = q.shape
    gates_t = gates[:, :, :3 * N_HEADS].reshape(bsz, t, KV_HEADS, 3 * GROUP).transpose(0, 2, 3, 1)
    return prompt_attention(q, gates_t, att["kaug"], att["vst"], att["kwin"], att["vwt"], att["ck"],
                            att["cvt"], att["bias_cmp"], att["bias_near"], att["bias_far"])


def _sample_attention_setup(rows6, caches, P, W, bias_dist):
    db, ts = rows6.shape[:2]
    pt = caches["page_table"]
    n_pages = pt.shape[1]
    page = caches["cache_kv_cmp"].shape[1]
    past = n_pages * page
    c = 2 * KV_HEADS * HEAD_DIM
    sel_cache = caches["cache_kv_sel"].reshape(-1, page, c)
    nc = past // CMP_BLOCK
    assert (past + ts) // CMP_BLOCK == nc and past % KEY_CHUNK == 0
    cmp = compress_paged(caches["cache_kv_cmp"].reshape(-1, page, c), pt, *_compress_weights(P, W))
    ck = cmp[:, 0].transpose(0, 2, 1, 3).reshape(db, nc, KV_HEADS * HEAD_DIM).astype(BF16)
    cv = cmp[:, 1].transpose(0, 2, 1, 3).reshape(db, nc, KV_HEADS * HEAD_DIM).astype(BF16)
    pad_new = lambda a: jnp.pad(a.reshape(db, ts, c), ((0, 0), (0, NEW_PAD - ts), (0, 0)))
    nsel = pad_new(rows6[:, :, 1])
    nwin = pad_new(rows6[:, :, 2])
    kwin = caches["cache_kv_win"].reshape(db, -1, c)
    wlen = kwin.shape[1]
    rows = GROUP * KV_HEADS * ts
    def by_row(f):
        per_q = jnp.stack([f(q) for q in range(ts)], axis=0)
        per_q = per_q.reshape(ts, KV_HEADS, GROUP, -1).transpose(2, 1, 0, 3)
        return per_q.reshape(rows, -1)

    bsel = by_row(lambda q: bias_dist.falling(past + q, past + NEW_PAD))
    bwin = by_row(lambda q: jnp.concatenate([bias_dist.falling(wlen + q, wlen),
                                             bias_dist.falling(q, NEW_PAD)], axis=1))
    bcmp = by_row(lambda q: bias_dist.falling(past + q - (CMP_BLOCK - 1), nc, step=CMP_BLOCK))
    ratio = SEL_BLOCK // CMP_BLOCK
    lane_blk = np.where(np.arange(nc) % ratio == 0, np.arange(nc) // ratio, -1)
    e_mat = jnp.asarray(lane_blk[:, None] == (np.arange(past) // SEL_BLOCK)[None, :], BF16)
    return dict(page_table=pt, sel_cache=sel_cache, nsel=nsel, kwin=kwin, nwin=nwin, ck=ck, cv=cv, bsel=bsel, bwin=bwin,
                bcmp=bcmp, e_mat=e_mat, past=past)


def _sample_attention_layer(q, gates, att):
    db, ts, _ = q.shape
    rows = GROUP * KV_HEADS * ts
    q5 = q.reshape(db, ts, KV_HEADS, GROUP, HEAD_DIM).transpose(0, 3, 2, 1, 4)
    eye = jnp.eye(KV_HEADS, dtype=q.dtype)
    q_bd = (q5[:, :, :, :, None, :] * eye[None, None, :, None, :, None]).reshape(db, rows, KV_HEADS * HEAD_DIM)
    g5 = gates[:, :, :3 * N_HEADS].reshape(db, ts, KV_HEADS, GROUP, 3).transpose(0, 3, 2, 1, 4)
    g_rows = jnp.pad(g5.reshape(db, rows, 3), ((0, 0), (0, 0), (0, 125)))
    o = sample_attention(att["page_table"], q_bd, g_rows, att["sel_cache"], att["nsel"], att["kwin"],
                         att["nwin"], att["ck"], att["cv"], att["bsel"], att["bwin"], att["bcmp"],
                         att["e_mat"], ts)
    o = o.reshape(db, GROUP, KV_HEADS, ts, HEAD_DIM).transpose(0, 3, 2, 1, 4)
    return o.reshape(db, ts, N_HEADS * HEAD_DIM).astype(BF16)


def kernel(x_prompt, x_sample, c_prompt, c_sample, cache_kv_cmp, cache_kv_sel, cache_kv_win, state_conv, page_table, w_ada, b_ada, norm_g, w_pw1, b_pw1, w_dw, b_dw, ln_g, ln_b, w_pw2, b_pw2, g_kv, w_kv, w_phi1, b_phi1, w_phi2, b_phi2, pe_cmp, w_qg, w_o, rel_table, w_gate, w_up, w_down, final_g):
    P = dict(w_ada=w_ada, b_ada=b_ada, norm_g=norm_g, w_pw1=w_pw1, b_pw1=b_pw1, w_dw=w_dw, b_dw=b_dw,
             ln_g=ln_g, ln_b=ln_b, w_pw2=w_pw2, b_pw2=b_pw2, g_kv=g_kv, w_kv=w_kv, w_phi1=w_phi1,
             b_phi1=b_phi1, w_phi2=w_phi2, b_phi2=b_phi2, pe_cmp=pe_cmp, w_qg=w_qg, w_o=w_o,
             rel_table=rel_table, w_gate=w_gate, w_up=w_up, w_down=w_down, final_g=final_g)
    W = _prep_weights(P)
    bp, tp, d = x_prompt.shape
    db, ts, _ = x_sample.shape
    depth = w_ada.shape[0]

    n_c = bp + db
    r_pad = -(-n_c // 8) * 8
    c_all = jnp.pad(jnp.concatenate([c_prompt, c_sample], axis=0), ((0, r_pad - n_c), (0, 0)))
    mod = ada_modulation(c_all, w_ada.reshape(depth * 2, d, 3 * d), b_ada.reshape(depth * 2, 1, 3 * d))
    mod_rows = mod.reshape(depth * 2, r_pad, 3, 1, d)
    mod_tok = jnp.repeat(mod[:, bp:n_c], ts, axis=1)
    mods_p, mods_s = {}, {}
    for l in range(depth):
        for j in range(2):
            lj = l * 2 + j
            mods_p[(l, j)] = tuple(
                _Slice(mod_rows, (None, None, None, 1, d), lambda grp, lj=lj, k=k: (lj, grp, k, 0, 0))
                for k in range(3))
            mods_s[(l, j)] = tuple(
                _Slice(mod_tok, (None, db * ts, d), lambda grp, lj=lj, k=k: (lj, 0, k)) for k in range(3)
            ) + (mod[lj, bp:n_c, 2 * d:3 * d][:, None, :],)

    past = page_table.shape[1] * cache_kv_cmp.shape[1]
    bias_dist = _BiasTable(rel_table, max(tp, past + 2 * NEW_PAD), pad=max(tp, NEW_PAD))
    caches = dict(cache_kv_cmp=cache_kv_cmp, cache_kv_sel=cache_kv_sel, cache_kv_win=cache_kv_win,
                  state_conv=state_conv, page_table=page_table)
    y_p, cmp_p, sel_p, win_p, conv_p = _trunk(x_prompt, mods_p, True, None, P, W, bias_dist)
    y_s, cmp_s, sel_s, win_s, conv_s = _trunk(x_sample, mods_s, False, caches, P, W, bias_dist)
    return (y_p, y_s, cmp_p, cmp_s, sel_p, sel_s, win_p, win_s, conv_p, conv_s)
```

```python
import functools
import math

import numpy as np
import jax
import jax.numpy as jnp
from jax import lax
from jax.experimental import pallas as pl
from jax.experimental.pallas import tpu as pltpu

F32 = jnp.float32
BF16 = jnp.bfloat16

N_HEADS = 16
KV_HEADS = 4
GROUP = N_HEADS // KV_HEADS
HEAD_DIM = 64
CONV_W = 31
CONV_PAD = 32
SUBLANES = 8
LANES = 128
PROMPT_ROW_TILE = 256
CMP_BLOCK = 32
SEL_BLOCK = 64
N_SEL = 16
WINDOW = 512
N_BUCKETS = 32
MAX_DISTANCE = 1024
EPS = 1e-6
FORCE = 1e4
MASK_NEG = -1e9
M_INIT = -1e30
QT = LANES
NEAR_TILES = 8
KEY_STEP = 512
NEAR_SPAN = NEAR_TILES + KEY_STEP // QT - 1
V7X_VMEM_LIMIT_BYTES = 56 * 1024 * 1024


def _cparams(*sem):
    return pltpu.CompilerParams(dimension_semantics=sem, vmem_limit_bytes=V7X_VMEM_LIMIT_BYTES)


class _Slice:
    def __init__(self, arr, block, index):
        self.arr, self.block, self.index = arr, block, index
        self.shape = tuple(b for b in block if b is not None)

    def spec(self, tiles_per_group=1):
        index = self.index
        return pl.BlockSpec(self.block, lambda i, *_: index(i // tiles_per_group))


def _layer(arr, l):
    nd = arr.ndim
    return _Slice(arr, (None,) + arr.shape[1:], lambda _: (l,) + (0,) * (nd - 1))


def _arr(a):
    return a.arr if isinstance(a, _Slice) else a


def _full_spec(a):
    if isinstance(a, _Slice):
        return a.spec()
    nd = a.ndim
    return pl.BlockSpec(a.shape, lambda *_: (0,) * nd)


def _silu(x):
    return x * jax.nn.sigmoid(x)


def _norm_mod(x, g, shift, scale):
    ms = jnp.mean(x * x, axis=-1, keepdims=True)
    return x * lax.rsqrt(ms + EPS) * g * (1.0 + scale) + shift


def _dot(a, b):
    return jnp.dot(a, b, preferred_element_type=F32)


def _dot_nt(a, b):
    return lax.dot_general(a, b, (((1,), (1,)), ((), ())), preferred_element_type=F32)


def _ada_kernel(c_ref, w_ref, b_ref, o_ref):
    c = c_ref[...]
    o_ref[0] = _dot(_silu(c), w_ref[0]) + b_ref[0]


def ada_modulation(c_all, w_ada, b_ada):
    nl, d, n3 = w_ada.shape
    r = c_all.shape[0]
    tn = n3 // 2
    return pl.pallas_call(
        _ada_kernel,
        grid=(nl, n3 // tn),
        in_specs=[pl.BlockSpec((r, d), lambda l, j: (0, 0)),
                  pl.BlockSpec((1, d, tn), lambda l, j: (l, 0, j)),
                  pl.BlockSpec((1, 1, tn), lambda l, j: (l, 0, j))],
        out_specs=pl.BlockSpec((1, r, tn), lambda l, j: (l, 0, j)),
        out_shape=jax.ShapeDtypeStruct((nl, r, n3), F32),
        compiler_params=_cparams("parallel", "parallel"),
        name="ada_modulation",
    )(c_all, w_ada, b_ada)


def _row_spec(tm, width):
    return pl.BlockSpec((tm, width), lambda i: (i, 0))


def _mod_spec(mod, tiles_per_group):
    return mod.spec(tiles_per_group)


def _pw1_kernel(x_ref, g_ref, sh_ref, sc_ref, w_ref, b_ref, u_ref):
    d = x_ref.shape[1]
    h = _norm_mod(x_ref[...], g_ref[...], sh_ref[...], sc_ref[...]).astype(BF16)
    z = _dot(h, w_ref[...]) + b_ref[...]
    u_ref[...] = z[:, :d] * jax.nn.sigmoid(z[:, d:])


def pw1_glu(x, g, shift, scale, w, b, tm, tpg):
    m, d = x.shape
    return pl.pallas_call(
        _pw1_kernel,
        grid=(m // tm,),
        in_specs=[_row_spec(tm, d), _full_spec(g), _mod_spec(shift, tpg), _mod_spec(scale, tpg),
                  _full_spec(w), _full_spec(b)],
        out_specs=_row_spec(tm, d),
        out_shape=jax.ShapeDtypeStruct((m, d), F32),
        compiler_params=_cparams("parallel"),
        name="pw1_glu",
    )(*map(_arr, (x, g, shift, scale, w, b)))


def _conv_tail(rows_at, wdw_ref, bdw_ref, lg_ref, lb_ref, w2_ref, b2_ref):
    acc = None
    for k in range(CONV_W):
        term = rows_at(k + CONV_PAD - CONV_W + 1) * wdw_ref[k:k + 1, :]
        acc = term if acc is None else acc + term
    acc = acc + bdw_ref[...]
    mu = jnp.mean(acc, axis=-1, keepdims=True)
    cen = acc - mu
    var = jnp.mean(cen * cen, axis=-1, keepdims=True)
    yn = cen * lax.rsqrt(var + EPS) * lg_ref[...] + lb_ref[...]
    return _dot(_silu(yn).astype(BF16), w2_ref[...]) + b2_ref[...]


def _conv_kernel(halo_ref, u_ref, wdw_ref, bdw_ref, lg_ref, lb_ref, w2_ref, b2_ref, x_ref, gate_ref,
                 o_ref, buf, shifted, *, tiles_per_seq):
    tm = x_ref.shape[0]
    first = pl.program_id(0) % tiles_per_seq == 0
    buf[0:CONV_PAD, :] = jnp.where(first, 0.0, halo_ref[0])
    buf[CONV_PAD:CONV_PAD + tm, :] = u_ref[0]
    span = tm + CONV_PAD - SUBLANES
    for s in range(1, SUBLANES):
        shifted[s - 1] = buf[pl.ds(s, span), :]

    def rows_at(o):
        a, s = divmod(o, SUBLANES)
        if s == 0:
            return buf[pl.ds(o, tm), :]
        return shifted[s - 1, pl.ds(a * SUBLANES, tm), :]

    out = _conv_tail(rows_at, wdw_ref, bdw_ref, lg_ref, lb_ref, w2_ref, b2_ref)
    o_ref[...] = x_ref[...] + gate_ref[...] * out


def conv_pw2_residual(u, wdw, bdw, lg, lb, w2, b2, x, gate, tm, tpg):
    m, d = x.shape
    bsz, t, _ = u.shape
    assert t % tm == 0 and tm % CONV_PAD == 0 and m == bsz * t
    tps = t // tm
    return pl.pallas_call(
        functools.partial(_conv_kernel, tiles_per_seq=tps),
        grid=(m // tm,),
        in_specs=[pl.BlockSpec((1, CONV_PAD, d),
                               lambda i: (i // tps, jnp.maximum((i % tps) * (tm // CONV_PAD) - 1, 0), 0)),
                  pl.BlockSpec((1, tm, d), lambda i: (i // tps, i % tps, 0)),
                  _full_spec(wdw), _full_spec(bdw), _full_spec(lg), _full_spec(lb),
                  _full_spec(w2), _full_spec(b2), _row_spec(tm, d), _mod_spec(gate, tpg)],
        out_specs=_row_spec(tm, d),
        out_shape=jax.ShapeDtypeStruct((m, d), F32),
        scratch_shapes=[pltpu.VMEM((tm + CONV_PAD, d), F32),
                        pltpu.VMEM((SUBLANES - 1, tm + CONV_PAD - SUBLANES, d), F32)],
        compiler_params=_cparams("parallel"),
        name="conv_pw2_residual",
    )(*map(_arr, (u, u, wdw, bdw, lg, lb, w2, b2, x, gate)))


def _conv_small_kernel(full_ref, wdw_ref, bdw_ref, lg_ref, lb_ref, w2_ref, b2_ref, x_ref, gate_ref,
                       o_ref):
    tm = x_ref.shape[1]
    rows_at = lambda o: full_ref[0, pl.ds(o, tm), :]
    out = _conv_tail(rows_at, wdw_ref, bdw_ref, lg_ref, lb_ref, w2_ref, b2_ref)
    o_ref[0] = x_ref[0] + gate_ref[0] * out


def conv_pw2_residual_small(full, wdw, bdw, lg, lb, w2, b2, x, gate):
    bsz, ts, d = x.shape
    seq = lambda a: pl.BlockSpec((1,) + a.shape[1:], lambda i: (i, 0, 0))
    return pl.pallas_call(
        _conv_small_kernel,
        grid=(bsz,),
        in_specs=[seq(full), _full_spec(wdw), _full_spec(bdw), _full_spec(lg), _full_spec(lb),
                  _full_spec(w2), _full_spec(b2), seq(x), seq(gate)],
        out_specs=seq(x),
        out_shape=jax.ShapeDtypeStruct(x.shape, F32),
        compiler_params=_cparams("parallel"),
        name="conv_pw2_residual_small",
    )(*map(_arr, (full, wdw, bdw, lg, lb, w2, b2, x, gate)))


def _ffn_kernel(x_ref, g_ref, sh_ref, sc_ref, gate_ref, wg_ref, wu_ref, wd_ref, *rest, final):
    x = x_ref[...]
    h = _norm_mod(x, g_ref[...], sh_ref[...], sc_ref[...]).astype(BF16)
    a = _dot(h, wg_ref[...])
    b = _dot(h, wu_ref[...])
    act = (_silu(a) * b).astype(BF16)
    xn = x + gate_ref[...] * _dot(act, wd_ref[...])
    if final:
        fg_ref, o_ref = rest
        ms = jnp.mean(xn * xn, axis=-1, keepdims=True)
        o_ref[...] = xn * lax.rsqrt(ms + EPS) * fg_ref[...]
    else:
        (o_ref,) = rest
        o_ref[...] = xn


def ffn(x, g, shift, scale, gate, wg, wu, wd, tm, tpg, final_g=None):
    m, d = x.shape
    final = final_g is not None
    ins = [x, g, shift, scale, gate, wg, wu, wd] + ([final_g] if final else [])
    specs = [_row_spec(tm, d), _full_spec(g), _mod_spec(shift, tpg), _mod_spec(scale, tpg),
             _mod_spec(gate, tpg), _full_spec(wg), _full_spec(wu), _full_spec(wd)]
    if final:
        specs.append(_full_spec(final_g))
    return pl.pallas_call(
        functools.partial(_ffn_kernel, final=final),
        grid=(m // tm,),
        in_specs=specs,
        out_specs=_row_spec(tm, d),
        out_shape=jax.ShapeDtypeStruct((m, d), F32),
        compiler_params=_cparams("parallel"),
        name="ffn_final" if final else "ffn",
    )(*map(_arr, ins))


def _kvproj_kernel(x_ref, g_ref, w_ref, o_ref):
    x = x_ref[...]
    ms = jnp.mean(x * x, axis=-1, keepdims=True)
    h = (x * lax.rsqrt(ms + EPS) * g_ref[...]).astype(BF16)
    o_ref[...] = _dot(h, w_ref[...])


def kv_proj(x, g, w, tm):
    m, d = x.shape
    n = w.shape[1]
    return pl.pallas_call(
        _kvproj_kernel,
        grid=(m // tm,),
        in_specs=[_row_spec(tm, d), _full_spec(g), _full_spec(w)],
        out_specs=_row_spec(tm, n),
        out_shape=jax.ShapeDtypeStruct((m, n), F32),
        compiler_params=_cparams("parallel"),
        name="kv_proj",
    )(x, g, w)


def _qg_kernel(x_ref, g_ref, sh_ref, sc_ref, w_ref, q_ref, gates_ref):
    nq = q_ref.shape[1]
    h = _norm_mod(x_ref[...], g_ref[...], sh_ref[...], sc_ref[...]).astype(BF16)
    z = _dot(h, w_ref[...])
    q_ref[...] = (z[:, :nq] * (HEAD_DIM ** -0.5)).astype(BF16)
    gates_ref[...] = jax.nn.sigmoid(z[:, nq:])


def qg_proj(x, g, shift, scale, w, tm, tpg):
    m, d = x.shape
    nq = N_HEADS * HEAD_DIM
    ng = w.shape[1] - nq
    return pl.pallas_call(
        _qg_kernel,
        grid=(m // tm,),
        in_specs=[_row_spec(tm, d), _full_spec(g), _mod_spec(shift, tpg), _mod_spec(scale, tpg),
                  _full_spec(w)],
        out_specs=[_row_spec(tm, nq), _row_spec(tm, ng)],
        out_shape=[jax.ShapeDtypeStruct((m, nq), BF16), jax.ShapeDtypeStruct((m, ng), F32)],
        compiler_params=_cparams("parallel"),
        name="qg_proj",
    )(*map(_arr, (x, g, shift, scale, w)))


def _oproj_kernel(o_ref, w_ref, x_ref, gate_ref, out_ref):
    out_ref[...] = x_ref[...] + gate_ref[...] * _dot(o_ref[...], w_ref[...])


def o_proj_residual(o, w, x, gate, tm, tpg):
    m, d = x.shape
    return pl.pallas_call(
        _oproj_kernel,
        grid=(m // tm,),
        in_specs=[_row_spec(tm, o.shape[1]), _full_spec(w), _row_spec(tm, d), _mod_spec(gate, tpg)],
        out_specs=_row_spec(tm, d),
        out_shape=jax.ShapeDtypeStruct((m, d), F32),
        compiler_params=_cparams("parallel"),
        name="o_proj_residual",
    )(*map(_arr, (o, w, x, gate)))


SLAB = LANES
HEADS_PER_SLAB = SLAB // HEAD_DIM


R_PER_DOT = 2
CMP_PITCH = 40


def _compress_core(rows_at, pe_row, w1_at, b1, w2, b2):
    hidden = w2.shape[0]
    acc = None
    for i in range(CMP_BLOCK // R_PER_DOT):
        parts = [(rows_at(R_PER_DOT * i + j) + pe_row(R_PER_DOT * i + j)).astype(BF16) for j in range(R_PER_DOT)]
        term = _dot(jnp.concatenate(parts, axis=1), w1_at(i))
        acc = term if acc is None else acc + term
    hid = _silu(acc + b1)
    return [_dot(hid[:, hh * hidden:(hh + 1) * hidden].astype(BF16), w2) + b2 for hh in range(HEADS_PER_SLAB)]


def _compress_kernel(x_ref, pe_ref, w1_ref, b1_ref, w2_ref, b2_ref, o_ref, *, nblk):
    outs = _compress_core(lambda r: x_ref[0, pl.ds(r, nblk, stride=CMP_BLOCK), :],
                          lambda r: pe_ref[0, r:r + 1, :], lambda i: w1_ref[0, i],
                          b1_ref[0], w2_ref[0], b2_ref[0])
    for hh, out in enumerate(outs):
        o_ref[0, 0, hh] = out


def compress(x, nblk, pe_slab, w1_bd, b1_slab, w2, b2):
    bc, rows, _ = x.shape
    n_slab = 2 * KV_HEADS // HEADS_PER_SLAB
    slabs_per_e = n_slab // 2
    per_e = lambda a: pl.BlockSpec((1,) + a.shape[1:], lambda b, s: (s // slabs_per_e,) + (0,) * (a.ndim - 1))
    return pl.pallas_call(
        functools.partial(_compress_kernel, nblk=nblk),
        grid=(bc, n_slab),
        in_specs=[pl.BlockSpec((1, rows, SLAB), lambda b, s: (b, 0, s)),
                  pl.BlockSpec((1, CMP_BLOCK, SLAB), lambda b, s: (s, 0, 0)),
                  per_e(w1_bd), per_e(b1_slab), per_e(w2), per_e(b2)],
        out_specs=pl.BlockSpec((1, 1, HEADS_PER_SLAB, nblk, HEAD_DIM),
                               lambda b, s: (b, s // slabs_per_e, s % slabs_per_e, 0, 0)),
        out_shape=jax.ShapeDtypeStruct((bc, 2, KV_HEADS, nblk, HEAD_DIM), F32),
        compiler_params=_cparams("parallel", "parallel"),
        name="compress",
    )(x, pe_slab, w1_bd, b1_slab, w2, b2)


def _compress_paged_kernel(pt_ref, *refs, n_pages):
    page_refs = refs[:n_pages]
    pe_ref, w1_ref, b1_ref, w2_ref, b2_ref, o_ref, scr = refs[n_pages:]
    blocks_per_page = page_refs[0].shape[1] // CMP_BLOCK
    nblk = n_pages * blocks_per_page
    n_slab = pe_ref.shape[0]
    slabs_per_e = n_slab // 2
    for s in range(n_slab):
        e = s // slabs_per_e
        for p in range(n_pages):
            for j in range(blocks_per_page):
                base = (p * blocks_per_page + j) * CMP_PITCH
                scr[base:base + CMP_BLOCK, :] = page_refs[p][0, j * CMP_BLOCK:(j + 1) * CMP_BLOCK,
                                                             s * SLAB:(s + 1) * SLAB]
        outs = _compress_core(lambda r: scr[pl.ds(r, nblk, stride=CMP_PITCH), :],
                              lambda r: pe_ref[s, r:r + 1, :], lambda i: w1_ref[e, i],
                              b1_ref[e], w2_ref[e], b2_ref[e])
        for hh, out in enumerate(outs):
            o_ref[0, e, (s % slabs_per_e) * HEADS_PER_SLAB + hh] = out


def compress_paged(cache, page_table, pe_slab, w1_bd, b1_slab, w2, b2):
    n_pool, page, c = cache.shape
    db, n_pages = page_table.shape
    assert page % CMP_BLOCK == 0
    nblk = n_pages * page // CMP_BLOCK
    const = lambda a: pl.BlockSpec(a.shape, lambda b, pt: (0,) * a.ndim)

    def page_spec(k):
        return pl.BlockSpec((1, page, c), lambda b, pt: (pt[b, k], 0, 0))

    return pl.pallas_call(
        functools.partial(_compress_paged_kernel, n_pages=n_pages),
        grid_spec=pltpu.PrefetchScalarGridSpec(
            num_scalar_prefetch=1,
            grid=(db,),
            in_specs=[page_spec(k) for k in range(n_pages)]
            + [const(pe_slab), const(w1_bd), const(b1_slab), const(w2), const(b2)],
            out_specs=pl.BlockSpec((1, 2, KV_HEADS, nblk, HEAD_DIM), lambda b, pt: (b, 0, 0, 0, 0)),
            scratch_shapes=[pltpu.VMEM((nblk * CMP_PITCH, SLAB), F32)],
        ),
        out_shape=jax.ShapeDtypeStruct((db, 2, KV_HEADS, nblk, HEAD_DIM), F32),
        compiler_params=_cparams("parallel"),
        name="compress_paged",
    )(page_table, *([cache] * n_pages), pe_slab, w1_bd, b1_slab, w2, b2)


def _rel_bucket(dist):
    n = jnp.maximum(dist, 0)
    exact = N_BUCKETS // 2
    nf = jnp.maximum(n, 1).astype(F32)
    large = exact + (jnp.log(nf / exact) / math.log(MAX_DISTANCE / exact) * (N_BUCKETS - exact)).astype(jnp.int32)
    large = jnp.minimum(large, N_BUCKETS - 1)
    return jnp.where(n < exact, n, large)


class _BiasTable:
    def __init__(self, table, max_dist, pad):
        tab = table.astype(F32)

        def lookup(dist):
            bucket = _rel_bucket(dist)
            out = jnp.zeros((tab.shape[1], dist.shape[0]), F32)
            for k in range(N_BUCKETS):
                out = jnp.where(bucket[None, :] == k, tab[k][:, None], out)
            return out

        self.pad = pad
        self.n = max_dist
        self.ext = lookup(jnp.arange(-pad, max_dist))
        self.rev = lookup(max_dist - 1 - jnp.arange(pad + max_dist))

    def at(self, d):
        return self.ext[:, self.pad + d]

    def rising(self, d0, length):
        return lax.slice_in_dim(self.ext, self.pad + d0, self.pad + d0 + length, axis=1)

    def falling(self, d0, length, step=1):
        start = self.n - 1 - d0
        return lax.slice(self.rev, (0, start), (self.rev.shape[0], start + step * (length - 1) + 1), (1, step))

    def frames(self, d0, count, hop, length):
        reps = length // hop
        seq = self.rising(d0, hop * (count + reps - 1)).reshape(-1, count + reps - 1, hop)
        return jnp.concatenate([seq[:, a:a + count] for a in range(reps)], axis=-1)

    def toeplitz(self, d0, count, hop, size):
        assert hop == size
        period = 2 * size - 1
        base = self.rising(d0 - (size - 1), hop * (count + 1)).reshape(-1, count + 1, hop)
        w = jnp.concatenate([base[:, :count, size - 1:], base[:, 1:, :size - 1],
                             base[:, :count, :size - 1]], axis=-1)
        rows = jnp.tile(w, (1, 1, size))[:, :, :size * (period - 1)]
        return rows.reshape(w.shape[0], count, size, period - 1)[..., :size]


def _softmax_step(s, vt, m, l, acc):
    m_new = jnp.maximum(m, jnp.max(s, axis=0, keepdims=True))
    alpha = jnp.exp(m - m_new)
    p = jnp.exp(s - m_new)
    l = alpha * l + jnp.sum(p, axis=0, keepdims=True)
    acc = alpha * acc + _dot(vt, p.astype(BF16))
    return m_new, l, acc


def _attn_kernel(bfar_ref, q_ref, gt_ref, kaug_ref, vst_ref, kw_ref, vwt_ref, ck_ref, cvt_ref,
                 bc_ref, bn_ref, o_ref, imp_ref, sc_ref, s_ref, *, nb, nc):
    g = pl.program_id(0)
    qt = pl.program_id(2)
    qs = qt * QT
    gq = GROUP * QT

    lane_cat = lambda parts: jnp.concatenate(parts, axis=1)
    q_t = q_ref[0].astype(F32).T.astype(BF16)
    q_heads = [q_t[h * HEAD_DIM:(h + 1) * HEAD_DIM] for h in range(GROUP)]
    q_plain = lane_cat(q_heads)

    col_i = lax.broadcasted_iota(jnp.int32, (1, gq), 1) % QT

    s_c = _dot(ck_ref[0, 0], q_plain) + lane_cat([bc_ref[h, 0] for h in range(GROUP)])
    n_idx = lax.broadcasted_iota(jnp.int32, (nc, gq), 0)
    valid_c = (n_idx * CMP_BLOCK + CMP_BLOCK - 1) <= (qs + col_i)
    s_c = jnp.where(valid_c, s_c, M_INIT)
    m_c = jnp.max(s_c, axis=0, keepdims=True)
    p_c = jnp.where(valid_c, jnp.exp(s_c - m_c), 0.0)
    l_c = jnp.sum(p_c, axis=0, keepdims=True)
    p_c = p_c * jnp.where(l_c > 0.0, 1.0 / l_c, 0.0)
    o_c = _dot(cvt_ref[0, 0], p_c.astype(BF16))

    imp = p_c[:, 0:QT]
    for h in range(1, GROUP):
        imp = imp + p_c[:, h * QT:(h + 1) * QT]
    imp_ref[...] = imp
    ratio = SEL_BLOCK // CMP_BLOCK
    imp2 = imp_ref[pl.ds(0, nb, stride=ratio), :]
    for r in range(1, ratio):
        imp2 = imp2 + imp_ref[pl.ds(r, nb, stride=ratio), :]
    j_idx = lax.broadcasted_iota(jnp.int32, (nb, QT), 0)
    i_idx = lax.broadcasted_iota(jnp.int32, (nb, QT), 1)
    jq = (qs + i_idx) // SEL_BLOCK
    forced = (j_idx == 0) | (j_idx == jq) | (j_idx == jq - 1)
    score = jnp.where(j_idx <= jq, jnp.where(forced, FORCE, imp2), -1.0)
    sc_ref[...] = score

    sub8 = lax.broadcasted_iota(jnp.int32, (8, QT), 0)
    groups = [score[8 * r:8 * r + 8] for r in range(nb // 8)]
    counts = [jnp.zeros((8, QT), F32) for _ in groups]
    for c in range(nb):
        row = sc_ref[c:c + 1, :]
        for r, grp_score in enumerate(groups):
            if r > c // 8:
                beats = row >= grp_score
            elif r < c // 8:
                beats = row > grp_score
            else:
                tie = jnp.where(row == grp_score, 1.0, 0.0) * jnp.where(sub8 > c % 8, 1.0, 0.0)
                counts[r] = counts[r] + tie
                beats = row > grp_score
            counts[r] = counts[r] + jnp.where(beats, 1.0, 0.0)
    cnt = jnp.concatenate(counts, axis=0)
    selected = (cnt < float(min(N_SEL, nb))) & (score >= 0.0)
    mask_feat = jnp.where(selected, 0.0, MASK_NEG).astype(BF16)
    q_aug = lane_cat([jnp.concatenate([qh, mask_feat], axis=0) for qh in q_heads])

    far_row = lane_cat([jnp.full((1, QT), bfar_ref[g * GROUP + h], F32) for h in range(GROUP)])

    def near_bias(kt0, n_sub):
        parts = []
        for sub in range(n_sub):
            dm = jnp.clip(qt - (kt0 + sub), 0, NEAR_SPAN - 1)
            parts.append(lane_cat([bn_ref[h, dm] for h in range(GROUP)]))
        return jnp.concatenate(parts, axis=0)

    init = (jnp.full((1, gq), M_INIT, F32), jnp.zeros((1, gq), F32), jnp.zeros((HEAD_DIM, gq), F32))
    sub_per_chunk = KEY_STEP // QT
    row_k = lax.broadcasted_iota(jnp.int32, (KEY_STEP, gq), 0)

    last_chunk = qt // sub_per_chunk

    def raw_scores(c):
        ks = pl.multiple_of(jnp.minimum(c, last_chunk) * KEY_STEP, KEY_STEP)
        return _dot(kaug_ref[0, 0, pl.ds(ks, KEY_STEP), :], q_aug)

    def pipelined(finish):
        def body(c, carry):
            cur = s_ref[c % 2]
            s_ref[(c + 1) % 2] = raw_scores(c + 1)
            ks = pl.multiple_of(c * KEY_STEP, KEY_STEP)
            return _softmax_step(finish(c, cur, ks), vst_ref[0, 0, :, pl.ds(ks, KEY_STEP)], *carry)
        return body

    def far_finish(c, s, ks):
        return s + far_row

    def near_finish(c, s, ks):
        return s + near_bias(c * sub_per_chunk, sub_per_chunk)

    wt0 = jnp.maximum(qt - WINDOW // QT, 0)
    ws = pl.multiple_of(wt0 * QT, QT)
    wk = WINDOW + QT
    s_w = _dot(kw_ref[0, 0, pl.ds(ws, wk), :], q_plain) + near_bias(wt0, wk // QT)
    dist = (qs + col_i) - (ws + lax.broadcasted_iota(jnp.int32, (wk, gq), 0))
    s_w = jnp.where((dist >= 0) & (dist < WINDOW), s_w, MASK_NEG)
    p_w = jnp.exp(s_w - jnp.max(s_w, axis=0, keepdims=True))
    l_w = jnp.sum(p_w, axis=0, keepdims=True)
    o_w = _dot(vwt_ref[0, 0, :, pl.ds(ws, wk)], p_w.astype(BF16)) / l_w

    s_ref[0] = raw_scores(0)
    n_far = jnp.maximum(qt - (NEAR_TILES - 1), 0) // sub_per_chunk
    carry = lax.fori_loop(0, n_far, pipelined(far_finish), init)
    carry = lax.fori_loop(n_far, last_chunk, pipelined(near_finish), carry)
    ks = pl.multiple_of(last_chunk * KEY_STEP, KEY_STEP)
    s = near_finish(last_chunk, s_ref[last_chunk % 2], ks)
    s = jnp.where(ks + row_k <= qs + col_i, s, MASK_NEG)
    m_s, l_s, acc_s = _softmax_step(s, vst_ref[0, 0, :, pl.ds(ks, KEY_STEP)], *carry)
    o_s = acc_s / l_s

    gate = lambda br: lane_cat([gt_ref[0, 0, h * 3 + br:h * 3 + br + 1, :] for h in range(GROUP)])
    o_t = gate(0) * o_c + gate(1) * o_s + gate(2) * o_w
    o_rows = jnp.concatenate([o_t[:, h * QT:(h + 1) * QT] for h in range(GROUP)], axis=0)
    o_ref[0] = o_rows.T.astype(BF16)


def prompt_attention(q, gates_t, kaug, vst, kwin, vwt, ck, cvt, bias_cmp, bias_near, bias_far):
    bsz, t, dq = q.shape
    nb = t // SEL_BLOCK
    nc = t // CMP_BLOCK
    nq = t // QT
    gd = GROUP * HEAD_DIM
    per_bg = lambda a: pl.BlockSpec((1, 1) + a.shape[2:], lambda g, b, i, *_: (b, g, 0, 0))
    grid_spec = pltpu.PrefetchScalarGridSpec(
        num_scalar_prefetch=1,
        grid=(KV_HEADS, bsz, nq),
        in_specs=[pl.BlockSpec((1, QT, gd), lambda g, b, i, *_: (b, i, g)),
                  pl.BlockSpec((1, 1, 3 * GROUP, QT), lambda g, b, i, *_: (b, g, 0, i)),
                  per_bg(kaug), per_bg(vst), per_bg(kwin), per_bg(vwt), per_bg(ck), per_bg(cvt),
                  pl.BlockSpec((GROUP, 1, nc, QT), lambda g, b, i, *_: (g, i, 0, 0)),
                  pl.BlockSpec((GROUP, NEAR_SPAN, QT, QT), lambda g, b, i, *_: (g, 0, 0, 0))],
        out_specs=pl.BlockSpec((1, QT, gd), lambda g, b, i, *_: (b, i, g)),
        scratch_shapes=[pltpu.VMEM((nc, QT), F32), pltpu.VMEM((nb, QT), F32),
                        pltpu.VMEM((2, KEY_STEP, GROUP * QT), F32)],
    )
    return pl.pallas_call(
        functools.partial(_attn_kernel, nb=nb, nc=nc),
        grid_spec=grid_spec,
        out_shape=jax.ShapeDtypeStruct((bsz, t, dq), BF16),
        compiler_params=_cparams("parallel", "parallel", "parallel"),
        name="prompt_attention",
    )(bias_far, q, gates_t, kaug, vst, kwin, vwt, ck, cvt, bias_cmp, bias_near)


KEY_CHUNK = 1024
NEW_PAD = LANES


def _diag_heads(full, row_g):
    out = None
    for g in range(KV_HEADS):
        part = jnp.where(row_g == g, full[:, g * HEAD_DIM:(g + 1) * HEAD_DIM], 0.0)
        out = part if out is None else out + part
    return out


def _sample_attn_kernel(pt_ref, q_ref, gates_ref, *refs, past, ts, n_pages):
    page_refs = refs[:n_pages]
    (nsel_ref, kwin_ref, nwin_ref, ck_ref, cv_ref, bsel_ref, bwin_ref, bcmp_ref, e_ref, o_ref,
     s_scr) = refs[n_pages:]
    kvd = KV_HEADS * HEAD_DIM
    page = page_refs[0].shape[1]
    pages_per_chunk = KEY_CHUNK // page

    def cached(c, lo, hi):
        parts = [page_refs[c * pages_per_chunk + j][0, :, lo:hi] for j in range(pages_per_chunk)]
        return jnp.concatenate(parts, axis=0).astype(BF16)

    rows = GROUP * KV_HEADS * ts
    nc = ck_ref.shape[1]
    q = q_ref[0]
    r_idx = lax.broadcasted_iota(jnp.int32, (rows, 1), 0)
    row_q = r_idx % ts
    row_g = (r_idx // ts) % KV_HEADS

    s_c = _dot_nt(q, ck_ref[0]) + bcmp_ref[...]
    p_c = jnp.exp(s_c - jnp.max(s_c, axis=1, keepdims=True))
    p_c = p_c / jnp.sum(p_c, axis=1, keepdims=True)
    o_c = _diag_heads(_dot(p_c.astype(BF16), cv_ref[0]), row_g)

    per = KV_HEADS * ts
    imp = p_c[0:per]
    for h in range(1, GROUP):
        imp = imp + p_c[h * per:(h + 1) * per]
    ratio = SEL_BLOCK // CMP_BLOCK
    imp2 = imp
    for r in range(1, ratio):
        imp2 = imp2 + pltpu.roll(imp, nc - r, axis=1)
    lane = lax.broadcasted_iota(jnp.int32, (per, nc), 1)
    blk = lane // ratio
    jq = past // SEL_BLOCK
    forced = (blk == 0) | (blk == jq - 1)
    score = jnp.where(lane % ratio == 0, jnp.where(forced, FORCE, imp2), -2.0)
    cnt = jnp.where(score < FORCE, 1.0, 0.0)
    for c in range(nc // ratio):
        col = score[:, c * ratio:c * ratio + 1]
        ge = jnp.where(col >= score, 1.0, 0.0)
        gt = jnp.where(col > score, 1.0, 0.0)
        cnt = cnt + jnp.where(lane > c * ratio, ge, gt)
    selected = (cnt < float(min(N_SEL, jq + 1))) & (score >= 0.0)
    mask_feat = jnp.where(selected, 0.0, MASK_NEG).astype(BF16)
    mask_feat = jnp.concatenate([mask_feat] * GROUP, axis=0)

    for c in range(past // KEY_CHUNK):
        sl = slice(c * KEY_CHUNK, (c + 1) * KEY_CHUNK)
        s_scr[:, sl] = _dot_nt(q, cached(c, 0, kvd)) + bsel_ref[:, sl] + _dot(mask_feat, e_ref[:, sl])
    new_j = lax.broadcasted_iota(jnp.int32, (rows, NEW_PAD), 1)
    new_ok = new_j <= row_q
    s_new = _dot_nt(q, nsel_ref[0, :, 0:kvd].astype(BF16)) + bsel_ref[:, past:past + NEW_PAD]
    s_scr[:, past:past + NEW_PAD] = jnp.where(new_ok, s_new, MASK_NEG)
    s_all = s_scr[...]
    p_s = jnp.exp(s_all - jnp.max(s_all, axis=1, keepdims=True))
    l_s = jnp.sum(p_s, axis=1, keepdims=True)
    s_scr[...] = p_s
    acc = _dot(s_scr[:, past:past + NEW_PAD].astype(BF16), nsel_ref[0, :, kvd:2 * kvd].astype(BF16))
    for c in range(past // KEY_CHUNK):
        sl = slice(c * KEY_CHUNK, (c + 1) * KEY_CHUNK)
        acc = acc + _dot(s_scr[:, sl].astype(BF16), cached(c, kvd, 2 * kvd))
    o_s = _diag_heads(acc, row_g) / l_s

    wlen = kwin_ref.shape[1]
    win_i = lax.broadcasted_iota(jnp.int32, (rows, wlen), 1)
    s_w1 = _dot_nt(q, kwin_ref[0, :, 0:kvd].astype(BF16)) + bwin_ref[:, 0:wlen]
    s_w1 = jnp.where(win_i + (WINDOW - wlen) > row_q, s_w1, MASK_NEG)
    s_w2 = _dot_nt(q, nwin_ref[0, :, 0:kvd].astype(BF16)) + bwin_ref[:, wlen:wlen + NEW_PAD]
    s_w2 = jnp.where(new_ok, s_w2, MASK_NEG)
    m_w = jnp.maximum(jnp.max(s_w1, axis=1, keepdims=True), jnp.max(s_w2, axis=1, keepdims=True))
    p_w1 = jnp.exp(s_w1 - m_w)
    p_w2 = jnp.exp(s_w2 - m_w)
    l_w = jnp.sum(p_w1, axis=1, keepdims=True) + jnp.sum(p_w2, axis=1, keepdims=True)
    acc_w = (_dot(p_w1.astype(BF16), kwin_ref[0, :, kvd:2 * kvd].astype(BF16))
             + _dot(p_w2.astype(BF16), nwin_ref[0, :, kvd:2 * kvd].astype(BF16)))
    o_w = _diag_heads(acc_w, row_g) / l_w

    gts = gates_ref[0]
    o_ref[0] = gts[:, 0:1] * o_c + gts[:, 1:2] * o_s + gts[:, 2:3] * o_w


def sample_attention(page_table, q_bd, gates, cache, nsel, kwin, nwin, ck, cv, bsel, bwin, bcmp, e_mat, ts):
    db, rows, kvd = q_bd.shape
    n_pages = page_table.shape[1]
    page = cache.shape[1]
    past = n_pages * page
    assert KEY_CHUNK % page == 0 and past % KEY_CHUNK == 0
    seq = lambda a: pl.BlockSpec((1,) + a.shape[1:], lambda b, pt: (b,) + (0,) * (a.ndim - 1))
    const = lambda a: pl.BlockSpec(a.shape, lambda b, pt: (0,) * a.ndim)

    def page_spec(k):
        return pl.BlockSpec((1,) + cache.shape[1:], lambda b, pt: (pt[b, k], 0, 0))

    return pl.pallas_call(
        functools.partial(_sample_attn_kernel, past=past, ts=ts, n_pages=n_pages),
        grid_spec=pltpu.PrefetchScalarGridSpec(
            num_scalar_prefetch=1,
            grid=(db,),
            in_specs=[seq(q_bd), seq(gates)] + [page_spec(k) for k in range(n_pages)]
            + [seq(nsel), seq(kwin), seq(nwin), seq(ck), seq(cv), const(bsel), const(bwin), const(bcmp),
               const(e_mat)],
            out_specs=pl.BlockSpec((1, rows, HEAD_DIM), lambda b, pt: (b, 0, 0)),
            scratch_shapes=[pltpu.VMEM((rows, past + NEW_PAD), F32)],
        ),
        out_shape=jax.ShapeDtypeStruct((db, rows, HEAD_DIM), F32),
        compiler_params=_cparams("parallel"),
        name="sample_attention",
    )(page_table, q_bd, gates, *([cache] * n_pages), nsel, kwin, nwin, ck, cv, bsel, bwin, bcmp, e_mat)


def _prep_weights(P):
    W = {}
    cast = lambda a: a.astype(BF16)
    for k in ("w_pw1", "w_pw2", "w_kv", "w_o", "w_gate", "w_up", "w_down", "w_phi1", "w_phi2"):
        W[k] = cast(P[k])
    nq = N_HEADS * HEAD_DIM
    wqg = P["w_qg"]
    pad = LANES - (wqg.shape[2] - nq)
    W["w_qg"] = cast(jnp.pad(wqg, ((0, 0), (0, 0), (0, pad))))
    rows = lambda a: a.reshape(-1, 1, a.shape[-1])
    for k in ("norm_g", "b_pw1", "b_dw", "ln_g", "ln_b", "b_pw2"):
        W[k] = rows(P[k])
    W["w_dw"] = jnp.pad(P["w_dw"], ((0, 0), (0, CONV_PAD - CONV_W), (0, 0)))
    return W


def _trunk(x, mods, is_prompt, caches, P, W, bias_dist):
    bsz, t, d = x.shape
    m = bsz * t
    depth = P["w_gate"].shape[0]
    n_a = P["w_pw1"].shape[0]
    if is_prompt:
        tm, tpg = PROMPT_ROW_TILE, t // PROMPT_ROW_TILE
    else:
        tm, tpg = m, 1
    x2 = x.reshape(m, d)
    row2 = lambda a: a.reshape(1, -1)
    conv_states = []
    y = None
    for l in range(depth):
        shift, scale, gate = mods[(l, 0)][:3]
        g0 = _layer(W["norm_g"], 2 * l)
        if l < n_a:
            u = pw1_glu(x2, g0, shift, scale, _layer(W["w_pw1"], l), _layer(W["b_pw1"], l), tm, tpg)
            u3 = u.reshape(bsz, t, d)
            args = tuple(_layer(W[k], l) for k in ("w_dw", "b_dw", "ln_g", "ln_b", "w_pw2", "b_pw2"))
            if is_prompt:
                conv_states.append(u3[:, -(CONV_W - 1):])
                x2 = conv_pw2_residual(u3, *args, x2, gate, tm, tpg)
            else:
                hist = jnp.pad(caches["state_conv"][l], ((0, 0), (CONV_PAD - CONV_W + 1, 0), (0, 0)))
                full = jnp.concatenate([hist, u3], axis=1)
                conv_states.append(full[:, -(CONV_W - 1):])
                x2 = conv_pw2_residual_small(full, *args, x2.reshape(bsz, t, d),
                                             mods[(l, 0)][3]).reshape(m, d)
        else:
            if l == n_a:
                rows = kv_proj(x2, row2(P["g_kv"]), W["w_kv"], tm)
                rows6 = rows.reshape(bsz, t, 3, 2, KV_HEADS, HEAD_DIM)
                if is_prompt:
                    att = _prompt_attention_setup(rows6, P, W, bias_dist)
                    win_state = rows6[:, -min(WINDOW, t):, 2]
                else:
                    att = _sample_attention_setup(rows6, caches, P, W, bias_dist)
                    wl = caches["cache_kv_win"].shape[1]
                    win_state = jnp.concatenate([caches["cache_kv_win"], rows6[:, :, 2]], axis=1)[:, -wl:]
            lb = l - n_a
            q, gates = qg_proj(x2, g0, shift, scale, _layer(W["w_qg"], lb), tm, tpg)
            if is_prompt:
                o = _prompt_attention_layer(q.reshape(bsz, t, -1), gates.reshape(bsz, t, -1), att)
            else:
                o = _sample_attention_layer(q.reshape(bsz, t, -1), gates.reshape(bsz, t, -1), att)
            x2 = o_proj_residual(o.reshape(m, -1), _layer(W["w_o"], lb), x2, gate, tm, tpg)
        shift, scale, gate = mods[(l, 1)][:3]
        fg = row2(P["final_g"]) if l == depth - 1 else None
        out = ffn(x2, _layer(W["norm_g"], 2 * l + 1), shift, scale, gate, _layer(W["w_gate"], l),
                  _layer(W["w_up"], l), _layer(W["w_down"], l), tm, tpg, final_g=fg)
        if l == depth - 1:
            y = out
        else:
            x2 = out
    return (y.reshape(bsz, t, d), rows6[:, :, 0], rows6[:, :, 1], win_state, jnp.stack(conv_states))


def _compress_weights(P, W):
    pe = P["pe_cmp"]
    pe_slab = jnp.repeat(jnp.tile(pe, (1, 1, HEADS_PER_SLAB)), KV_HEADS // HEADS_PER_SLAB, axis=0)
    w1 = W["w_phi1"].reshape(2, CMP_BLOCK, HEAD_DIM, -1)
    zero = jnp.zeros_like(w1)
    w1_bd = jnp.concatenate([jnp.concatenate([w1, zero], axis=-1), jnp.concatenate([zero, w1], axis=-1)],
                            axis=2)
    w1_bd = w1_bd.reshape(2, CMP_BLOCK // R_PER_DOT, R_PER_DOT * SLAB, -1)
    b1_slab = jnp.tile(P["b_phi1"], (1, HEADS_PER_SLAB))[:, None, :]
    return pe_slab, w1_bd, b1_slab, W["w_phi2"], P["b_phi2"][:, None, :]


def _prompt_attention_setup(rows6, P, W, bias_dist):
    bsz, t = rows6.shape[:2]
    nb = t // SEL_BLOCK
    nc = t // CMP_BLOCK
    nq = t // QT
    cmp = compress(rows6.reshape(bsz, t, -1), nc, *_compress_weights(P, W))
    ck = cmp[:, 0].astype(BF16)
    cvt = cmp[:, 1].transpose(0, 1, 3, 2).astype(BF16)
    onehot = (jnp.arange(t)[:, None] // SEL_BLOCK == jnp.arange(nb)[None, :]).astype(BF16)
    ksel = rows6[:, :, 1, 0].transpose(0, 2, 1, 3).astype(BF16)
    kaug = jnp.concatenate([ksel, jnp.broadcast_to(onehot, (bsz, KV_HEADS, t, nb))], axis=-1)
    vst = rows6[:, :, 1, 1].transpose(0, 2, 3, 1).astype(BF16)
    kwin = rows6[:, :, 2, 0].transpose(0, 2, 1, 3).astype(BF16)
    vwt = rows6[:, :, 2, 1].transpose(0, 2, 3, 1).astype(BF16)
    per_qt = QT // CMP_BLOCK
    n_u = nc + per_qt * (nq - 1)
    frames = bias_dist.frames(-(CMP_BLOCK - 1) - CMP_BLOCK * (nc - 1), n_u, CMP_BLOCK, QT)[:, ::-1]
    frames = lax.optimization_barrier(frames)
    bias_cmp = jnp.stack([frames[:, per_qt * (nq - 1 - i):per_qt * (nq - 1 - i) + nc] for i in range(nq)],
                         axis=1)
    bias_near = bias_dist.toeplitz(0, NEAR_SPAN, QT, QT)
    bias_far = bias_dist.at(NEAR_TILES * QT)
    return dict(kaug=kaug, vst=vst, kwin=kwin, vwt=vwt, ck=ck, cvt=cvt, bias_cmp=bias_cmp,
                bias_near=bias_near, bias_far=bias_far)


def _prompt_attention_layer(q, gates, att):
    bsz, t, _ = q.shape
    gates_t = gates[:, :, :3 * N_HEADS].reshape(bsz, t, KV_HEADS, 3 * GROUP).transpose(0, 2, 3, 1)
    return prompt_attention(q, gates_t, att["kaug"], att["vst"], att["kwin"], att["vwt"], att["ck"],
                            att["cvt"], att["bias_cmp"], att["bias_near"], att["bias_far"])


def _sample_attention_setup(rows6, caches, P, W, bias_dist):
    db, ts = rows6.shape[:2]
    pt = caches["page_table"]
    n_pages = pt.shape[1]
    page = caches["cache_kv_cmp"].shape[1]
    past = n_pages * page
    c = 2 * KV_HEADS * HEAD_DIM
    sel_cache = caches["cache_kv_sel"].reshape(-1, page, c)
    nc = past // CMP_BLOCK
    assert (past + ts) // CMP_BLOCK == nc and past % KEY_CHUNK == 0
    cmp = compress_paged(caches["cache_kv_cmp"].reshape(-1, page, c), pt, *_compress_weights(P, W))
    ck = cmp[:, 0].transpose(0, 2, 1, 3).reshape(db, nc, KV_HEADS * HEAD_DIM).astype(BF16)
    cv = cmp[:, 1].transpose(0, 2, 1, 3).reshape(db, nc, KV_HEADS * HEAD_DIM).astype(BF16)
    pad_new = lambda a: jnp.pad(a.reshape(db, ts, c), ((0, 0), (0, NEW_PAD - ts), (0, 0)))
    nsel = pad_new(rows6[:, :, 1])
    nwin = pad_new(rows6[:, :, 2])
    kwin = caches["cache_kv_win"].reshape(db, -1, c)
    wlen = kwin.shape[1]
    rows = GROUP * KV_HEADS * ts
    def by_row(f):
        per_q = jnp.stack([f(q) for q in range(ts)], axis=0)
        per_q = per_q.reshape(ts, KV_HEADS, GROUP, -1).transpose(2, 1, 0, 3)
        return per_q.reshape(rows, -1)

    bsel = by_row(lambda q: bias_dist.falling(past + q, past + NEW_PAD))
    bwin = by_row(lambda q: jnp.concatenate([bias_dist.falling(wlen + q, wlen),
                                             bias_dist.falling(q, NEW_PAD)], axis=1))
    bcmp = by_row(lambda q: bias_dist.falling(past + q - (CMP_BLOCK - 1), nc, step=CMP_BLOCK))
    ratio = SEL_BLOCK // CMP_BLOCK
    lane_blk = np.where(np.arange(nc) % ratio == 0, np.arange(nc) // ratio, -1)
    e_mat = jnp.asarray(lane_blk[:, None] == (np.arange(past) // SEL_BLOCK)[None, :], BF16)
    return dict(page_table=pt, sel_cache=sel_cache, nsel=nsel, kwin=kwin, nwin=nwin, ck=ck, cv=cv, bsel=bsel, bwin=bwin,
                bcmp=bcmp, e_mat=e_mat, past=past)


def _sample_attention_layer(q, gates, att):
    db, ts, _ = q.shape
    rows = GROUP * KV_HEADS * ts
    q5 = q.reshape(db, ts, KV_HEADS, GROUP, HEAD_DIM).transpose(0, 3, 2, 1, 4)
    eye = jnp.eye(KV_HEADS, dtype=q.dtype)
    q_bd = (q5[:, :, :, :, None, :] * eye[None, None, :, None, :, None]).reshape(db, rows, KV_HEADS * HEAD_DIM)
    g5 = gates[:, :, :3 * N_HEADS].reshape(db, ts, KV_HEADS, GROUP, 3).transpose(0, 3, 2, 1, 4)
    g_rows = jnp.pad(g5.reshape(db, rows, 3), ((0, 0), (0, 0), (0, LANES - 3)))
    o = sample_attention(att["page_table"], q_bd, g_rows, att["sel_cache"], att["nsel"], att["kwin"],
                         att["nwin"], att["ck"], att["cv"], att["bsel"], att["bwin"], att["bcmp"],
                         att["e_mat"], ts)
    o = o.reshape(db, GROUP, KV_HEADS, ts, HEAD_DIM).transpose(0, 3, 2, 1, 4)
    return o.reshape(db, ts, N_HEADS * HEAD_DIM).astype(BF16)


def kernel(x_prompt, x_sample, c_prompt, c_sample, cache_kv_cmp, cache_kv_sel, cache_kv_win, state_conv, page_table, w_ada, b_ada, norm_g, w_pw1, b_pw1, w_dw, b_dw, ln_g, ln_b, w_pw2, b_pw2, g_kv, w_kv, w_phi1, b_phi1, w_phi2, b_phi2, pe_cmp, w_qg, w_o, rel_table, w_gate, w_up, w_down, final_g):
    P = dict(w_ada=w_ada, b_ada=b_ada, norm_g=norm_g, w_pw1=w_pw1, b_pw1=b_pw1, w_dw=w_dw, b_dw=b_dw,
             ln_g=ln_g, ln_b=ln_b, w_pw2=w_pw2, b_pw2=b_pw2, g_kv=g_kv, w_kv=w_kv, w_phi1=w_phi1,
             b_phi1=b_phi1, w_phi2=w_phi2, b_phi2=b_phi2, pe_cmp=pe_cmp, w_qg=w_qg, w_o=w_o,
             rel_table=rel_table, w_gate=w_gate, w_up=w_up, w_down=w_down, final_g=final_g)
    W = _prep_weights(P)
    bp, tp, d = x_prompt.shape
    db, ts, _ = x_sample.shape
    depth = w_ada.shape[0]

    n_c = bp + db
    r_pad = -(-n_c // 8) * 8
    c_all = jnp.pad(jnp.concatenate([c_prompt, c_sample], axis=0), ((0, r_pad - n_c), (0, 0)))
    mod = ada_modulation(c_all, w_ada.reshape(depth * 2, d, 3 * d), b_ada.reshape(depth * 2, 1, 3 * d))
    mod_rows = mod.reshape(depth * 2, r_pad, 3, 1, d)
    mod_tok = jnp.repeat(mod[:, bp:n_c], ts, axis=1)
    mods_p, mods_s = {}, {}
    for l in range(depth):
        for j in range(2):
            lj = l * 2 + j
            mods_p[(l, j)] = tuple(
                _Slice(mod_rows, (None, None, None, 1, d), lambda grp, lj=lj, k=k: (lj, grp, k, 0, 0))
                for k in range(3))
            mods_s[(l, j)] = tuple(
                _Slice(mod_tok, (None, db * ts, d), lambda grp, lj=lj, k=k: (lj, 0, k)) for k in range(3)
            ) + (mod[lj, bp:n_c, 2 * d:3 * d][:, None, :],)

    past = page_table.shape[1] * cache_kv_cmp.shape[1]
    bias_dist = _BiasTable(rel_table, max(tp, past + 2 * NEW_PAD), pad=max(tp, NEW_PAD))
    caches = dict(cache_kv_cmp=cache_kv_cmp, cache_kv_sel=cache_kv_sel, cache_kv_win=cache_kv_win,
                  state_conv=state_conv, page_table=page_table)
    y_p, cmp_p, sel_p, win_p, conv_p = _trunk(x_prompt, mods_p, True, None, P, W, bias_dist)
    y_s, cmp_s, sel_s, win_s, conv_s = _trunk(x_sample, mods_s, False, caches, P, W, bias_dist)
    return (y_p, y_s, cmp_p, cmp_s, sel_p, sel_s, win_p, win_s, conv_p, conv_s)
```

```python
import functools
import math

import numpy as np
import jax
import jax.numpy as jnp
from jax import lax
from jax.experimental import pallas as pl
from jax.experimental.pallas import tpu as pltpu

F32 = jnp.float32
BF16 = jnp.bfloat16

N_HEADS = 16
KV_HEADS = 4
GROUP = N_HEADS // KV_HEADS
HEAD_DIM = 64
CONV_W = 31
CONV_PAD = 32
SUBLANES = 8
LANES = 128
PROMPT_ROW_TILE = 256
CMP_BLOCK = 32
SEL_BLOCK = 64
N_SEL = 16
WINDOW = 512
N_BUCKETS = 32
MAX_DISTANCE = 1024
EPS = 1e-6
FORCE = 1e4
MASK_NEG = -1e9
M_INIT = -1e30
QT = LANES
NEAR_TILES = 8
KEY_STEP = 512
NEAR_SPAN = NEAR_TILES + KEY_STEP // QT - 1
V7X_VMEM_LIMIT_BYTES = 56 * 1024 * 1024


def _cparams(*sem):
    return pltpu.CompilerParams(dimension_semantics=sem, vmem_limit_bytes=V7X_VMEM_LIMIT_BYTES)


class _Slice:
    def __init__(self, arr, block, index):
        self.arr, self.block, self.index = arr, block, index
        self.shape = tuple(b for b in block if b is not None)

    def spec(self, tiles_per_group=1):
        index = self.index
        return pl.BlockSpec(self.block, lambda i, *_: index(i // tiles_per_group))


def _layer(arr, l):
    nd = arr.ndim
    return _Slice(arr, (None,) + arr.shape[1:], lambda _: (l,) + (0,) * (nd - 1))


def _arr(a):
    return a.arr if isinstance(a, _Slice) else a


def _full_spec(a):
    if isinstance(a, _Slice):
        return a.spec()
    nd = a.ndim
    return pl.BlockSpec(a.shape, lambda *_: (0,) * nd)


def _silu(x):
    return x * jax.nn.sigmoid(x)


def _norm_mod(x, g, shift, scale):
    ms = jnp.mean(x * x, axis=-1, keepdims=True)
    return x * lax.rsqrt(ms + EPS) * g * (1.0 + scale) + shift


def _dot(a, b):
    return jnp.dot(a, b, preferred_element_type=F32)


def _dot_nt(a, b):
    return lax.dot_general(a, b, (((1,), (1,)), ((), ())), preferred_element_type=F32)


def _ada_kernel(c_ref, w_ref, b_ref, o_ref):
    c = c_ref[...]
    o_ref[0] = _dot(_silu(c), w_ref[0]) + b_ref[0]


def ada_modulation(c_all, w_ada, b_ada):
    nl, d, n3 = w_ada.shape
    r = c_all.shape[0]
    tn = n3 // 2
    return pl.pallas_call(
        _ada_kernel,
        grid=(nl, n3 // tn),
        in_specs=[pl.BlockSpec((r, d), lambda l, j: (0, 0)),
                  pl.BlockSpec((1, d, tn), lambda l, j: (l, 0, j)),
                  pl.BlockSpec((1, 1, tn), lambda l, j: (l, 0, j))],
        out_specs=pl.BlockSpec((1, r, tn), lambda l, j: (l, 0, j)),
        out_shape=jax.ShapeDtypeStruct((nl, r, n3), F32),
        compiler_params=_cparams("parallel", "parallel"),
        name="ada_modulation",
    )(c_all, w_ada, b_ada)


def _row_spec(tm, width):
    return pl.BlockSpec((tm, width), lambda i: (i, 0))


def _mod_spec(mod, tiles_per_group):
    return mod.spec(tiles_per_group)


def _pw1_kernel(x_ref, g_ref, sh_ref, sc_ref, w_ref, b_ref, u_ref):
    d = x_ref.shape[1]
    h = _norm_mod(x_ref[...], g_ref[...], sh_ref[...], sc_ref[...]).astype(BF16)
    z = _dot(h, w_ref[...]) + b_ref[...]
    u_ref[...] = z[:, :d] * jax.nn.sigmoid(z[:, d:])


def pw1_glu(x, g, shift, scale, w, b, tm, tpg):
    m, d = x.shape
    return pl.pallas_call(
        _pw1_kernel,
        grid=(m // tm,),
        in_specs=[_row_spec(tm, d), _full_spec(g), _mod_spec(shift, tpg), _mod_spec(scale, tpg),
                  _full_spec(w), _full_spec(b)],
        out_specs=_row_spec(tm, d),
        out_shape=jax.ShapeDtypeStruct((m, d), F32),
        compiler_params=_cparams("parallel"),
        name="pw1_glu",
    )(*map(_arr, (x, g, shift, scale, w, b)))


def _conv_tail(rows_at, wdw_ref, bdw_ref, lg_ref, lb_ref, w2_ref, b2_ref):
    acc = None
    for k in range(CONV_W):
        term = rows_at(k + CONV_PAD - CONV_W + 1) * wdw_ref[k:k + 1, :]
        acc = term if acc is None else acc + term
    acc = acc + bdw_ref[...]
    mu = jnp.mean(acc, axis=-1, keepdims=True)
    cen = acc - mu
    var = jnp.mean(cen * cen, axis=-1, keepdims=True)
    yn = cen * lax.rsqrt(var + EPS) * lg_ref[...] + lb_ref[...]
    return _dot(_silu(yn).astype(BF16), w2_ref[...]) + b2_ref[...]


def _conv_kernel(halo_ref, u_ref, wdw_ref, bdw_ref, lg_ref, lb_ref, w2_ref, b2_ref, x_ref, gate_ref,
                 o_ref, buf, shifted, *, tiles_per_seq):
    tm = x_ref.shape[0]
    first = pl.program_id(0) % tiles_per_seq == 0
    buf[0:CONV_PAD, :] = jnp.where(first, 0.0, halo_ref[0])
    buf[CONV_PAD:CONV_PAD + tm, :] = u_ref[0]
    span = tm + CONV_PAD - SUBLANES
    for s in range(1, SUBLANES):
        shifted[s - 1] = buf[pl.ds(s, span), :]

    def rows_at(o):
        a, s = divmod(o, SUBLANES)
        if s == 0:
            return buf[pl.ds(o, tm), :]
        return shifted[s - 1, pl.ds(a * SUBLANES, tm), :]

    out = _conv_tail(rows_at, wdw_ref, bdw_ref, lg_ref, lb_ref, w2_ref, b2_ref)
    o_ref[...] = x_ref[...] + gate_ref[...] * out


def conv_pw2_residual(u, wdw, bdw, lg, lb, w2, b2, x, gate, tm, tpg):
    m, d = x.shape
    bsz, t, _ = u.shape
    assert t % tm == 0 and tm % CONV_PAD == 0 and m == bsz * t
    tps = t // tm
    return pl.pallas_call(
        functools.partial(_conv_kernel, tiles_per_seq=tps),
        grid=(m // tm,),
        in_specs=[pl.BlockSpec((1, CONV_PAD, d),
                               lambda i: (i // tps, jnp.maximum((i % tps) * (tm // CONV_PAD) - 1, 0), 0)),
                  pl.BlockSpec((1, tm, d), lambda i: (i // tps, i % tps, 0)),
                  _full_spec(wdw), _full_spec(bdw), _full_spec(lg), _full_spec(lb),
                  _full_spec(w2), _full_spec(b2), _row_spec(tm, d), _mod_spec(gate, tpg)],
        out_specs=_row_spec(tm, d),
        out_shape=jax.ShapeDtypeStruct((m, d), F32),
        scratch_shapes=[pltpu.VMEM((tm + CONV_PAD, d), F32),
                        pltpu.VMEM((SUBLANES - 1, tm + CONV_PAD - SUBLANES, d), F32)],
        compiler_params=_cparams("parallel"),
        name="conv_pw2_residual",
    )(*map(_arr, (u, u, wdw, bdw, lg, lb, w2, b2, x, gate)))


def _conv_small_kernel(full_ref, wdw_ref, bdw_ref, lg_ref, lb_ref, w2_ref, b2_ref, x_ref, gate_ref,
                       o_ref):
    tm = x_ref.shape[1]
    rows_at = lambda o: full_ref[0, pl.ds(o, tm), :]
    out = _conv_tail(rows_at, wdw_ref, bdw_ref, lg_ref, lb_ref, w2_ref, b2_ref)
    o_ref[0] = x_ref[0] + gate_ref[0] * out


def conv_pw2_residual_small(full, wdw, bdw, lg, lb, w2, b2, x, gate):
    bsz, ts, d = x.shape
    seq = lambda a: pl.BlockSpec((1,) + a.shape[1:], lambda i: (i, 0, 0))
    return pl.pallas_call(
        _conv_small_kernel,
        grid=(bsz,),
        in_specs=[seq(full), _full_spec(wdw), _full_spec(bdw), _full_spec(lg), _full_spec(lb),
                  _full_spec(w2), _full_spec(b2), seq(x), seq(gate)],
        out_specs=seq(x),
        out_shape=jax.ShapeDtypeStruct(x.shape, F32),
        compiler_params=_cparams("parallel"),
        name="conv_pw2_residual_small",
    )(*map(_arr, (full, wdw, bdw, lg, lb, w2, b2, x, gate)))


def _ffn_kernel(x_ref, *refs, attn, final):
    x = x_ref[...]
    if attn:
        att_ref, wo_ref, ga_ref = refs[:3]
        refs = refs[3:]
        x = x + ga_ref[...] * _dot(att_ref[...], wo_ref[...])
    g_ref, sh_ref, sc_ref, gate_ref, wg_ref, wu_ref, wd_ref = refs[:7]
    h = _norm_mod(x, g_ref[...], sh_ref[...], sc_ref[...]).astype(BF16)
    a = _dot(h, wg_ref[...])
    b = _dot(h, wu_ref[...])
    act = (_silu(a) * b).astype(BF16)
    xn = x + gate_ref[...] * _dot(act, wd_ref[...])
    if final:
        fg_ref, o_ref = refs[7:]
        ms = jnp.mean(xn * xn, axis=-1, keepdims=True)
        o_ref[...] = xn * lax.rsqrt(ms + EPS) * fg_ref[...]
    else:
        (o_ref,) = refs[7:]
        o_ref[...] = xn


def ffn(x, g, shift, scale, gate, wg, wu, wd, tm, tpg, attn=None, final_g=None):
    m, d = x.shape
    final = final_g is not None
    ins = [x]
    specs = [_row_spec(tm, d)]
    if attn is not None:
        o, w_o, gate_a = attn
        ins += [o, w_o, gate_a]
        specs += [_row_spec(tm, o.shape[1]), _full_spec(w_o), _mod_spec(gate_a, tpg)]
    ins += [g, shift, scale, gate, wg, wu, wd]
    specs += [_full_spec(g), _mod_spec(shift, tpg), _mod_spec(scale, tpg), _mod_spec(gate, tpg),
              _full_spec(wg), _full_spec(wu), _full_spec(wd)]
    if final:
        ins.append(final_g)
        specs.append(_full_spec(final_g))
    return pl.pallas_call(
        functools.partial(_ffn_kernel, attn=attn is not None, final=final),
        grid=(m // tm,),
        in_specs=specs,
        out_specs=_row_spec(tm, d),
        out_shape=jax.ShapeDtypeStruct((m, d), F32),
        compiler_params=_cparams("parallel"),
        name="ffn_final" if final else "ffn",
    )(*map(_arr, ins))


def _kvproj_kernel(x_ref, g_ref, w_ref, o_ref):
    x = x_ref[...]
    ms = jnp.mean(x * x, axis=-1, keepdims=True)
    h = (x * lax.rsqrt(ms + EPS) * g_ref[...]).astype(BF16)
    o_ref[...] = _dot(h, w_ref[...])


def kv_proj(x, g, w, tm):
    m, d = x.shape
    n = w.shape[1]
    return pl.pallas_call(
        _kvproj_kernel,
        grid=(m // tm,),
        in_specs=[_row_spec(tm, d), _full_spec(g), _full_spec(w)],
        out_specs=_row_spec(tm, n),
        out_shape=jax.ShapeDtypeStruct((m, n), F32),
        compiler_params=_cparams("parallel"),
        name="kv_proj",
    )(x, g, w)


def _qg_kernel(x_ref, g_ref, sh_ref, sc_ref, w_ref, q_ref, gates_ref):
    nq = q_ref.shape[1]
    h = _norm_mod(x_ref[...], g_ref[...], sh_ref[...], sc_ref[...]).astype(BF16)
    z = _dot(h, w_ref[...])
    q_ref[...] = (z[:, :nq] * (HEAD_DIM ** -0.5)).astype(BF16)
    gates_ref[...] = jax.nn.sigmoid(z[:, nq:])


def qg_proj(x, g, shift, scale, w, tm, tpg):
    m, d = x.shape
    nq = N_HEADS * HEAD_DIM
    ng = w.shape[1] - nq
    return pl.pallas_call(
        _qg_kernel,
        grid=(m // tm,),
        in_specs=[_row_spec(tm, d), _full_spec(g), _mod_spec(shift, tpg), _mod_spec(scale, tpg),
                  _full_spec(w)],
        out_specs=[_row_spec(tm, nq), _row_spec(tm, ng)],
        out_shape=[jax.ShapeDtypeStruct((m, nq), BF16), jax.ShapeDtypeStruct((m, ng), F32)],
        compiler_params=_cparams("parallel"),
        name="qg_proj",
    )(*map(_arr, (x, g, shift, scale, w)))


SLAB = LANES
HEADS_PER_SLAB = SLAB // HEAD_DIM


R_PER_DOT = 2
CMP_PITCH = 40


def _compress_core(rows_at, pe_row, w1_at, b1, w2, b2):
    hidden = w2.shape[0]
    acc = None
    for i in range(CMP_BLOCK // R_PER_DOT):
        parts = [(rows_at(R_PER_DOT * i + j) + pe_row(R_PER_DOT * i + j)).astype(BF16) for j in range(R_PER_DOT)]
        term = _dot(jnp.concatenate(parts, axis=1), w1_at(i))
        acc = term if acc is None else acc + term
    hid = _silu(acc + b1)
    return [_dot(hid[:, hh * hidden:(hh + 1) * hidden].astype(BF16), w2) + b2 for hh in range(HEADS_PER_SLAB)]


def _compress_kernel(x_ref, pe_ref, w1_ref, b1_ref, w2_ref, b2_ref, o_ref, *, nblk):
    outs = _compress_core(lambda r: x_ref[0, pl.ds(r, nblk, stride=CMP_BLOCK), :],
                          lambda r: pe_ref[0, r:r + 1, :], lambda i: w1_ref[0, i],
                          b1_ref[0], w2_ref[0], b2_ref[0])
    for hh, out in enumerate(outs):
        o_ref[0, 0, hh] = out


def compress(x, nblk, pe_slab, w1_bd, b1_slab, w2, b2):
    bc, rows, _ = x.shape
    n_slab = 2 * KV_HEADS // HEADS_PER_SLAB
    slabs_per_e = n_slab // 2
    per_e = lambda a: pl.BlockSpec((1,) + a.shape[1:], lambda b, s: (s // slabs_per_e,) + (0,) * (a.ndim - 1))
    return pl.pallas_call(
        functools.partial(_compress_kernel, nblk=nblk),
        grid=(bc, n_slab),
        in_specs=[pl.BlockSpec((1, rows, SLAB), lambda b, s: (b, 0, s)),
                  pl.BlockSpec((1, CMP_BLOCK, SLAB), lambda b, s: (s, 0, 0)),
                  per_e(w1_bd), per_e(b1_slab), per_e(w2), per_e(b2)],
        out_specs=pl.BlockSpec((1, 1, HEADS_PER_SLAB, nblk, HEAD_DIM),
                               lambda b, s: (b, s // slabs_per_e, s % slabs_per_e, 0, 0)),
        out_shape=jax.ShapeDtypeStruct((bc, 2, KV_HEADS, nblk, HEAD_DIM), F32),
        compiler_params=_cparams("parallel", "parallel"),
        name="compress",
    )(x, pe_slab, w1_bd, b1_slab, w2, b2)


def _compress_paged_kernel(pt_ref, *refs, n_pages):
    page_refs = refs[:n_pages]
    pe_ref, w1_ref, b1_ref, w2_ref, b2_ref, o_ref, scr = refs[n_pages:]
    blocks_per_page = page_refs[0].shape[1] // CMP_BLOCK
    nblk = n_pages * blocks_per_page
    n_slab = pe_ref.shape[0]
    slabs_per_e = n_slab // 2
    for s in range(n_slab):
        e = s // slabs_per_e
        for p in range(n_pages):
            for j in range(blocks_per_page):
                base = (p * blocks_per_page + j) * CMP_PITCH
                scr[base:base + CMP_BLOCK, :] = page_refs[p][0, j * CMP_BLOCK:(j + 1) * CMP_BLOCK,
                                                             s * SLAB:(s + 1) * SLAB]
        outs = _compress_core(lambda r: scr[pl.ds(r, nblk, stride=CMP_PITCH), :],
                              lambda r: pe_ref[s, r:r + 1, :], lambda i: w1_ref[e, i],
                              b1_ref[e], w2_ref[e], b2_ref[e])
        for hh, out in enumerate(outs):
            o_ref[0, e, (s % slabs_per_e) * HEADS_PER_SLAB + hh] = out


def compress_paged(cache, page_table, pe_slab, w1_bd, b1_slab, w2, b2):
    n_pool, page, c = cache.shape
    db, n_pages = page_table.shape
    assert page % CMP_BLOCK == 0
    nblk = n_pages * page // CMP_BLOCK
    const = lambda a: pl.BlockSpec(a.shape, lambda b, pt: (0,) * a.ndim)

    def page_spec(k):
        return pl.BlockSpec((1, page, c), lambda b, pt: (pt[b, k], 0, 0))

    return pl.pallas_call(
        functools.partial(_compress_paged_kernel, n_pages=n_pages),
        grid_spec=pltpu.PrefetchScalarGridSpec(
            num_scalar_prefetch=1,
            grid=(db,),
            in_specs=[page_spec(k) for k in range(n_pages)]
            + [const(pe_slab), const(w1_bd), const(b1_slab), const(w2), const(b2)],
            out_specs=pl.BlockSpec((1, 2, KV_HEADS, nblk, HEAD_DIM), lambda b, pt: (b, 0, 0, 0, 0)),
            scratch_shapes=[pltpu.VMEM((nblk * CMP_PITCH, SLAB), F32)],
        ),
        out_shape=jax.ShapeDtypeStruct((db, 2, KV_HEADS, nblk, HEAD_DIM), F32),
        compiler_params=_cparams("parallel"),
        name="compress_paged",
    )(page_table, *([cache] * n_pages), pe_slab, w1_bd, b1_slab, w2, b2)


def _rel_bucket(dist):
    n = jnp.maximum(dist, 0)
    exact = N_BUCKETS // 2
    nf = jnp.maximum(n, 1).astype(F32)
    large = exact + (jnp.log(nf / exact) / math.log(MAX_DISTANCE / exact) * (N_BUCKETS - exact)).astype(jnp.int32)
    large = jnp.minimum(large, N_BUCKETS - 1)
    return jnp.where(n < exact, n, large)


class _BiasTable:
    def __init__(self, table, max_dist, pad):
        tab = table.astype(F32)

        def lookup(dist):
            bucket = _rel_bucket(dist)
            out = jnp.zeros((tab.shape[1], dist.shape[0]), F32)
            for k in range(N_BUCKETS):
                out = jnp.where(bucket[None, :] == k, tab[k][:, None], out)
            return out

        self.pad = pad
        self.n = max_dist
        self.ext = lookup(jnp.arange(-pad, max_dist))
        self.rev = lookup(max_dist - 1 - jnp.arange(pad + max_dist))

    def at(self, d):
        return self.ext[:, self.pad + d]

    def rising(self, d0, length):
        return lax.slice_in_dim(self.ext, self.pad + d0, self.pad + d0 + length, axis=1)

    def falling(self, d0, length, step=1):
        start = self.n - 1 - d0
        return lax.slice(self.rev, (0, start), (self.rev.shape[0], start + step * (length - 1) + 1), (1, step))

    def frames(self, d0, count, hop, length):
        reps = length // hop
        seq = self.rising(d0, hop * (count + reps - 1)).reshape(-1, count + reps - 1, hop)
        return jnp.concatenate([seq[:, a:a + count] for a in range(reps)], axis=-1)

    def toeplitz(self, d0, count, hop, size):
        assert hop == size
        period = 2 * size - 1
        base = self.rising(d0 - (size - 1), hop * (count + 1)).reshape(-1, count + 1, hop)
        w = jnp.concatenate([base[:, :count, size - 1:], base[:, 1:, :size - 1],
                             base[:, :count, :size - 1]], axis=-1)
        rows = jnp.tile(w, (1, 1, size))[:, :, :size * (period - 1)]
        return rows.reshape(w.shape[0], count, size, period - 1)[..., :size]


def _softmax_step(s, vt, m, l, acc):
    m_new = jnp.maximum(m, jnp.max(s, axis=0, keepdims=True))
    alpha = jnp.exp(m - m_new)
    p = jnp.exp(s - m_new)
    l = alpha * l + jnp.sum(p, axis=0, keepdims=True)
    acc = alpha * acc + _dot(vt, p.astype(BF16))
    return m_new, l, acc


def _attn_kernel(bfar_ref, q_ref, gt_ref, kaug_ref, vst_ref, kw_ref, vwt_ref, ck_ref, cvt_ref,
                 bc_ref, bn_ref, o_ref, imp_ref, sc_ref, s_ref, *, nb, nc):
    g = pl.program_id(0)
    qt = pl.program_id(2)
    qs = qt * QT
    gq = GROUP * QT

    lane_cat = lambda parts: jnp.concatenate(parts, axis=1)
    q_t = q_ref[0].astype(F32).T.astype(BF16)
    q_heads = [q_t[h * HEAD_DIM:(h + 1) * HEAD_DIM] for h in range(GROUP)]
    q_plain = lane_cat(q_heads)

    col_i = lax.broadcasted_iota(jnp.int32, (1, gq), 1) % QT

    s_c = _dot(ck_ref[0, 0], q_plain) + lane_cat([bc_ref[h, 0] for h in range(GROUP)])
    n_idx = lax.broadcasted_iota(jnp.int32, (nc, gq), 0)
    valid_c = (n_idx * CMP_BLOCK + CMP_BLOCK - 1) <= (qs + col_i)
    s_c = jnp.where(valid_c, s_c, M_INIT)
    m_c = jnp.max(s_c, axis=0, keepdims=True)
    p_c = jnp.where(valid_c, jnp.exp(s_c - m_c), 0.0)
    l_c = jnp.sum(p_c, axis=0, keepdims=True)
    p_c = p_c * jnp.where(l_c > 0.0, 1.0 / l_c, 0.0)
    o_c = _dot(cvt_ref[0, 0], p_c.astype(BF16))

    imp = p_c[:, 0:QT]
    for h in range(1, GROUP):
        imp = imp + p_c[:, h * QT:(h + 1) * QT]
    imp_ref[...] = imp
    ratio = SEL_BLOCK // CMP_BLOCK
    imp2 = imp_ref[pl.ds(0, nb, stride=ratio), :]
    for r in range(1, ratio):
        imp2 = imp2 + imp_ref[pl.ds(r, nb, stride=ratio), :]
    j_idx = lax.broadcasted_iota(jnp.int32, (nb, QT), 0)
    i_idx = lax.broadcasted_iota(jnp.int32, (nb, QT), 1)
    jq = (qs + i_idx) // SEL_BLOCK
    forced = (j_idx == 0) | (j_idx == jq) | (j_idx == jq - 1)
    score = jnp.where(j_idx <= jq, jnp.where(forced, FORCE, imp2), -1.0)
    sc_ref[...] = score

    sub8 = lax.broadcasted_iota(jnp.int32, (8, QT), 0)
    groups = [score[8 * r:8 * r + 8] for r in range(nb // 8)]
    counts = [jnp.zeros((8, QT), F32) for _ in groups]
    for c in range(nb):
        row = sc_ref[c:c + 1, :]
        for r, grp_score in enumerate(groups):
            if r > c // 8:
                beats = row >= grp_score
            elif r < c // 8:
                beats = row > grp_score
            else:
                tie = jnp.where(row == grp_score, 1.0, 0.0) * jnp.where(sub8 > c % 8, 1.0, 0.0)
                counts[r] = counts[r] + tie
                beats = row > grp_score
            counts[r] = counts[r] + jnp.where(beats, 1.0, 0.0)
    cnt = jnp.concatenate(counts, axis=0)
    selected = (cnt < float(min(N_SEL, nb))) & (score >= 0.0)
    mask_feat = jnp.where(selected, 0.0, MASK_NEG).astype(BF16)
    q_aug = lane_cat([jnp.concatenate([qh, mask_feat], axis=0) for qh in q_heads])

    far_row = lane_cat([jnp.full((1, QT), bfar_ref[g * GROUP + h], F32) for h in range(GROUP)])

    def near_bias(kt0, n_sub):
        parts = []
        for sub in range(n_sub):
            dm = jnp.clip(qt - (kt0 + sub), 0, NEAR_SPAN - 1)
            parts.append(lane_cat([bn_ref[h, dm] for h in range(GROUP)]))
        return jnp.concatenate(parts, axis=0)

    init = (jnp.full((1, gq), M_INIT, F32), jnp.zeros((1, gq), F32), jnp.zeros((HEAD_DIM, gq), F32))
    sub_per_chunk = KEY_STEP // QT
    row_k = lax.broadcasted_iota(jnp.int32, (KEY_STEP, gq), 0)

    last_chunk = qt // sub_per_chunk

    def raw_scores(c):
        ks = pl.multiple_of(jnp.minimum(c, last_chunk) * KEY_STEP, KEY_STEP)
        return _dot(kaug_ref[0, 0, pl.ds(ks, KEY_STEP), :], q_aug)

    def pipelined(finish):
        def body(c, carry):
            cur = s_ref[c % 2]
            s_ref[(c + 1) % 2] = raw_scores(c + 1)
            ks = pl.multiple_of(c * KEY_STEP, KEY_STEP)
            return _softmax_step(finish(c, cur, ks), vst_ref[0, 0, :, pl.ds(ks, KEY_STEP)], *carry)
        return body

    def far_finish(c, s, ks):
        return s + far_row

    def near_finish(c, s, ks):
        return s + near_bias(c * sub_per_chunk, sub_per_chunk)

    wt0 = jnp.maximum(qt - WINDOW // QT, 0)
    ws = pl.multiple_of(wt0 * QT, QT)
    wk = WINDOW + QT
    s_w = _dot(kw_ref[0, 0, pl.ds(ws, wk), :], q_plain) + near_bias(wt0, wk // QT)
    dist = (qs + col_i) - (ws + lax.broadcasted_iota(jnp.int32, (wk, gq), 0))
    s_w = jnp.where((dist >= 0) & (dist < WINDOW), s_w, MASK_NEG)
    p_w = jnp.exp(s_w - jnp.max(s_w, axis=0, keepdims=True))
    l_w = jnp.sum(p_w, axis=0, keepdims=True)
    o_w = _dot(vwt_ref[0, 0, :, pl.ds(ws, wk)], p_w.astype(BF16)) / l_w

    s_ref[0] = raw_scores(0)
    n_far = jnp.maximum(qt - (NEAR_TILES - 1), 0) // sub_per_chunk
    carry = lax.fori_loop(0, n_far, pipelined(far_finish), init)
    carry = lax.fori_loop(n_far, last_chunk, pipelined(near_finish), carry)
    ks = pl.multiple_of(last_chunk * KEY_STEP, KEY_STEP)
    s = near_finish(last_chunk, s_ref[last_chunk % 2], ks)
    s = jnp.where(ks + row_k <= qs + col_i, s, MASK_NEG)
    m_s, l_s, acc_s = _softmax_step(s, vst_ref[0, 0, :, pl.ds(ks, KEY_STEP)], *carry)
    o_s = acc_s / l_s

    gate = lambda br: lane_cat([gt_ref[0, 0, h * 3 + br:h * 3 + br + 1, :] for h in range(GROUP)])
    o_t = gate(0) * o_c + gate(1) * o_s + gate(2) * o_w
    o_rows = jnp.concatenate([o_t[:, h * QT:(h + 1) * QT] for h in range(GROUP)], axis=0)
    o_ref[0] = o_rows.T.astype(BF16)


def prompt_attention(q, gates_t, kaug, vst, kwin, vwt, ck, cvt, bias_cmp, bias_near, bias_far):
    bsz, t, dq = q.shape
    nb = t // SEL_BLOCK
    nc = t // CMP_BLOCK
    nq = t // QT
    gd = GROUP * HEAD_DIM
    per_bg = lambda a: pl.BlockSpec((1, 1) + a.shape[2:], lambda g, b, i, *_: (b, g, 0, 0))
    grid_spec = pltpu.PrefetchScalarGridSpec(
        num_scalar_prefetch=1,
        grid=(KV_HEADS, bsz, nq),
        in_specs=[pl.BlockSpec((1, QT, gd), lambda g, b, i, *_: (b, i, g)),
                  pl.BlockSpec((1, 1, 3 * GROUP, QT), lambda g, b, i, *_: (b, g, 0, i)),
                  per_bg(kaug), per_bg(vst), per_bg(kwin), per_bg(vwt), per_bg(ck), per_bg(cvt),
                  pl.BlockSpec((GROUP, 1, nc, QT), lambda g, b, i, *_: (g, i, 0, 0)),
                  pl.BlockSpec((GROUP, NEAR_SPAN, QT, QT), lambda g, b, i, *_: (g, 0, 0, 0))],
        out_specs=pl.BlockSpec((1, QT, gd), lambda g, b, i, *_: (b, i, g)),
        scratch_shapes=[pltpu.VMEM((nc, QT), F32), pltpu.VMEM((nb, QT), F32),
                        pltpu.VMEM((2, KEY_STEP, GROUP * QT), F32)],
    )
    return pl.pallas_call(
        functools.partial(_attn_kernel, nb=nb, nc=nc),
        grid_spec=grid_spec,
        out_shape=jax.ShapeDtypeStruct((bsz, t, dq), BF16),
        compiler_params=_cparams("parallel", "parallel", "parallel"),
        name="prompt_attention",
    )(bias_far, q, gates_t, kaug, vst, kwin, vwt, ck, cvt, bias_cmp, bias_near)


KEY_CHUNK = 1024
NEW_PAD = LANES


def _diag_heads(full, row_g):
    out = None
    for g in range(KV_HEADS):
        part = jnp.where(row_g == g, full[:, g * HEAD_DIM:(g + 1) * HEAD_DIM], 0.0)
        out = part if out is None else out + part
    return out


def _sample_attn_kernel(pt_ref, q_ref, gates_ref, *refs, past, ts, n_pages):
    page_refs = refs[:n_pages]
    (nsel_ref, kwin_ref, nwin_ref, ck_ref, cv_ref, bsel_ref, bwin_ref, bcmp_ref, e_ref, o_ref,
     s_scr) = refs[n_pages:]
    kvd = KV_HEADS * HEAD_DIM
    page = page_refs[0].shape[1]
    pages_per_chunk = KEY_CHUNK // page

    def cached(c, lo, hi):
        parts = [page_refs[c * pages_per_chunk + j][0, :, lo:hi] for j in range(pages_per_chunk)]
        return jnp.concatenate(parts, axis=0).astype(BF16)

    rows = GROUP * KV_HEADS * ts
    nc = ck_ref.shape[1]
    q = q_ref[0]
    r_idx = lax.broadcasted_iota(jnp.int32, (rows, 1), 0)
    row_q = r_idx % ts
    row_g = (r_idx // ts) % KV_HEADS

    s_c = _dot_nt(q, ck_ref[0]) + bcmp_ref[...]
    p_c = jnp.exp(s_c - jnp.max(s_c, axis=1, keepdims=True))
    p_c = p_c / jnp.sum(p_c, axis=1, keepdims=True)
    o_c = _diag_heads(_dot(p_c.astype(BF16), cv_ref[0]), row_g)

    per = KV_HEADS * ts
    imp = p_c[0:per]
    for h in range(1, GROUP):
        imp = imp + p_c[h * per:(h + 1) * per]
    ratio = SEL_BLOCK // CMP_BLOCK
    imp2 = imp
    for r in range(1, ratio):
        imp2 = imp2 + pltpu.roll(imp, nc - r, axis=1)
    lane = lax.broadcasted_iota(jnp.int32, (per, nc), 1)
    blk = lane // ratio
    jq = past // SEL_BLOCK
    forced = (blk == 0) | (blk == jq - 1)
    score = jnp.where(lane % ratio == 0, jnp.where(forced, FORCE, imp2), -2.0)
    cnt = jnp.where(score < FORCE, 1.0, 0.0)
    for c in range(nc // ratio):
        col = score[:, c * ratio:c * ratio + 1]
        ge = jnp.where(col >= score, 1.0, 0.0)
        gt = jnp.where(col > score, 1.0, 0.0)
        cnt = cnt + jnp.where(lane > c * ratio, ge, gt)
    selected = (cnt < float(min(N_SEL, jq + 1))) & (score >= 0.0)
    mask_feat = jnp.where(selected, 0.0, MASK_NEG).astype(BF16)
    mask_feat = jnp.concatenate([mask_feat] * GROUP, axis=0)

    for c in range(past // KEY_CHUNK):
        sl = slice(c * KEY_CHUNK, (c + 1) * KEY_CHUNK)
        s_scr[:, sl] = _dot_nt(q, cached(c, 0, kvd)) + bsel_ref[:, sl] + _dot(mask_feat, e_ref[:, sl])
    new_j = lax.broadcasted_iota(jnp.int32, (rows, NEW_PAD), 1)
    new_ok = new_j <= row_q
    s_new = _dot_nt(q, nsel_ref[0, :, 0:kvd].astype(BF16)) + bsel_ref[:, past:past + NEW_PAD]
    s_scr[:, past:past + NEW_PAD] = jnp.where(new_ok, s_new, MASK_NEG)
    s_all = s_scr[...]
    p_s = jnp.exp(s_all - jnp.max(s_all, axis=1, keepdims=True))
    l_s = jnp.sum(p_s, axis=1, keepdims=True)
    s_scr[...] = p_s
    acc = _dot(s_scr[:, past:past + NEW_PAD].astype(BF16), nsel_ref[0, :, kvd:2 * kvd].astype(BF16))
    for c in range(past // KEY_CHUNK):
        sl = slice(c * KEY_CHUNK, (c + 1) * KEY_CHUNK)
        acc = acc + _dot(s_scr[:, sl].astype(BF16), cached(c, kvd, 2 * kvd))
    o_s = _diag_heads(acc, row_g) / l_s

    wlen = kwin_ref.shape[1]
    win_i = lax.broadcasted_iota(jnp.int32, (rows, wlen), 1)
    s_w1 = _dot_nt(q, kwin_ref[0, :, 0:kvd].astype(BF16)) + bwin_ref[:, 0:wlen]
    s_w1 = jnp.where(win_i + (WINDOW - wlen) > row_q, s_w1, MASK_NEG)
    s_w2 = _dot_nt(q, nwin_ref[0, :, 0:kvd].astype(BF16)) + bwin_ref[:, wlen:wlen + NEW_PAD]
    s_w2 = jnp.where(new_ok, s_w2, MASK_NEG)
    m_w = jnp.maximum(jnp.max(s_w1, axis=1, keepdims=True), jnp.max(s_w2, axis=1, keepdims=True))
    p_w1 = jnp.exp(s_w1 - m_w)
    p_w2 = jnp.exp(s_w2 - m_w)
    l_w = jnp.sum(p_w1, axis=1, keepdims=True) + jnp.sum(p_w2, axis=1, keepdims=True)
    acc_w = (_dot(p_w1.astype(BF16), kwin_ref[0, :, kvd:2 * kvd].astype(BF16))
             + _dot(p_w2.astype(BF16), nwin_ref[0, :, kvd:2 * kvd].astype(BF16)))
    o_w = _diag_heads(acc_w, row_g) / l_w

    gts = gates_ref[0]
    o_ref[0] = gts[:, 0:1] * o_c + gts[:, 1:2] * o_s + gts[:, 2:3] * o_w


def sample_attention(page_table, q_bd, gates, cache, nsel, kwin, nwin, ck, cv, bsel, bwin, bcmp, e_mat, ts):
    db, rows, kvd = q_bd.shape
    n_pages = page_table.shape[1]
    page = cache.shape[1]
    past = n_pages * page
    assert KEY_CHUNK % page == 0 and past % KEY_CHUNK == 0
    seq = lambda a: pl.BlockSpec((1,) + a.shape[1:], lambda b, pt: (b,) + (0,) * (a.ndim - 1))
    const = lambda a: pl.BlockSpec(a.shape, lambda b, pt: (0,) * a.ndim)

    def page_spec(k):
        return pl.BlockSpec((1,) + cache.shape[1:], lambda b, pt: (pt[b, k], 0, 0))

    return pl.pallas_call(
        functools.partial(_sample_attn_kernel, past=past, ts=ts, n_pages=n_pages),
        grid_spec=pltpu.PrefetchScalarGridSpec(
            num_scalar_prefetch=1,
            grid=(db,),
            in_specs=[seq(q_bd), seq(gates)] + [page_spec(k) for k in range(n_pages)]
            + [seq(nsel), seq(kwin), seq(nwin), seq(ck), seq(cv), const(bsel), const(bwin), const(bcmp),
               const(e_mat)],
            out_specs=pl.BlockSpec((1, rows, HEAD_DIM), lambda b, pt: (b, 0, 0)),
            scratch_shapes=[pltpu.VMEM((rows, past + NEW_PAD), F32)],
        ),
        out_shape=jax.ShapeDtypeStruct((db, rows, HEAD_DIM), F32),
        compiler_params=_cparams("parallel"),
        name="sample_attention",
    )(page_table, q_bd, gates, *([cache] * n_pages), nsel, kwin, nwin, ck, cv, bsel, bwin, bcmp, e_mat)


def _prep_weights(P):
    W = {}
    cast = lambda a: a.astype(BF16)
    for k in ("w_pw1", "w_pw2", "w_kv", "w_o", "w_gate", "w_up", "w_down", "w_phi1", "w_phi2"):
        W[k] = cast(P[k])
    nq = N_HEADS * HEAD_DIM
    wqg = P["w_qg"]
    pad = LANES - (wqg.shape[2] - nq)
    W["w_qg"] = cast(jnp.pad(wqg, ((0, 0), (0, 0), (0, pad))))
    rows = lambda a: a.reshape(-1, 1, a.shape[-1])
    for k in ("norm_g", "b_pw1", "b_dw", "ln_g", "ln_b", "b_pw2"):
        W[k] = rows(P[k])
    W["w_dw"] = jnp.pad(P["w_dw"], ((0, 0), (0, CONV_PAD - CONV_W), (0, 0)))
    return W


def _trunk(x, mods, is_prompt, caches, P, W, bias_dist):
    bsz, t, d = x.shape
    m = bsz * t
    depth = P["w_gate"].shape[0]
    n_a = P["w_pw1"].shape[0]
    if is_prompt:
        tm, tpg = PROMPT_ROW_TILE, t // PROMPT_ROW_TILE
    else:
        tm, tpg = m, 1
    x2 = x.reshape(m, d)
    row2 = lambda a: a.reshape(1, -1)
    conv_states = []
    y = None
    for l in range(depth):
        shift, scale, gate = mods[(l, 0)][:3]
        g0 = _layer(W["norm_g"], 2 * l)
        attn = None
        if l < n_a:
            u = pw1_glu(x2, g0, shift, scale, _layer(W["w_pw1"], l), _layer(W["b_pw1"], l), tm, tpg)
            u3 = u.reshape(bsz, t, d)
            args = tuple(_layer(W[k], l) for k in ("w_dw", "b_dw", "ln_g", "ln_b", "w_pw2", "b_pw2"))
            if is_prompt:
                conv_states.append(u3[:, -(CONV_W - 1):])
                x2 = conv_pw2_residual(u3, *args, x2, gate, tm, tpg)
            else:
                hist = jnp.pad(caches["state_conv"][l], ((0, 0), (CONV_PAD - CONV_W + 1, 0), (0, 0)))
                full = jnp.concatenate([hist, u3], axis=1)
                conv_states.append(full[:, -(CONV_W - 1):])
                x2 = conv_pw2_residual_small(full, *args, x2.reshape(bsz, t, d),
                                             mods[(l, 0)][3]).reshape(m, d)
        else:
            if l == n_a:
                rows = kv_proj(x2, row2(P["g_kv"]), W["w_kv"], tm)
                rows6 = rows.reshape(bsz, t, 3, 2, KV_HEADS, HEAD_DIM)
                if is_prompt:
                    att = _prompt_attention_setup(rows6, P, W, bias_dist)
                    win_state = rows6[:, -min(WINDOW, t):, 2]
                else:
                    att = _sample_attention_setup(rows6, caches, P, W, bias_dist)
                    wl = caches["cache_kv_win"].shape[1]
                    win_state = jnp.concatenate([caches["cache_kv_win"], rows6[:, :, 2]], axis=1)[:, -wl:]
            lb = l - n_a
            q, gates = qg_proj(x2, g0, shift, scale, _layer(W["w_qg"], lb), tm, tpg)
            if is_prompt:
                o = _prompt_attention_layer(q.reshape(bsz, t, -1), gates.reshape(bsz, t, -1), att)
            else:
                o = _sample_attention_layer(q.reshape(bsz, t, -1), gates.reshape(bsz, t, -1), att)
            attn = (o.reshape(m, -1), _layer(W["w_o"], lb), gate)
        shift, scale, gate = mods[(l, 1)][:3]
        fg = row2(P["final_g"]) if l == depth - 1 else None
        out = ffn(x2, _layer(W["norm_g"], 2 * l + 1), shift, scale, gate, _layer(W["w_gate"], l),
                  _layer(W["w_up"], l), _layer(W["w_down"], l), tm, tpg, attn=attn, final_g=fg)
        if l == depth - 1:
            y = out
        else:
            x2 = out
    return (y.reshape(bsz, t, d), rows6[:, :, 0], rows6[:, :, 1], win_state, jnp.stack(conv_states))


def _compress_weights(P, W):
    pe = P["pe_cmp"]
    pe_slab = jnp.repeat(jnp.tile(pe, (1, 1, HEADS_PER_SLAB)), KV_HEADS // HEADS_PER_SLAB, axis=0)
    w1 = W["w_phi1"].reshape(2, CMP_BLOCK, HEAD_DIM, -1)
    zero = jnp.zeros_like(w1)
    w1_bd = jnp.concatenate([jnp.concatenate([w1, zero], axis=-1), jnp.concatenate([zero, w1], axis=-1)],
                            axis=2)
    w1_bd = w1_bd.reshape(2, CMP_BLOCK // R_PER_DOT, R_PER_DOT * SLAB, -1)
    b1_slab = jnp.tile(P["b_phi1"], (1, HEADS_PER_SLAB))[:, None, :]
    return pe_slab, w1_bd, b1_slab, W["w_phi2"], P["b_phi2"][:, None, :]


def _prompt_attention_setup(rows6, P, W, bias_dist):
    bsz, t = rows6.shape[:2]
    nb = t // SEL_BLOCK
    nc = t // CMP_BLOCK
    nq = t // QT
    cmp = compress(rows6.reshape(bsz, t, -1), nc, *_compress_weights(P, W))
    ck = cmp[:, 0].astype(BF16)
    cvt = cmp[:, 1].transpose(0, 1, 3, 2).astype(BF16)
    onehot = (jnp.arange(t)[:, None] // SEL_BLOCK == jnp.arange(nb)[None, :]).astype(BF16)
    ksel = rows6[:, :, 1, 0].transpose(0, 2, 1, 3).astype(BF16)
    kaug = jnp.concatenate([ksel, jnp.broadcast_to(onehot, (bsz, KV_HEADS, t, nb))], axis=-1)
    vst = rows6[:, :, 1, 1].transpose(0, 2, 3, 1).astype(BF16)
    kwin = rows6[:, :, 2, 0].transpose(0, 2, 1, 3).astype(BF16)
    vwt = rows6[:, :, 2, 1].transpose(0, 2, 3, 1).astype(BF16)
    per_qt = QT // CMP_BLOCK
    n_u = nc + per_qt * (nq - 1)
    frames = bias_dist.frames(-(CMP_BLOCK - 1) - CMP_BLOCK * (nc - 1), n_u, CMP_BLOCK, QT)[:, ::-1]
    frames = lax.optimization_barrier(frames)
    bias_cmp = jnp.stack([frames[:, per_qt * (nq - 1 - i):per_qt * (nq - 1 - i) + nc] for i in range(nq)],
                         axis=1)
    bias_near = bias_dist.toeplitz(0, NEAR_SPAN, QT, QT)
    bias_far = bias_dist.at(NEAR_TILES * QT)
    return dict(kaug=kaug, vst=vst, kwin=kwin, vwt=vwt, ck=ck, cvt=cvt, bias_cmp=bias_cmp,
                bias_near=bias_near, bias_far=bias_far)


def _prompt_attention_layer(q, gates, att):
    bsz, t, _ = q.shape
    gates_t = gates[:, :, :3 * N_HEADS].reshape(bsz, t, KV_HEADS, 3 * GROUP).transpose(0, 2, 3, 1)
    return prompt_attention(q, gates_t, att["kaug"], att["vst"], att["kwin"], att["vwt"], att["ck"],
                            att["cvt"], att["bias_cmp"], att["bias_near"], att["bias_far"])


def _sample_attention_setup(rows6, caches, P, W, bias_dist):
    db, ts = rows6.shape[:2]
    pt = caches["page_table"]
    n_pages = pt.shape[1]
    page = caches["cache_kv_cmp"].shape[1]
    past = n_pages * page
    c = 2 * KV_HEADS * HEAD_DIM
    sel_cache = caches["cache_kv_sel"].reshape(-1, page, c)
    nc = past // CMP_BLOCK
    assert (past + ts) // CMP_BLOCK == nc and past % KEY_CHUNK == 0
    cmp = compress_paged(caches["cache_kv_cmp"].reshape(-1, page, c), pt, *_compress_weights(P, W))
    ck = cmp[:, 0].transpose(0, 2, 1, 3).reshape(db, nc, KV_HEADS * HEAD_DIM).astype(BF16)
    cv = cmp[:, 1].transpose(0, 2, 1, 3).reshape(db, nc, KV_HEADS * HEAD_DIM).astype(BF16)
    pad_new = lambda a: jnp.pad(a.reshape(db, ts, c), ((0, 0), (0, NEW_PAD - ts), (0, 0)))
    nsel = pad_new(rows6[:, :, 1])
    nwin = pad_new(rows6[:, :, 2])
    kwin = caches["cache_kv_win"].reshape(db, -1, c)
    wlen = kwin.shape[1]
    rows = GROUP * KV_HEADS * ts
    def by_row(f):
        per_q = jnp.stack([f(q) for q in range(ts)], axis=0)
        per_q = per_q.reshape(ts, KV_HEADS, GROUP, -1).transpose(2, 1, 0, 3)
        return per_q.reshape(rows, -1)

    bsel = by_row(lambda q: bias_dist.falling(past + q, past + NEW_PAD))
    bwin = by_row(lambda q: jnp.concatenate([bias_dist.falling(wlen + q, wlen),
                                             bias_dist.falling(q, NEW_PAD)], axis=1))
    bcmp = by_row(lambda q: bias_dist.falling(past + q - (CMP_BLOCK - 1), nc, step=CMP_BLOCK))
    ratio = SEL_BLOCK // CMP_BLOCK
    lane_blk = np.where(np.arange(nc) % ratio == 0, np.arange(nc) // ratio, -1)
    e_mat = jnp.asarray(lane_blk[:, None] == (np.arange(past) // SEL_BLOCK)[None, :], BF16)
    return dict(page_table=pt, sel_cache=sel_cache, nsel=nsel, kwin=kwin, nwin=nwin, ck=ck, cv=cv, bsel=bsel, bwin=bwin,
                bcmp=bcmp, e_mat=e_mat, past=past)


def _sample_attention_layer(q, gates, att):
    db, ts, _ = q.shape
    rows = GROUP * KV_HEADS * ts
    q5 = q.reshape(db, ts, KV_HEADS, GROUP, HEAD_DIM).transpose(0, 3, 2, 1, 4)
    eye = jnp.eye(KV_HEADS, dtype=q.dtype)
    q_bd = (q5[:, :, :, :, None, :] * eye[None, None, :, None, :, None]).reshape(db, rows, KV_HEADS * HEAD_DIM)
    g5 = gates[:, :, :3 * N_HEADS].reshape(db, ts, KV_HEADS, GROUP, 3).transpose(0, 3, 2, 1, 4)
    g_rows = jnp.pad(g5.reshape(db, rows, 3), ((0, 0), (0, 0), (0, LANES - 3)))
    o = sample_attention(att["page_table"], q_bd, g_rows, att["sel_cache"], att["nsel"], att["kwin"],
                         att["nwin"], att["ck"], att["cv"], att["bsel"], att["bwin"], att["bcmp"],
                         att["e_mat"], ts)
    o = o.reshape(db, GROUP, KV_HEADS, ts, HEAD_DIM).transpose(0, 3, 2, 1, 4)
    return o.reshape(db, ts, N_HEADS * HEAD_DIM).astype(BF16)


def kernel(x_prompt, x_sample, c_prompt, c_sample, cache_kv_cmp, cache_kv_sel, cache_kv_win, state_conv, page_table, w_ada, b_ada, norm_g, w_pw1, b_pw1, w_dw, b_dw, ln_g, ln_b, w_pw2, b_pw2, g_kv, w_kv, w_phi1, b_phi1, w_phi2, b_phi2, pe_cmp, w_qg, w_o, rel_table, w_gate, w_up, w_down, final_g):
    P = dict(w_ada=w_ada, b_ada=b_ada, norm_g=norm_g, w_pw1=w_pw1, b_pw1=b_pw1, w_dw=w_dw, b_dw=b_dw,
             ln_g=ln_g, ln_b=ln_b, w_pw2=w_pw2, b_pw2=b_pw2, g_kv=g_kv, w_kv=w_kv, w_phi1=w_phi1,
             b_phi1=b_phi1, w_phi2=w_phi2, b_phi2=b_phi2, pe_cmp=pe_cmp, w_qg=w_qg, w_o=w_o,
             rel_table=rel_table, w_gate=w_gate, w_up=w_up, w_down=w_down, final_g=final_g)
    W = _prep_weights(P)
    bp, tp, d = x_prompt.shape
    db, ts, _ = x_sample.shape
    depth = w_ada.shape[0]

    n_c = bp + db
    r_pad = -(-n_c // 8) * 8
    c_all = jnp.pad(jnp.concatenate([c_prompt, c_sample], axis=0), ((0, r_pad - n_c), (0, 0)))
    mod = ada_modulation(c_all, w_ada.reshape(depth * 2, d, 3 * d), b_ada.reshape(depth * 2, 1, 3 * d))
    mod_rows = mod.reshape(depth * 2, r_pad, 3, 1, d)
    mod_tok = jnp.repeat(mod[:, bp:n_c], ts, axis=1)
    mods_p, mods_s = {}, {}
    for l in range(depth):
        for j in range(2):
            lj = l * 2 + j
            mods_p[(l, j)] = tuple(
                _Slice(mod_rows, (None, None, None, 1, d), lambda grp, lj=lj, k=k: (lj, grp, k, 0, 0))
                for k in range(3))
            mods_s[(l, j)] = tuple(
                _Slice(mod_tok, (None, db * ts, d), lambda grp, lj=lj, k=k: (lj, 0, k)) for k in range(3)
            ) + (mod[lj, bp:n_c, 2 * d:3 * d][:, None, :],)

    past = page_table.shape[1] * cache_kv_cmp.shape[1]
    bias_dist = _BiasTable(rel_table, max(tp, past + 2 * NEW_PAD), pad=max(tp, NEW_PAD))
    caches = dict(cache_kv_cmp=cache_kv_cmp, cache_kv_sel=cache_kv_sel, cache_kv_win=cache_kv_win,
                  state_conv=state_conv, page_table=page_table)
    y_p, cmp_p, sel_p, win_p, conv_p = _trunk(x_prompt, mods_p, True, None, P, W, bias_dist)
    y_s, cmp_s, sel_s, win_s, conv_s = _trunk(x_sample, mods_s, False, caches, P, W, bias_dist)
    return (y_p, y_s, cmp_p, cmp_s, sel_p, sel_s, win_p, win_s, conv_p, conv_s)
```

```python
import functools
import math

import numpy as np
import jax
import jax.numpy as jnp
from jax import lax
from jax.experimental import pallas as pl
from jax.experimental.pallas import tpu as pltpu

F32 = jnp.float32
BF16 = jnp.bfloat16

N_HEADS = 16
KV_HEADS = 4
GROUP = N_HEADS // KV_HEADS
HEAD_DIM = 64
CONV_W = 31
CONV_PAD = 32
SUBLANES = 8
LANES = 128
PROMPT_ROW_TILE = 256
CMP_BLOCK = 32
SEL_BLOCK = 64
N_SEL = 16
WINDOW = 512
N_BUCKETS = 32
MAX_DISTANCE = 1024
EPS = 1e-6
FORCE = 1e4
MASK_NEG = -1e9
M_INIT = -1e30
QT = LANES
NEAR_TILES = 8
KEY_STEP = 512
NEAR_SPAN = NEAR_TILES + KEY_STEP // QT - 1
V7X_VMEM_LIMIT_BYTES = 56 * 1024 * 1024


def _cparams(*sem):
    return pltpu.CompilerParams(dimension_semantics=sem, vmem_limit_bytes=V7X_VMEM_LIMIT_BYTES)


class _Slice:
    def __init__(self, arr, block, index):
        self.arr, self.block, self.index = arr, block, index
        self.shape = tuple(b for b in block if b is not None)

    def spec(self, tiles_per_group=1):
        index = self.index
        return pl.BlockSpec(self.block, lambda i, *_: index(i // tiles_per_group))


def _layer(arr, l):
    nd = arr.ndim
    return _Slice(arr, (None,) + arr.shape[1:], lambda _: (l,) + (0,) * (nd - 1))


def _arr(a):
    return a.arr if isinstance(a, _Slice) else a


def _full_spec(a):
    if isinstance(a, _Slice):
        return a.spec()
    nd = a.ndim
    return pl.BlockSpec(a.shape, lambda *_: (0,) * nd)


def _silu(x):
    return x * jax.nn.sigmoid(x)


def _norm_mod(x, g, shift, scale):
    ms = jnp.mean(x * x, axis=-1, keepdims=True)
    return x * lax.rsqrt(ms + EPS) * g * (1.0 + scale) + shift


def _dot(a, b):
    return jnp.dot(a, b, preferred_element_type=F32)


def _dot_nt(a, b):
    return lax.dot_general(a, b, (((1,), (1,)), ((), ())), preferred_element_type=F32)


def _ada_kernel(c_ref, w_ref, b_ref, o_ref):
    c = c_ref[...]
    o_ref[0] = _dot(_silu(c), w_ref[0]) + b_ref[0]


def ada_modulation(c_all, w_ada, b_ada):
    nl, d, n3 = w_ada.shape
    r = c_all.shape[0]
    tn = n3 // 2
    return pl.pallas_call(
        _ada_kernel,
        grid=(nl, n3 // tn),
        in_specs=[pl.BlockSpec((r, d), lambda l, j: (0, 0)),
                  pl.BlockSpec((1, d, tn), lambda l, j: (l, 0, j)),
                  pl.BlockSpec((1, 1, tn), lambda l, j: (l, 0, j))],
        out_specs=pl.BlockSpec((1, r, tn), lambda l, j: (l, 0, j)),
        out_shape=jax.ShapeDtypeStruct((nl, r, n3), F32),
        compiler_params=_cparams("parallel", "parallel"),
        name="ada_modulation",
    )(c_all, w_ada, b_ada)


def _row_spec(tm, width):
    return pl.BlockSpec((tm, width), lambda i: (i, 0))


def _mod_spec(mod, tiles_per_group):
    return mod.spec(tiles_per_group)


def _pw1_kernel(x_ref, g_ref, sh_ref, sc_ref, w_ref, b_ref, u_ref):
    d = x_ref.shape[1]
    h = _norm_mod(x_ref[...], g_ref[...], sh_ref[...], sc_ref[...]).astype(BF16)
    z = _dot(h, w_ref[...]) + b_ref[...]
    u_ref[...] = z[:, :d] * jax.nn.sigmoid(z[:, d:])


def pw1_glu(x, g, shift, scale, w, b, tm, tpg):
    m, d = x.shape
    return pl.pallas_call(
        _pw1_kernel,
        grid=(m // tm,),
        in_specs=[_row_spec(tm, d), _full_spec(g), _mod_spec(shift, tpg), _mod_spec(scale, tpg),
                  _full_spec(w), _full_spec(b)],
        out_specs=_row_spec(tm, d),
        out_shape=jax.ShapeDtypeStruct((m, d), F32),
        compiler_params=_cparams("parallel"),
        name="pw1_glu",
    )(*map(_arr, (x, g, shift, scale, w, b)))


def _conv_tail(rows_at, wdw_ref, bdw_ref, lg_ref, lb_ref, w2_ref, b2_ref):
    acc = None
    for k in range(CONV_W):
        term = rows_at(k + CONV_PAD - CONV_W + 1) * wdw_ref[k:k + 1, :]
        acc = term if acc is None else acc + term
    acc = acc + bdw_ref[...]
    mu = jnp.mean(acc, axis=-1, keepdims=True)
    cen = acc - mu
    var = jnp.mean(cen * cen, axis=-1, keepdims=True)
    yn = cen * lax.rsqrt(var + EPS) * lg_ref[...] + lb_ref[...]
    return _dot(_silu(yn).astype(BF16), w2_ref[...]) + b2_ref[...]


def _conv_kernel(halo_ref, u_ref, wdw_ref, bdw_ref, lg_ref, lb_ref, w2_ref, b2_ref, x_ref, gate_ref,
                 o_ref, buf, shifted, *, tiles_per_seq):
    tm = x_ref.shape[0]
    first = pl.program_id(0) % tiles_per_seq == 0
    buf[0:CONV_PAD, :] = jnp.where(first, 0.0, halo_ref[0])
    buf[CONV_PAD:CONV_PAD + tm, :] = u_ref[0]
    span = tm + CONV_PAD - SUBLANES
    for s in range(1, SUBLANES):
        shifted[s - 1] = buf[pl.ds(s, span), :]

    def rows_at(o):
        a, s = divmod(o, SUBLANES)
        if s == 0:
            return buf[pl.ds(o, tm), :]
        return shifted[s - 1, pl.ds(a * SUBLANES, tm), :]

    out = _conv_tail(rows_at, wdw_ref, bdw_ref, lg_ref, lb_ref, w2_ref, b2_ref)
    o_ref[...] = x_ref[...] + gate_ref[...] * out


def conv_pw2_residual(u, wdw, bdw, lg, lb, w2, b2, x, gate, tm, tpg):
    m, d = x.shape
    bsz, t, _ = u.shape
    assert t % tm == 0 and tm % CONV_PAD == 0 and m == bsz * t
    tps = t // tm
    return pl.pallas_call(
        functools.partial(_conv_kernel, tiles_per_seq=tps),
        grid=(m // tm,),
        in_specs=[pl.BlockSpec((1, CONV_PAD, d),
                               lambda i: (i // tps, jnp.maximum((i % tps) * (tm // CONV_PAD) - 1, 0), 0)),
                  pl.BlockSpec((1, tm, d), lambda i: (i // tps, i % tps, 0)),
                  _full_spec(wdw), _full_spec(bdw), _full_spec(lg), _full_spec(lb),
                  _full_spec(w2), _full_spec(b2), _row_spec(tm, d), _mod_spec(gate, tpg)],
        out_specs=_row_spec(tm, d),
        out_shape=jax.ShapeDtypeStruct((m, d), F32),
        scratch_shapes=[pltpu.VMEM((tm + CONV_PAD, d), F32),
                        pltpu.VMEM((SUBLANES - 1, tm + CONV_PAD - SUBLANES, d), F32)],
        compiler_params=_cparams("parallel"),
        name="conv_pw2_residual",
    )(*map(_arr, (u, u, wdw, bdw, lg, lb, w2, b2, x, gate)))


def _conv_small_kernel(full_ref, wdw_ref, bdw_ref, lg_ref, lb_ref, w2_ref, b2_ref, x_ref, gate_ref,
                       o_ref):
    tm = x_ref.shape[1]
    rows_at = lambda o: full_ref[0, pl.ds(o, tm), :]
    out = _conv_tail(rows_at, wdw_ref, bdw_ref, lg_ref, lb_ref, w2_ref, b2_ref)
    o_ref[0] = x_ref[0] + gate_ref[0] * out


def conv_pw2_residual_small(full, wdw, bdw, lg, lb, w2, b2, x, gate):
    bsz, ts, d = x.shape
    seq = lambda a: pl.BlockSpec((1,) + a.shape[1:], lambda i: (i, 0, 0))
    return pl.pallas_call(
        _conv_small_kernel,
        grid=(bsz,),
        in_specs=[seq(full), _full_spec(wdw), _full_spec(bdw), _full_spec(lg), _full_spec(lb),
                  _full_spec(w2), _full_spec(b2), seq(x), seq(gate)],
        out_specs=seq(x),
        out_shape=jax.ShapeDtypeStruct(x.shape, F32),
        compiler_params=_cparams("parallel"),
        name="conv_pw2_residual_small",
    )(*map(_arr, (full, wdw, bdw, lg, lb, w2, b2, x, gate)))


def _ffn_kernel(x_ref, *refs, attn, kv, final):
    x = x_ref[...]
    if attn:
        att_ref, wo_ref, ga_ref = refs[:3]
        refs = refs[3:]
        x = x + ga_ref[...] * _dot(att_ref[...], wo_ref[...])
    g_ref, sh_ref, sc_ref, gate_ref, wg_ref, wu_ref, wd_ref = refs[:7]
    h = _norm_mod(x, g_ref[...], sh_ref[...], sc_ref[...]).astype(BF16)
    a = _dot(h, wg_ref[...])
    b = _dot(h, wu_ref[...])
    act = (_silu(a) * b).astype(BF16)
    xn = x + gate_ref[...] * _dot(act, wd_ref[...])
    refs = refs[7:]
    if kv:
        gkv_ref, wkv_ref = refs[:2]
        refs = refs[2:]
        rows_ref = refs[-1]
        ms = jnp.mean(xn * xn, axis=-1, keepdims=True)
        rows_ref[...] = _dot((xn * lax.rsqrt(ms + EPS) * gkv_ref[...]).astype(BF16), wkv_ref[...])
    if final:
        fg_ref, o_ref = refs[:2]
        ms = jnp.mean(xn * xn, axis=-1, keepdims=True)
        o_ref[...] = xn * lax.rsqrt(ms + EPS) * fg_ref[...]
    else:
        o_ref = refs[0]
        o_ref[...] = xn


def ffn(x, g, shift, scale, gate, wg, wu, wd, tm, tpg, attn=None, kv=None, final_g=None):
    m, d = x.shape
    final = final_g is not None
    ins = [x]
    specs = [_row_spec(tm, d)]
    if attn is not None:
        o, w_o, gate_a = attn
        ins += [o, w_o, gate_a]
        specs += [_row_spec(tm, o.shape[1]), _full_spec(w_o), _mod_spec(gate_a, tpg)]
    ins += [g, shift, scale, gate, wg, wu, wd]
    specs += [_full_spec(g), _mod_spec(shift, tpg), _mod_spec(scale, tpg), _mod_spec(gate, tpg),
              _full_spec(wg), _full_spec(wu), _full_spec(wd)]
    out_specs = [_row_spec(tm, d)]
    out_shape = [jax.ShapeDtypeStruct((m, d), F32)]
    if kv is not None:
        ins += list(kv)
        specs += [_full_spec(kv[0]), _full_spec(kv[1])]
        out_specs.append(_row_spec(tm, kv[1].shape[1]))
        out_shape.append(jax.ShapeDtypeStruct((m, kv[1].shape[1]), F32))
    if final:
        ins.append(final_g)
        specs.append(_full_spec(final_g))
    outs = pl.pallas_call(
        functools.partial(_ffn_kernel, attn=attn is not None, kv=kv is not None, final=final),
        grid=(m // tm,),
        in_specs=specs,
        out_specs=out_specs,
        out_shape=out_shape,
        compiler_params=_cparams("parallel"),
        name="ffn_final" if final else "ffn",
    )(*map(_arr, ins))
    return outs if kv is not None else outs[0]


def _qg_kernel(x_ref, g_ref, sh_ref, sc_ref, w_ref, q_ref, gates_ref):
    nq = q_ref.shape[1]
    h = _norm_mod(x_ref[...], g_ref[...], sh_ref[...], sc_ref[...]).astype(BF16)
    z = _dot(h, w_ref[...])
    q_ref[...] = (z[:, :nq] * (HEAD_DIM ** -0.5)).astype(BF16)
    gates_ref[...] = jax.nn.sigmoid(z[:, nq:])


def qg_proj(x, g, shift, scale, w, tm, tpg):
    m, d = x.shape
    nq = N_HEADS * HEAD_DIM
    ng = w.shape[1] - nq
    return pl.pallas_call(
        _qg_kernel,
        grid=(m // tm,),
        in_specs=[_row_spec(tm, d), _full_spec(g), _mod_spec(shift, tpg), _mod_spec(scale, tpg),
                  _full_spec(w)],
        out_specs=[_row_spec(tm, nq), _row_spec(tm, ng)],
        out_shape=[jax.ShapeDtypeStruct((m, nq), BF16), jax.ShapeDtypeStruct((m, ng), F32)],
        compiler_params=_cparams("parallel"),
        name="qg_proj",
    )(*map(_arr, (x, g, shift, scale, w)))


SLAB = LANES
HEADS_PER_SLAB = SLAB // HEAD_DIM


R_PER_DOT = 2
CMP_PITCH = 40


def _compress_core(rows_at, pe_row, w1_at, b1, w2, b2):
    hidden = w2.shape[0]
    acc = None
    for i in range(CMP_BLOCK // R_PER_DOT):
        parts = [(rows_at(R_PER_DOT * i + j) + pe_row(R_PER_DOT * i + j)).astype(BF16) for j in range(R_PER_DOT)]
        term = _dot(jnp.concatenate(parts, axis=1), w1_at(i))
        acc = term if acc is None else acc + term
    hid = _silu(acc + b1)
    return [_dot(hid[:, hh * hidden:(hh + 1) * hidden].astype(BF16), w2) + b2 for hh in range(HEADS_PER_SLAB)]


def _compress_kernel(x_ref, pe_ref, w1_ref, b1_ref, w2_ref, b2_ref, o_ref, *, nblk):
    outs = _compress_core(lambda r: x_ref[0, pl.ds(r, nblk, stride=CMP_BLOCK), :],
                          lambda r: pe_ref[0, r:r + 1, :], lambda i: w1_ref[0, i],
                          b1_ref[0], w2_ref[0], b2_ref[0])
    for hh, out in enumerate(outs):
        o_ref[0, 0, hh] = out


def compress(x, nblk, pe_slab, w1_bd, b1_slab, w2, b2):
    bc, rows, _ = x.shape
    n_slab = 2 * KV_HEADS // HEADS_PER_SLAB
    slabs_per_e = n_slab // 2
    per_e = lambda a: pl.BlockSpec((1,) + a.shape[1:], lambda b, s: (s // slabs_per_e,) + (0,) * (a.ndim - 1))
    return pl.pallas_call(
        functools.partial(_compress_kernel, nblk=nblk),
        grid=(bc, n_slab),
        in_specs=[pl.BlockSpec((1, rows, SLAB), lambda b, s: (b, 0, s)),
                  pl.BlockSpec((1, CMP_BLOCK, SLAB), lambda b, s: (s, 0, 0)),
                  per_e(w1_bd), per_e(b1_slab), per_e(w2), per_e(b2)],
        out_specs=pl.BlockSpec((1, 1, HEADS_PER_SLAB, nblk, HEAD_DIM),
                               lambda b, s: (b, s // slabs_per_e, s % slabs_per_e, 0, 0)),
        out_shape=jax.ShapeDtypeStruct((bc, 2, KV_HEADS, nblk, HEAD_DIM), F32),
        compiler_params=_cparams("parallel", "parallel"),
        name="compress",
    )(x, pe_slab, w1_bd, b1_slab, w2, b2)


def _compress_paged_kernel(pt_ref, *refs, n_pages):
    page_refs = refs[:n_pages]
    pe_ref, w1_ref, b1_ref, w2_ref, b2_ref, o_ref, scr = refs[n_pages:]
    blocks_per_page = page_refs[0].shape[1] // CMP_BLOCK
    nblk = n_pages * blocks_per_page
    n_slab = pe_ref.shape[0]
    slabs_per_e = n_slab // 2
    for s in range(n_slab):
        e = s // slabs_per_e
        for p in range(n_pages):
            for j in range(blocks_per_page):
                base = (p * blocks_per_page + j) * CMP_PITCH
                scr[base:base + CMP_BLOCK, :] = page_refs[p][0, j * CMP_BLOCK:(j + 1) * CMP_BLOCK,
                                                             s * SLAB:(s + 1) * SLAB]
        outs = _compress_core(lambda r: scr[pl.ds(r, nblk, stride=CMP_PITCH), :],
                              lambda r: pe_ref[s, r:r + 1, :], lambda i: w1_ref[e, i],
                              b1_ref[e], w2_ref[e], b2_ref[e])
        for hh, out in enumerate(outs):
            o_ref[0, e, (s % slabs_per_e) * HEADS_PER_SLAB + hh] = out


def compress_paged(cache, page_table, pe_slab, w1_bd, b1_slab, w2, b2):
    n_pool, page, c = cache.shape
    db, n_pages = page_table.shape
    assert page % CMP_BLOCK == 0
    nblk = n_pages * page // CMP_BLOCK
    const = lambda a: pl.BlockSpec(a.shape, lambda b, pt: (0,) * a.ndim)

    def page_spec(k):
        return pl.BlockSpec((1, page, c), lambda b, pt: (pt[b, k], 0, 0))

    return pl.pallas_call(
        functools.partial(_compress_paged_kernel, n_pages=n_pages),
        grid_spec=pltpu.PrefetchScalarGridSpec(
            num_scalar_prefetch=1,
            grid=(db,),
            in_specs=[page_spec(k) for k in range(n_pages)]
            + [const(pe_slab), const(w1_bd), const(b1_slab), const(w2), const(b2)],
            out_specs=pl.BlockSpec((1, 2, KV_HEADS, nblk, HEAD_DIM), lambda b, pt: (b, 0, 0, 0, 0)),
            scratch_shapes=[pltpu.VMEM((nblk * CMP_PITCH, SLAB), F32)],
        ),
        out_shape=jax.ShapeDtypeStruct((db, 2, KV_HEADS, nblk, HEAD_DIM), F32),
        compiler_params=_cparams("parallel"),
        name="compress_paged",
    )(page_table, *([cache] * n_pages), pe_slab, w1_bd, b1_slab, w2, b2)


def _rel_bucket(dist):
    n = jnp.maximum(dist, 0)
    exact = N_BUCKETS // 2
    nf = jnp.maximum(n, 1).astype(F32)
    large = exact + (jnp.log(nf / exact) / math.log(MAX_DISTANCE / exact) * (N_BUCKETS - exact)).astype(jnp.int32)
    large = jnp.minimum(large, N_BUCKETS - 1)
    return jnp.where(n < exact, n, large)


class _BiasTable:
    def __init__(self, table, max_dist, pad):
        tab = table.astype(F32)

        def lookup(dist):
            bucket = _rel_bucket(dist)
            out = jnp.zeros((tab.shape[1], dist.shape[0]), F32)
            for k in range(N_BUCKETS):
                out = jnp.where(bucket[None, :] == k, tab[k][:, None], out)
            return out

        self.pad = pad
        self.n = max_dist
        self.ext = lookup(jnp.arange(-pad, max_dist))
        self.rev = lookup(max_dist - 1 - jnp.arange(pad + max_dist))

    def at(self, d):
        return self.ext[:, self.pad + d]

    def rising(self, d0, length):
        return lax.slice_in_dim(self.ext, self.pad + d0, self.pad + d0 + length, axis=1)

    def falling(self, d0, length, step=1):
        start = self.n - 1 - d0
        return lax.slice(self.rev, (0, start), (self.rev.shape[0], start + step * (length - 1) + 1), (1, step))

    def frames(self, d0, count, hop, length):
        reps = length // hop
        seq = self.rising(d0, hop * (count + reps - 1)).reshape(-1, count + reps - 1, hop)
        return jnp.concatenate([seq[:, a:a + count] for a in range(reps)], axis=-1)

    def toeplitz(self, d0, count, hop, size):
        assert hop == size
        period = 2 * size - 1
        base = self.rising(d0 - (size - 1), hop * (count + 1)).reshape(-1, count + 1, hop)
        w = jnp.concatenate([base[:, :count, size - 1:], base[:, 1:, :size - 1],
                             base[:, :count, :size - 1]], axis=-1)
        rows = jnp.tile(w, (1, 1, size))[:, :, :size * (period - 1)]
        return rows.reshape(w.shape[0], count, size, period - 1)[..., :size]


def _softmax_step(s, vt, m, l, acc):
    m_new = jnp.maximum(m, jnp.max(s, axis=0, keepdims=True))
    alpha = jnp.exp(m - m_new)
    p = jnp.exp(s - m_new)
    l = alpha * l + jnp.sum(p, axis=0, keepdims=True)
    acc = alpha * acc + _dot(vt, p.astype(BF16))
    return m_new, l, acc


def _attn_kernel(bfar_ref, q_ref, gt_ref, kaug_ref, vst_ref, kw_ref, vwt_ref, ck_ref, cvt_ref,
                 bc_ref, bn_ref, o_ref, imp_ref, sc_ref, s_ref, *, nb, nc):
    g = pl.program_id(0)
    qt = pl.program_id(2)
    qs = qt * QT
    gq = GROUP * QT

    lane_cat = lambda parts: jnp.concatenate(parts, axis=1)
    q_t = q_ref[0].astype(F32).T.astype(BF16)
    q_heads = [q_t[h * HEAD_DIM:(h + 1) * HEAD_DIM] for h in range(GROUP)]
    q_plain = lane_cat(q_heads)

    col_i = lax.broadcasted_iota(jnp.int32, (1, gq), 1) % QT

    s_c = _dot(ck_ref[0, 0], q_plain) + lane_cat([bc_ref[h, 0] for h in range(GROUP)])
    n_idx = lax.broadcasted_iota(jnp.int32, (nc, gq), 0)
    valid_c = (n_idx * CMP_BLOCK + CMP_BLOCK - 1) <= (qs + col_i)
    s_c = jnp.where(valid_c, s_c, M_INIT)
    m_c = jnp.max(s_c, axis=0, keepdims=True)
    p_c = jnp.where(valid_c, jnp.exp(s_c - m_c), 0.0)
    l_c = jnp.sum(p_c, axis=0, keepdims=True)
    p_c = p_c * jnp.where(l_c > 0.0, 1.0 / l_c, 0.0)
    o_c = _dot(cvt_ref[0, 0], p_c.astype(BF16))

    imp = p_c[:, 0:QT]
    for h in range(1, GROUP):
        imp = imp + p_c[:, h * QT:(h + 1) * QT]
    imp_ref[...] = imp
    ratio = SEL_BLOCK // CMP_BLOCK
    imp2 = imp_ref[pl.ds(0, nb, stride=ratio), :]
    for r in range(1, ratio):
        imp2 = imp2 + imp_ref[pl.ds(r, nb, stride=ratio), :]
    j_idx = lax.broadcasted_iota(jnp.int32, (nb, QT), 0)
    i_idx = lax.broadcasted_iota(jnp.int32, (nb, QT), 1)
    jq = (qs + i_idx) // SEL_BLOCK
    forced = (j_idx == 0) | (j_idx == jq) | (j_idx == jq - 1)
    score = jnp.where(j_idx <= jq, jnp.where(forced, FORCE, imp2), -1.0)
    sc_ref[...] = score

    sub8 = lax.broadcasted_iota(jnp.int32, (8, QT), 0)
    groups = [score[8 * r:8 * r + 8] for r in range(nb // 8)]
    counts = [jnp.zeros((8, QT), F32) for _ in groups]
    for c in range(nb):
        row = sc_ref[c:c + 1, :]
        for r, grp_score in enumerate(groups):
            if r > c // 8:
                beats = row >= grp_score
            elif r < c // 8:
                beats = row > grp_score
            else:
                tie = jnp.where(row == grp_score, 1.0, 0.0) * jnp.where(sub8 > c % 8, 1.0, 0.0)
                counts[r] = counts[r] + tie
                beats = row > grp_score
            counts[r] = counts[r] + jnp.where(beats, 1.0, 0.0)
    cnt = jnp.concatenate(counts, axis=0)
    selected = (cnt < float(min(N_SEL, nb))) & (score >= 0.0)
    mask_feat = jnp.where(selected, 0.0, MASK_NEG).astype(BF16)
    q_aug = lane_cat([jnp.concatenate([qh, mask_feat], axis=0) for qh in q_heads])

    far_row = lane_cat([jnp.full((1, QT), bfar_ref[g * GROUP + h], F32) for h in range(GROUP)])

    def near_bias(kt0, n_sub):
        parts = []
        for sub in range(n_sub):
            dm = jnp.clip(qt - (kt0 + sub), 0, NEAR_SPAN - 1)
            parts.append(lane_cat([bn_ref[h, dm] for h in range(GROUP)]))
        return jnp.concatenate(parts, axis=0)

    init = (jnp.full((1, gq), M_INIT, F32), jnp.zeros((1, gq), F32), jnp.zeros((HEAD_DIM, gq), F32))
    sub_per_chunk = KEY_STEP // QT
    row_k = lax.broadcasted_iota(jnp.int32, (KEY_STEP, gq), 0)

    last_chunk = qt // sub_per_chunk

    def raw_scores(c):
        ks = pl.multiple_of(jnp.minimum(c, last_chunk) * KEY_STEP, KEY_STEP)
        return _dot(kaug_ref[0, 0, pl.ds(ks, KEY_STEP), :], q_aug)

    def pipelined(finish):
        def body(c, carry):
            cur = s_ref[c % 2]
            s_ref[(c + 1) % 2] = raw_scores(c + 1)
            ks = pl.multiple_of(c * KEY_STEP, KEY_STEP)
            return _softmax_step(finish(c, cur, ks), vst_ref[0, 0, :, pl.ds(ks, KEY_STEP)], *carry)
        return body

    def far_finish(c, s, ks):
        return s + far_row

    def near_finish(c, s, ks):
        return s + near_bias(c * sub_per_chunk, sub_per_chunk)

    wt0 = jnp.maximum(qt - WINDOW // QT, 0)
    ws = pl.multiple_of(wt0 * QT, QT)
    wk = WINDOW + QT
    s_w = _dot(kw_ref[0, 0, pl.ds(ws, wk), :], q_plain) + near_bias(wt0, wk // QT)
    dist = (qs + col_i) - (ws + lax.broadcasted_iota(jnp.int32, (wk, gq), 0))
    s_w = jnp.where((dist >= 0) & (dist < WINDOW), s_w, MASK_NEG)
    p_w = jnp.exp(s_w - jnp.max(s_w, axis=0, keepdims=True))
    l_w = jnp.sum(p_w, axis=0, keepdims=True)
    o_w = _dot(vwt_ref[0, 0, :, pl.ds(ws, wk)], p_w.astype(BF16)) / l_w

    s_ref[0] = raw_scores(0)
    n_far = jnp.maximum(qt - (NEAR_TILES - 1), 0) // sub_per_chunk
    carry = lax.fori_loop(0, n_far, pipelined(far_finish), init)
    carry = lax.fori_loop(n_far, last_chunk, pipelined(near_finish), carry)
    ks = pl.multiple_of(last_chunk * KEY_STEP, KEY_STEP)
    s = near_finish(last_chunk, s_ref[last_chunk % 2], ks)
    s = jnp.where(ks + row_k <= qs + col_i, s, MASK_NEG)
    m_s, l_s, acc_s = _softmax_step(s, vst_ref[0, 0, :, pl.ds(ks, KEY_STEP)], *carry)
    o_s = acc_s / l_s

    gate = lambda br: lane_cat([gt_ref[0, 0, h * 3 + br:h * 3 + br + 1, :] for h in range(GROUP)])
    o_t = gate(0) * o_c + gate(1) * o_s + gate(2) * o_w
    o_rows = jnp.concatenate([o_t[:, h * QT:(h + 1) * QT] for h in range(GROUP)], axis=0)
    o_ref[0] = o_rows.T.astype(BF16)


def prompt_attention(q, gates_t, kaug, vst, kwin, vwt, ck, cvt, bias_cmp, bias_near, bias_far):
    bsz, t, dq = q.shape
    nb = t // SEL_BLOCK
    nc = t // CMP_BLOCK
    nq = t // QT
    gd = GROUP * HEAD_DIM
    per_bg = lambda a: pl.BlockSpec((1, 1) + a.shape[2:], lambda g, b, i, *_: (b, g, 0, 0))
    grid_spec = pltpu.PrefetchScalarGridSpec(
        num_scalar_prefetch=1,
        grid=(KV_HEADS, bsz, nq),
        in_specs=[pl.BlockSpec((1, QT, gd), lambda g, b, i, *_: (b, i, g)),
                  pl.BlockSpec((1, 1, 3 * GROUP, QT), lambda g, b, i, *_: (b, g, 0, i)),
                  per_bg(kaug), per_bg(vst), per_bg(kwin), per_bg(vwt), per_bg(ck), per_bg(cvt),
                  pl.BlockSpec((GROUP, 1, nc, QT), lambda g, b, i, *_: (g, i, 0, 0)),
                  pl.BlockSpec((GROUP, NEAR_SPAN, QT, QT), lambda g, b, i, *_: (g, 0, 0, 0))],
        out_specs=pl.BlockSpec((1, QT, gd), lambda g, b, i, *_: (b, i, g)),
        scratch_shapes=[pltpu.VMEM((nc, QT), F32), pltpu.VMEM((nb, QT), F32),
                        pltpu.VMEM((2, KEY_STEP, GROUP * QT), F32)],
    )
    return pl.pallas_call(
        functools.partial(_attn_kernel, nb=nb, nc=nc),
        grid_spec=grid_spec,
        out_shape=jax.ShapeDtypeStruct((bsz, t, dq), BF16),
        compiler_params=_cparams("parallel", "parallel", "parallel"),
        name="prompt_attention",
    )(bias_far, q, gates_t, kaug, vst, kwin, vwt, ck, cvt, bias_cmp, bias_near)


KEY_CHUNK = 1024
NEW_PAD = LANES


def _diag_heads(full, row_g):
    out = None
    for g in range(KV_HEADS):
        part = jnp.where(row_g == g, full[:, g * HEAD_DIM:(g + 1) * HEAD_DIM], 0.0)
        out = part if out is None else out + part
    return out


def _sample_attn_kernel(pt_ref, q_ref, gates_ref, *refs, past, ts, n_pages):
    page_refs = refs[:n_pages]
    (nsel_ref, kwin_ref, nwin_ref, ck_ref, cv_ref, bsel_ref, bwin_ref, bcmp_ref, e_ref, o_ref,
     s_scr) = refs[n_pages:]
    kvd = KV_HEADS * HEAD_DIM
    page = page_refs[0].shape[1]
    pages_per_chunk = KEY_CHUNK // page

    def cached(c, lo, hi):
        parts = [page_refs[c * pages_per_chunk + j][0, :, lo:hi] for j in range(pages_per_chunk)]
        return jnp.concatenate(parts, axis=0).astype(BF16)

    rows = GROUP * KV_HEADS * ts
    nc = ck_ref.shape[1]
    q = q_ref[0]
    r_idx = lax.broadcasted_iota(jnp.int32, (rows, 1), 0)
    row_q = r_idx % ts
    row_g = (r_idx // ts) % KV_HEADS

    s_c = _dot_nt(q, ck_ref[0]) + bcmp_ref[...]
    p_c = jnp.exp(s_c - jnp.max(s_c, axis=1, keepdims=True))
    p_c = p_c / jnp.sum(p_c, axis=1, keepdims=True)
    o_c = _diag_heads(_dot(p_c.astype(BF16), cv_ref[0]), row_g)

    per = KV_HEADS * ts
    imp = p_c[0:per]
    for h in range(1, GROUP):
        imp = imp + p_c[h * per:(h + 1) * per]
    ratio = SEL_BLOCK // CMP_BLOCK
    imp2 = imp
    for r in range(1, ratio):
        imp2 = imp2 + pltpu.roll(imp, nc - r, axis=1)
    lane = lax.broadcasted_iota(jnp.int32, (per, nc), 1)
    blk = lane // ratio
    jq = past // SEL_BLOCK
    forced = (blk == 0) | (blk == jq - 1)
    score = jnp.where(lane % ratio == 0, jnp.where(forced, FORCE, imp2), -2.0)
    cnt = jnp.where(score < FORCE, 1.0, 0.0)
    for c in range(nc // ratio):
        col = score[:, c * ratio:c * ratio + 1]
        ge = jnp.where(col >= score, 1.0, 0.0)
        gt = jnp.where(col > score, 1.0, 0.0)
        cnt = cnt + jnp.where(lane > c * ratio, ge, gt)
    selected = (cnt < float(min(N_SEL, jq + 1))) & (score >= 0.0)
    mask_feat = jnp.where(selected, 0.0, MASK_NEG).astype(BF16)
    mask_feat = jnp.concatenate([mask_feat] * GROUP, axis=0)

    for c in range(past // KEY_CHUNK):
        sl = slice(c * KEY_CHUNK, (c + 1) * KEY_CHUNK)
        s_scr[:, sl] = _dot_nt(q, cached(c, 0, kvd)) + bsel_ref[:, sl] + _dot(mask_feat, e_ref[:, sl])
    new_j = lax.broadcasted_iota(jnp.int32, (rows, NEW_PAD), 1)
    new_ok = new_j <= row_q
    s_new = _dot_nt(q, nsel_ref[0, :, 0:kvd].astype(BF16)) + bsel_ref[:, past:past + NEW_PAD]
    s_scr[:, past:past + NEW_PAD] = jnp.where(new_ok, s_new, MASK_NEG)
    s_all = s_scr[...]
    p_s = jnp.exp(s_all - jnp.max(s_all, axis=1, keepdims=True))
    l_s = jnp.sum(p_s, axis=1, keepdims=True)
    s_scr[...] = p_s
    acc = _dot(s_scr[:, past:past + NEW_PAD].astype(BF16), nsel_ref[0, :, kvd:2 * kvd].astype(BF16))
    for c in range(past // KEY_CHUNK):
        sl = slice(c * KEY_CHUNK, (c + 1) * KEY_CHUNK)
        acc = acc + _dot(s_scr[:, sl].astype(BF16), cached(c, kvd, 2 * kvd))
    o_s = _diag_heads(acc, row_g) / l_s

    wlen = kwin_ref.shape[1]
    win_i = lax.broadcasted_iota(jnp.int32, (rows, wlen), 1)
    s_w1 = _dot_nt(q, kwin_ref[0, :, 0:kvd].astype(BF16)) + bwin_ref[:, 0:wlen]
    s_w1 = jnp.where(win_i + (WINDOW - wlen) > row_q, s_w1, MASK_NEG)
    s_w2 = _dot_nt(q, nwin_ref[0, :, 0:kvd].astype(BF16)) + bwin_ref[:, wlen:wlen + NEW_PAD]
    s_w2 = jnp.where(new_ok, s_w2, MASK_NEG)
    m_w = jnp.maximum(jnp.max(s_w1, axis=1, keepdims=True), jnp.max(s_w2, axis=1, keepdims=True))
    p_w1 = jnp.exp(s_w1 - m_w)
    p_w2 = jnp.exp(s_w2 - m_w)
    l_w = jnp.sum(p_w1, axis=1, keepdims=True) + jnp.sum(p_w2, axis=1, keepdims=True)
    acc_w = (_dot(p_w1.astype(BF16), kwin_ref[0, :, kvd:2 * kvd].astype(BF16))
             + _dot(p_w2.astype(BF16), nwin_ref[0, :, kvd:2 * kvd].astype(BF16)))
    o_w = _diag_heads(acc_w, row_g) / l_w

    gts = gates_ref[0]
    o_ref[0] = gts[:, 0:1] * o_c + gts[:, 1:2] * o_s + gts[:, 2:3] * o_w


def sample_attention(page_table, q_bd, gates, cache, nsel, kwin, nwin, ck, cv, bsel, bwin, bcmp, e_mat, ts):
    db, rows, kvd = q_bd.shape
    n_pages = page_table.shape[1]
    page = cache.shape[1]
    past = n_pages * page
    assert KEY_CHUNK % page == 0 and past % KEY_CHUNK == 0
    seq = lambda a: pl.BlockSpec((1,) + a.shape[1:], lambda b, pt: (b,) + (0,) * (a.ndim - 1))
    const = lambda a: pl.BlockSpec(a.shape, lambda b, pt: (0,) * a.ndim)

    def page_spec(k):
        return pl.BlockSpec((1,) + cache.shape[1:], lambda b, pt: (pt[b, k], 0, 0))

    return pl.pallas_call(
        functools.partial(_sample_attn_kernel, past=past, ts=ts, n_pages=n_pages),
        grid_spec=pltpu.PrefetchScalarGridSpec(
            num_scalar_prefetch=1,
            grid=(db,),
            in_specs=[seq(q_bd), seq(gates)] + [page_spec(k) for k in range(n_pages)]
            + [seq(nsel), seq(kwin), seq(nwin), seq(ck), seq(cv), const(bsel), const(bwin), const(bcmp),
               const(e_mat)],
            out_specs=pl.BlockSpec((1, rows, HEAD_DIM), lambda b, pt: (b, 0, 0)),
            scratch_shapes=[pltpu.VMEM((rows, past + NEW_PAD), F32)],
        ),
        out_shape=jax.ShapeDtypeStruct((db, rows, HEAD_DIM), F32),
        compiler_params=_cparams("parallel"),
        name="sample_attention",
    )(page_table, q_bd, gates, *([cache] * n_pages), nsel, kwin, nwin, ck, cv, bsel, bwin, bcmp, e_mat)


def _prep_weights(P):
    W = {}
    cast = lambda a: a.astype(BF16)
    for k in ("w_pw1", "w_pw2", "w_kv", "w_o", "w_gate", "w_up", "w_down", "w_phi1", "w_phi2"):
        W[k] = cast(P[k])
    nq = N_HEADS * HEAD_DIM
    wqg = P["w_qg"]
    pad = LANES - (wqg.shape[2] - nq)
    W["w_qg"] = cast(jnp.pad(wqg, ((0, 0), (0, 0), (0, pad))))
    rows = lambda a: a.reshape(-1, 1, a.shape[-1])
    for k in ("norm_g", "b_pw1", "b_dw", "ln_g", "ln_b", "b_pw2"):
        W[k] = rows(P[k])
    W["w_dw"] = jnp.pad(P["w_dw"], ((0, 0), (0, CONV_PAD - CONV_W), (0, 0)))
    return W


def _trunk(x, mods, is_prompt, caches, P, W, bias_dist):
    bsz, t, d = x.shape
    m = bsz * t
    depth = P["w_gate"].shape[0]
    n_a = P["w_pw1"].shape[0]
    if is_prompt:
        tm, tpg = PROMPT_ROW_TILE, t // PROMPT_ROW_TILE
    else:
        tm, tpg = m, 1
    x2 = x.reshape(m, d)
    row2 = lambda a: a.reshape(1, -1)
    conv_states = []
    y = None
    for l in range(depth):
        shift, scale, gate = mods[(l, 0)][:3]
        g0 = _layer(W["norm_g"], 2 * l)
        attn = None
        if l < n_a:
            u = pw1_glu(x2, g0, shift, scale, _layer(W["w_pw1"], l), _layer(W["b_pw1"], l), tm, tpg)
            u3 = u.reshape(bsz, t, d)
            args = tuple(_layer(W[k], l) for k in ("w_dw", "b_dw", "ln_g", "ln_b", "w_pw2", "b_pw2"))
            if is_prompt:
                conv_states.append(u3[:, -(CONV_W - 1):])
                x2 = conv_pw2_residual(u3, *args, x2, gate, tm, tpg)
            else:
                hist = jnp.pad(caches["state_conv"][l], ((0, 0), (CONV_PAD - CONV_W + 1, 0), (0, 0)))
                full = jnp.concatenate([hist, u3], axis=1)
                conv_states.append(full[:, -(CONV_W - 1):])
                x2 = conv_pw2_residual_small(full, *args, x2.reshape(bsz, t, d),
                                             mods[(l, 0)][3]).reshape(m, d)
        else:
            if l == n_a:
                rows6 = rows.reshape(bsz, t, 3, 2, KV_HEADS, HEAD_DIM)
                if is_prompt:
                    att = _prompt_attention_setup(rows6, P, W, bias_dist)
                    win_state = rows6[:, -min(WINDOW, t):, 2]
                else:
                    att = _sample_attention_setup(rows6, caches, P, W, bias_dist)
                    wl = caches["cache_kv_win"].shape[1]
                    win_state = jnp.concatenate([caches["cache_kv_win"], rows6[:, :, 2]], axis=1)[:, -wl:]
            lb = l - n_a
            q, gates = qg_proj(x2, g0, shift, scale, _layer(W["w_qg"], lb), tm, tpg)
            if is_prompt:
                o = _prompt_attention_layer(q.reshape(bsz, t, -1), gates.reshape(bsz, t, -1), att)
            else:
                o = _sample_attention_layer(q.reshape(bsz, t, -1), gates.reshape(bsz, t, -1), att)
            attn = (o.reshape(m, -1), _layer(W["w_o"], lb), gate)
        shift, scale, gate = mods[(l, 1)][:3]
        fg = row2(P["final_g"]) if l == depth - 1 else None
        kv = (row2(P["g_kv"]), W["w_kv"]) if l == n_a - 1 else None
        out = ffn(x2, _layer(W["norm_g"], 2 * l + 1), shift, scale, gate, _layer(W["w_gate"], l),
                  _layer(W["w_up"], l), _layer(W["w_down"], l), tm, tpg, attn=attn, kv=kv, final_g=fg)
        if kv is not None:
            out, rows = out
        if l == depth - 1:
            y = out
        else:
            x2 = out
    return (y.reshape(bsz, t, d), rows6[:, :, 0], rows6[:, :, 1], win_state, jnp.stack(conv_states))


def _compress_weights(P, W):
    pe = P["pe_cmp"]
    pe_slab = jnp.repeat(jnp.tile(pe, (1, 1, HEADS_PER_SLAB)), KV_HEADS // HEADS_PER_SLAB, axis=0)
    w1 = W["w_phi1"].reshape(2, CMP_BLOCK, HEAD_DIM, -1)
    zero = jnp.zeros_like(w1)
    w1_bd = jnp.concatenate([jnp.concatenate([w1, zero], axis=-1), jnp.concatenate([zero, w1], axis=-1)],
                            axis=2)
    w1_bd = w1_bd.reshape(2, CMP_BLOCK // R_PER_DOT, R_PER_DOT * SLAB, -1)
    b1_slab = jnp.tile(P["b_phi1"], (1, HEADS_PER_SLAB))[:, None, :]
    return pe_slab, w1_bd, b1_slab, W["w_phi2"], P["b_phi2"][:, None, :]


def _prompt_attention_setup(rows6, P, W, bias_dist):
    bsz, t = rows6.shape[:2]
    nb = t // SEL_BLOCK
    nc = t // CMP_BLOCK
    nq = t // QT
    cmp = compress(rows6.reshape(bsz, t, -1), nc, *_compress_weights(P, W))
    ck = cmp[:, 0].astype(BF16)
    cvt = cmp[:, 1].transpose(0, 1, 3, 2).astype(BF16)
    onehot = (jnp.arange(t)[:, None] // SEL_BLOCK == jnp.arange(nb)[None, :]).astype(BF16)
    ksel = rows6[:, :, 1, 0].transpose(0, 2, 1, 3).astype(BF16)
    kaug = jnp.concatenate([ksel, jnp.broadcast_to(onehot, (bsz, KV_HEADS, t, nb))], axis=-1)
    vst = rows6[:, :, 1, 1].transpose(0, 2, 3, 1).astype(BF16)
    kwin = rows6[:, :, 2, 0].transpose(0, 2, 1, 3).astype(BF16)
    vwt = rows6[:, :, 2, 1].transpose(0, 2, 3, 1).astype(BF16)
    per_qt = QT // CMP_BLOCK
    n_u = nc + per_qt * (nq - 1)
    frames = bias_dist.frames(-(CMP_BLOCK - 1) - CMP_BLOCK * (nc - 1), n_u, CMP_BLOCK, QT)[:, ::-1]
    frames = lax.optimization_barrier(frames)
    bias_cmp = jnp.stack([frames[:, per_qt * (nq - 1 - i):per_qt * (nq - 1 - i) + nc] for i in range(nq)],
                         axis=1)
    bias_near = bias_dist.toeplitz(0, NEAR_SPAN, QT, QT)
    bias_far = bias_dist.at(NEAR_TILES * QT)
    return dict(kaug=kaug, vst=vst, kwin=kwin, vwt=vwt, ck=ck, cvt=cvt, bias_cmp=bias_cmp,
                bias_near=bias_near, bias_far=bias_far)


def _prompt_attention_layer(q, gates, att):
    bsz, t, _ = q.shape
    gates_t = gates[:, :, :3 * N_HEADS].reshape(bsz, t, KV_HEADS, 3 * GROUP).transpose(0, 2, 3, 1)
    return prompt_attention(q, gates_t, att["kaug"], att["vst"], att["kwin"], att["vwt"], att["ck"],
                            att["cvt"], att["bias_cmp"], att["bias_near"], att["bias_far"])


def _sample_attention_setup(rows6, caches, P, W, bias_dist):
    db, ts = rows6.shape[:2]
    pt = caches["page_table"]
    n_pages = pt.shape[1]
    page = caches["cache_kv_cmp"].shape[1]
    past = n_pages * page
    c = 2 * KV_HEADS * HEAD_DIM
    sel_cache = caches["cache_kv_sel"].reshape(-1, page, c)
    nc = past // CMP_BLOCK
    assert (past + ts) // CMP_BLOCK == nc and past % KEY_CHUNK == 0
    cmp = compress_paged(caches["cache_kv_cmp"].reshape(-1, page, c), pt, *_compress_weights(P, W))
    ck = cmp[:, 0].transpose(0, 2, 1, 3).reshape(db, nc, KV_HEADS * HEAD_DIM).astype(BF16)
    cv = cmp[:, 1].transpose(0, 2, 1, 3).reshape(db, nc, KV_HEADS * HEAD_DIM).astype(BF16)
    pad_new = lambda a: jnp.pad(a.reshape(db, ts, c), ((0, 0), (0, NEW_PAD - ts), (0, 0)))
    nsel = pad_new(rows6[:, :, 1])
    nwin = pad_new(rows6[:, :, 2])
    kwin = caches["cache_kv_win"].reshape(db, -1, c)
    wlen = kwin.shape[1]
    rows = GROUP * KV_HEADS * ts
    def by_row(f):
        per_q = jnp.stack([f(q) for q in range(ts)], axis=0)
        per_q = per_q.reshape(ts, KV_HEADS, GROUP, -1).transpose(2, 1, 0, 3)
        return per_q.reshape(rows, -1)

    bsel = by_row(lambda q: bias_dist.falling(past + q, past + NEW_PAD))
    bwin = by_row(lambda q: jnp.concatenate([bias_dist.falling(wlen + q, wlen),
                                             bias_dist.falling(q, NEW_PAD)], axis=1))
    bcmp = by_row(lambda q: bias_dist.falling(past + q - (CMP_BLOCK - 1), nc, step=CMP_BLOCK))
    ratio = SEL_BLOCK // CMP_BLOCK
    lane_blk = np.where(np.arange(nc) % ratio == 0, np.arange(nc) // ratio, -1)
    e_mat = jnp.asarray(lane_blk[:, None] == (np.arange(past) // SEL_BLOCK)[None, :], BF16)
    return dict(page_table=pt, sel_cache=sel_cache, nsel=nsel, kwin=kwin, nwin=nwin, ck=ck, cv=cv, bsel=bsel, bwin=bwin,
                bcmp=bcmp, e_mat=e_mat, past=past)


def _sample_attention_layer(q, gates, att):
    db, ts, _ = q.shape
    rows = GROUP * KV_HEADS * ts
    q5 = q.reshape(db, ts, KV_HEADS, GROUP, HEAD_DIM).transpose(0, 3, 2, 1, 4)
    eye = jnp.eye(KV_HEADS, dtype=q.dtype)
    q_bd = (q5[:, :, :, :, None, :] * eye[None, None, :, None, :, None]).reshape(db, rows, KV_HEADS * HEAD_DIM)
    g5 = gates[:, :, :3 * N_HEADS].reshape(db, ts, KV_HEADS, GROUP, 3).transpose(0, 3, 2, 1, 4)
    g_rows = jnp.pad(g5.reshape(db, rows, 3), ((0, 0), (0, 0), (0, LANES - 3)))
    o = sample_attention(att["page_table"], q_bd, g_rows, att["sel_cache"], att["nsel"], att["kwin"],
                         att["nwin"], att["ck"], att["cv"], att["bsel"], att["bwin"], att["bcmp"],
                         att["e_mat"], ts)
    o = o.reshape(db, GROUP, KV_HEADS, ts, HEAD_DIM).transpose(0, 3, 2, 1, 4)
    return o.reshape(db, ts, N_HEADS * HEAD_DIM).astype(BF16)


def kernel(x_prompt, x_sample, c_prompt, c_sample, cache_kv_cmp, cache_kv_sel, cache_kv_win, state_conv, page_table, w_ada, b_ada, norm_g, w_pw1, b_pw1, w_dw, b_dw, ln_g, ln_b, w_pw2, b_pw2, g_kv, w_kv, w_phi1, b_phi1, w_phi2, b_phi2, pe_cmp, w_qg, w_o, rel_table, w_gate, w_up, w_down, final_g):
    P = dict(w_ada=w_ada, b_ada=b_ada, norm_g=norm_g, w_pw1=w_pw1, b_pw1=b_pw1, w_dw=w_dw, b_dw=b_dw,
             ln_g=ln_g, ln_b=ln_b, w_pw2=w_pw2, b_pw2=b_pw2, g_kv=g_kv, w_kv=w_kv, w_phi1=w_phi1,
             b_phi1=b_phi1, w_phi2=w_phi2, b_phi2=b_phi2, pe_cmp=pe_cmp, w_qg=w_qg, w_o=w_o,
             rel_table=rel_table, w_gate=w_gate, w_up=w_up, w_down=w_down, final_g=final_g)
    W = _prep_weights(P)
    bp, tp, d = x_prompt.shape
    db, ts, _ = x_sample.shape
    depth = w_ada.shape[0]

    n_c = bp + db
    r_pad = -(-n_c // 8) * 8
    c_all = jnp.pad(jnp.concatenate([c_prompt, c_sample], axis=0), ((0, r_pad - n_c), (0, 0)))
    mod = ada_modulation(c_all, w_ada.reshape(depth * 2, d, 3 * d), b_ada.reshape(depth * 2, 1, 3 * d))
    mod_rows = mod.reshape(depth * 2, r_pad, 3, 1, d)
    mod_tok = jnp.repeat(mod[:, bp:n_c], ts, axis=1)
    mods_p, mods_s = {}, {}
    for l in range(depth):
        for j in range(2):
            lj = l * 2 + j
            mods_p[(l, j)] = tuple(
                _Slice(mod_rows, (None, None, None, 1, d), lambda grp, lj=lj, k=k: (lj, grp, k, 0, 0))
                for k in range(3))
            mods_s[(l, j)] = tuple(
                _Slice(mod_tok, (None, db * ts, d), lambda grp, lj=lj, k=k: (lj, 0, k)) for k in range(3)
            ) + (mod[lj, bp:n_c, 2 * d:3 * d][:, None, :],)

    past = page_table.shape[1] * cache_kv_cmp.shape[1]
    bias_dist = _BiasTable(rel_table, max(tp, past + 2 * NEW_PAD), pad=max(tp, NEW_PAD))
    caches = dict(cache_kv_cmp=cache_kv_cmp, cache_kv_sel=cache_kv_sel, cache_kv_win=cache_kv_win,
                  state_conv=state_conv, page_table=page_table)
    y_p, cmp_p, sel_p, win_p, conv_p = _trunk(x_prompt, mods_p, True, None, P, W, bias_dist)
    y_s, cmp_s, sel_s, win_s, conv_s = _trunk(x_sample, mods_s, False, caches, P, W, bias_dist)
    return (y_p, y_s, cmp_p, cmp_s, sel_p, sel_s, win_p, win_s, conv_p, conv_s)
```

```python
import functools
import math

import numpy as np
import jax
import jax.numpy as jnp
from jax import lax
from jax.experimental import pallas as pl
from jax.experimental.pallas import tpu as pltpu

F32 = jnp.float32
BF16 = jnp.bfloat16

N_HEADS = 16
KV_HEADS = 4
GROUP = N_HEADS // KV_HEADS
HEAD_DIM = 64
CONV_W = 31
CONV_PAD = 32
SUBLANES = 8
LANES = 128
PROMPT_ROW_TILE = 256
CMP_BLOCK = 32
SEL_BLOCK = 64
N_SEL = 16
WINDOW = 512
N_BUCKETS = 32
MAX_DISTANCE = 1024
EPS = 1e-6
FORCE = 1e4
MASK_NEG = -1e9
M_INIT = -1e30
QT = LANES
NEAR_TILES = 8
KEY_STEP = 512
NEAR_SPAN = NEAR_TILES + KEY_STEP // QT - 1
KV_PER_STEP = 2
V7X_VMEM_LIMIT_BYTES = 56 * 1024 * 1024


def _cparams(*sem):
    return pltpu.CompilerParams(dimension_semantics=sem, vmem_limit_bytes=V7X_VMEM_LIMIT_BYTES)


class _Slice:
    def __init__(self, arr, block, index):
        self.arr, self.block, self.index = arr, block, index
        self.shape = tuple(b for b in block if b is not None)

    def spec(self, tiles_per_group=1):
        index = self.index
        return pl.BlockSpec(self.block, lambda i, *_: index(i // tiles_per_group))


def _layer(arr, l):
    nd = arr.ndim
    return _Slice(arr, (None,) + arr.shape[1:], lambda _: (l,) + (0,) * (nd - 1))


def _arr(a):
    return a.arr if isinstance(a, _Slice) else a


def _full_spec(a):
    if isinstance(a, _Slice):
        return a.spec()
    nd = a.ndim
    return pl.BlockSpec(a.shape, lambda *_: (0,) * nd)


def _silu(x):
    return x * jax.nn.sigmoid(x)


def _norm_mod(x, g, shift, scale):
    ms = jnp.mean(x * x, axis=-1, keepdims=True)
    return x * lax.rsqrt(ms + EPS) * g * (1.0 + scale) + shift


def _dot(a, b):
    return jnp.dot(a, b, preferred_element_type=F32)


def _dot_nt(a, b):
    return lax.dot_general(a, b, (((1,), (1,)), ((), ())), preferred_element_type=F32)


def _ada_kernel(c_ref, w_ref, b_ref, o_ref):
    c = c_ref[...]
    o_ref[0] = _dot(_silu(c), w_ref[0]) + b_ref[0]


def ada_modulation(c_all, w_ada, b_ada):
    nl, d, n3 = w_ada.shape
    r = c_all.shape[0]
    tn = n3 // 2
    return pl.pallas_call(
        _ada_kernel,
        grid=(nl, n3 // tn),
        in_specs=[pl.BlockSpec((r, d), lambda l, j: (0, 0)),
                  pl.BlockSpec((1, d, tn), lambda l, j: (l, 0, j)),
                  pl.BlockSpec((1, 1, tn), lambda l, j: (l, 0, j))],
        out_specs=pl.BlockSpec((1, r, tn), lambda l, j: (l, 0, j)),
        out_shape=jax.ShapeDtypeStruct((nl, r, n3), F32),
        compiler_params=_cparams("parallel", "parallel"),
        name="ada_modulation",
    )(c_all, w_ada, b_ada)


def _row_spec(tm, width):
    return pl.BlockSpec((tm, width), lambda i: (i, 0))


def _mod_spec(mod, tiles_per_group):
    return mod.spec(tiles_per_group)


def _pw1_kernel(x_ref, g_ref, sh_ref, sc_ref, w_ref, b_ref, u_ref):
    d = x_ref.shape[1]
    h = _norm_mod(x_ref[...], g_ref[...], sh_ref[...], sc_ref[...]).astype(BF16)
    z = _dot(h, w_ref[...]) + b_ref[...]
    u_ref[...] = z[:, :d] * jax.nn.sigmoid(z[:, d:])


def pw1_glu(x, g, shift, scale, w, b, tm, tpg):
    m, d = x.shape
    return pl.pallas_call(
        _pw1_kernel,
        grid=(m // tm,),
        in_specs=[_row_spec(tm, d), _full_spec(g), _mod_spec(shift, tpg), _mod_spec(scale, tpg),
                  _full_spec(w), _full_spec(b)],
        out_specs=_row_spec(tm, d),
        out_shape=jax.ShapeDtypeStruct((m, d), F32),
        compiler_params=_cparams("parallel"),
        name="pw1_glu",
    )(*map(_arr, (x, g, shift, scale, w, b)))


def _conv_tail(rows_at, wdw_ref, bdw_ref, lg_ref, lb_ref, w2_ref, b2_ref):
    acc = None
    for k in range(CONV_W):
        term = rows_at(k + CONV_PAD - CONV_W + 1) * wdw_ref[k:k + 1, :]
        acc = term if acc is None else acc + term
    acc = acc + bdw_ref[...]
    mu = jnp.mean(acc, axis=-1, keepdims=True)
    cen = acc - mu
    var = jnp.mean(cen * cen, axis=-1, keepdims=True)
    yn = cen * lax.rsqrt(var + EPS) * lg_ref[...] + lb_ref[...]
    return _dot(_silu(yn).astype(BF16), w2_ref[...]) + b2_ref[...]


def _conv_kernel(halo_ref, u_ref, wdw_ref, bdw_ref, lg_ref, lb_ref, w2_ref, b2_ref, x_ref, gate_ref,
                 o_ref, buf, shifted, *, tiles_per_seq):
    tm = x_ref.shape[0]
    first = pl.program_id(0) % tiles_per_seq == 0
    buf[0:CONV_PAD, :] = jnp.where(first, 0.0, halo_ref[0])
    buf[CONV_PAD:CONV_PAD + tm, :] = u_ref[0]
    span = tm + CONV_PAD - SUBLANES
    for s in range(1, SUBLANES):
        shifted[s - 1] = buf[pl.ds(s, span), :]

    def rows_at(o):
        a, s = divmod(o, SUBLANES)
        if s == 0:
            return buf[pl.ds(o, tm), :]
        return shifted[s - 1, pl.ds(a * SUBLANES, tm), :]

    out = _conv_tail(rows_at, wdw_ref, bdw_ref, lg_ref, lb_ref, w2_ref, b2_ref)
    o_ref[...] = x_ref[...] + gate_ref[...] * out


def conv_pw2_residual(u, wdw, bdw, lg, lb, w2, b2, x, gate, tm, tpg):
    m, d = x.shape
    bsz, t, _ = u.shape
    assert t % tm == 0 and tm % CONV_PAD == 0 and m == bsz * t
    tps = t // tm
    return pl.pallas_call(
        functools.partial(_conv_kernel, tiles_per_seq=tps),
        grid=(m // tm,),
        in_specs=[pl.BlockSpec((1, CONV_PAD, d),
                               lambda i: (i // tps, jnp.maximum((i % tps) * (tm // CONV_PAD) - 1, 0), 0)),
                  pl.BlockSpec((1, tm, d), lambda i: (i // tps, i % tps, 0)),
                  _full_spec(wdw), _full_spec(bdw), _full_spec(lg), _full_spec(lb),
                  _full_spec(w2), _full_spec(b2), _row_spec(tm, d), _mod_spec(gate, tpg)],
        out_specs=_row_spec(tm, d),
        out_shape=jax.ShapeDtypeStruct((m, d), F32),
        scratch_shapes=[pltpu.VMEM((tm + CONV_PAD, d), F32),
                        pltpu.VMEM((SUBLANES - 1, tm + CONV_PAD - SUBLANES, d), F32)],
        compiler_params=_cparams("parallel"),
        name="conv_pw2_residual",
    )(*map(_arr, (u, u, wdw, bdw, lg, lb, w2, b2, x, gate)))


def _conv_small_kernel(full_ref, wdw_ref, bdw_ref, lg_ref, lb_ref, w2_ref, b2_ref, x_ref, gate_ref,
                       o_ref):
    tm = x_ref.shape[1]
    rows_at = lambda o: full_ref[0, pl.ds(o, tm), :]
    out = _conv_tail(rows_at, wdw_ref, bdw_ref, lg_ref, lb_ref, w2_ref, b2_ref)
    o_ref[0] = x_ref[0] + gate_ref[0] * out


def conv_pw2_residual_small(full, wdw, bdw, lg, lb, w2, b2, x, gate):
    bsz, ts, d = x.shape
    seq = lambda a: pl.BlockSpec((1,) + a.shape[1:], lambda i: (i, 0, 0))
    return pl.pallas_call(
        _conv_small_kernel,
        grid=(bsz,),
        in_specs=[seq(full), _full_spec(wdw), _full_spec(bdw), _full_spec(lg), _full_spec(lb),
                  _full_spec(w2), _full_spec(b2), seq(x), seq(gate)],
        out_specs=seq(x),
        out_shape=jax.ShapeDtypeStruct(x.shape, F32),
        compiler_params=_cparams("parallel"),
        name="conv_pw2_residual_small",
    )(*map(_arr, (full, wdw, bdw, lg, lb, w2, b2, x, gate)))


def _ffn_kernel(x_ref, *refs, attn, kv, final):
    x = x_ref[...]
    if attn:
        att_ref, wo_ref, ga_ref = refs[:3]
        refs = refs[3:]
        x = x + ga_ref[...] * _dot(att_ref[...], wo_ref[...])
    g_ref, sh_ref, sc_ref, gate_ref, wg_ref, wu_ref, wd_ref = refs[:7]
    h = _norm_mod(x, g_ref[...], sh_ref[...], sc_ref[...]).astype(BF16)
    a = _dot(h, wg_ref[...])
    b = _dot(h, wu_ref[...])
    act = (_silu(a) * b).astype(BF16)
    xn = x + gate_ref[...] * _dot(act, wd_ref[...])
    refs = refs[7:]
    if kv:
        gkv_ref, wkv_ref = refs[:2]
        refs = refs[2:]
        rows_ref = refs[-1]
        ms = jnp.mean(xn * xn, axis=-1, keepdims=True)
        rows_ref[...] = _dot((xn * lax.rsqrt(ms + EPS) * gkv_ref[...]).astype(BF16), wkv_ref[...])
    if final:
        fg_ref, o_ref = refs[:2]
        ms = jnp.mean(xn * xn, axis=-1, keepdims=True)
        o_ref[...] = xn * lax.rsqrt(ms + EPS) * fg_ref[...]
    else:
        o_ref = refs[0]
        o_ref[...] = xn


def ffn(x, g, shift, scale, gate, wg, wu, wd, tm, tpg, attn=None, kv=None, final_g=None):
    m, d = x.shape
    final = final_g is not None
    ins = [x]
    specs = [_row_spec(tm, d)]
    if attn is not None:
        o, w_o, gate_a = attn
        ins += [o, w_o, gate_a]
        specs += [_row_spec(tm, o.shape[1]), _full_spec(w_o), _mod_spec(gate_a, tpg)]
    ins += [g, shift, scale, gate, wg, wu, wd]
    specs += [_full_spec(g), _mod_spec(shift, tpg), _mod_spec(scale, tpg), _mod_spec(gate, tpg),
              _full_spec(wg), _full_spec(wu), _full_spec(wd)]
    out_specs = [_row_spec(tm, d)]
    out_shape = [jax.ShapeDtypeStruct((m, d), F32)]
    if kv is not None:
        ins += list(kv)
        specs += [_full_spec(kv[0]), _full_spec(kv[1])]
        out_specs.append(_row_spec(tm, kv[1].shape[1]))
        out_shape.append(jax.ShapeDtypeStruct((m, kv[1].shape[1]), F32))
    if final:
        ins.append(final_g)
        specs.append(_full_spec(final_g))
    outs = pl.pallas_call(
        functools.partial(_ffn_kernel, attn=attn is not None, kv=kv is not None, final=final),
        grid=(m // tm,),
        in_specs=specs,
        out_specs=out_specs,
        out_shape=out_shape,
        compiler_params=_cparams("parallel"),
        name="ffn_final" if final else "ffn",
    )(*map(_arr, ins))
    return outs if kv is not None else outs[0]


def _qg_kernel(x_ref, g_ref, sh_ref, sc_ref, w_ref, q_ref, gates_ref):
    nq = q_ref.shape[1]
    h = _norm_mod(x_ref[...], g_ref[...], sh_ref[...], sc_ref[...]).astype(BF16)
    z = _dot(h, w_ref[...])
    q_ref[...] = (z[:, :nq] * (HEAD_DIM ** -0.5)).astype(BF16)
    gates_ref[...] = jax.nn.sigmoid(z[:, nq:])


def qg_proj(x, g, shift, scale, w, tm, tpg):
    m, d = x.shape
    nq = N_HEADS * HEAD_DIM
    ng = w.shape[1] - nq
    return pl.pallas_call(
        _qg_kernel,
        grid=(m // tm,),
        in_specs=[_row_spec(tm, d), _full_spec(g), _mod_spec(shift, tpg), _mod_spec(scale, tpg),
                  _full_spec(w)],
        out_specs=[_row_spec(tm, nq), _row_spec(tm, ng)],
        out_shape=[jax.ShapeDtypeStruct((m, nq), BF16), jax.ShapeDtypeStruct((m, ng), F32)],
        compiler_params=_cparams("parallel"),
        name="qg_proj",
    )(*map(_arr, (x, g, shift, scale, w)))


SLAB = LANES
HEADS_PER_SLAB = SLAB // HEAD_DIM


R_PER_DOT = 2
CMP_PITCH = 40


def _compress_core(rows_at, pe_row, w1_at, b1, w2, b2):
    hidden = w2.shape[0]
    acc = None
    for i in range(CMP_BLOCK // R_PER_DOT):
        parts = [(rows_at(R_PER_DOT * i + j) + pe_row(R_PER_DOT * i + j)).astype(BF16) for j in range(R_PER_DOT)]
        term = _dot(jnp.concatenate(parts, axis=1), w1_at(i))
        acc = term if acc is None else acc + term
    hid = _silu(acc + b1)
    return [_dot(hid[:, hh * hidden:(hh + 1) * hidden].astype(BF16), w2) + b2 for hh in range(HEADS_PER_SLAB)]


def _compress_kernel(x_ref, pe_ref, w1_ref, b1_ref, w2_ref, b2_ref, o_ref, *, nblk):
    outs = _compress_core(lambda r: x_ref[0, pl.ds(r, nblk, stride=CMP_BLOCK), :],
                          lambda r: pe_ref[0, r:r + 1, :], lambda i: w1_ref[0, i],
                          b1_ref[0], w2_ref[0], b2_ref[0])
    for hh, out in enumerate(outs):
        o_ref[0, 0, hh] = out


def compress(x, nblk, pe_slab, w1_bd, b1_slab, w2, b2):
    bc, rows, _ = x.shape
    n_slab = 2 * KV_HEADS // HEADS_PER_SLAB
    slabs_per_e = n_slab // 2
    per_e = lambda a: pl.BlockSpec((1,) + a.shape[1:], lambda b, s: (s // slabs_per_e,) + (0,) * (a.ndim - 1))
    return pl.pallas_call(
        functools.partial(_compress_kernel, nblk=nblk),
        grid=(bc, n_slab),
        in_specs=[pl.BlockSpec((1, rows, SLAB), lambda b, s: (b, 0, s)),
                  pl.BlockSpec((1, CMP_BLOCK, SLAB), lambda b, s: (s, 0, 0)),
                  per_e(w1_bd), per_e(b1_slab), per_e(w2), per_e(b2)],
        out_specs=pl.BlockSpec((1, 1, HEADS_PER_SLAB, nblk, HEAD_DIM),
                               lambda b, s: (b, s // slabs_per_e, s % slabs_per_e, 0, 0)),
        out_shape=jax.ShapeDtypeStruct((bc, 2, KV_HEADS, nblk, HEAD_DIM), F32),
        compiler_params=_cparams("parallel", "parallel"),
        name="compress",
    )(x, pe_slab, w1_bd, b1_slab, w2, b2)


def _compress_paged_kernel(pt_ref, *refs, n_pages):
    page_refs = refs[:n_pages]
    pe_ref, w1_ref, b1_ref, w2_ref, b2_ref, o_ref, scr = refs[n_pages:]
    blocks_per_page = page_refs[0].shape[1] // CMP_BLOCK
    nblk = n_pages * blocks_per_page
    n_slab = pe_ref.shape[0]
    slabs_per_e = n_slab // 2
    for s in range(n_slab):
        e = s // slabs_per_e
        for p in range(n_pages):
            for j in range(blocks_per_page):
                base = (p * blocks_per_page + j) * CMP_PITCH
                scr[base:base + CMP_BLOCK, :] = page_refs[p][0, j * CMP_BLOCK:(j + 1) * CMP_BLOCK,
                                                             s * SLAB:(s + 1) * SLAB]
        outs = _compress_core(lambda r: scr[pl.ds(r, nblk, stride=CMP_PITCH), :],
                              lambda r: pe_ref[s, r:r + 1, :], lambda i: w1_ref[e, i],
                              b1_ref[e], w2_ref[e], b2_ref[e])
        for hh, out in enumerate(outs):
            o_ref[0, e, (s % slabs_per_e) * HEADS_PER_SLAB + hh] = out


def compress_paged(cache, page_table, pe_slab, w1_bd, b1_slab, w2, b2):
    n_pool, page, c = cache.shape
    db, n_pages = page_table.shape
    assert page % CMP_BLOCK == 0
    nblk = n_pages * page // CMP_BLOCK
    const = lambda a: pl.BlockSpec(a.shape, lambda b, pt: (0,) * a.ndim)

    def page_spec(k):
        return pl.BlockSpec((1, page, c), lambda b, pt: (pt[b, k], 0, 0))

    return pl.pallas_call(
        functools.partial(_compress_paged_kernel, n_pages=n_pages),
        grid_spec=pltpu.PrefetchScalarGridSpec(
            num_scalar_prefetch=1,
            grid=(db,),
            in_specs=[page_spec(k) for k in range(n_pages)]
            + [const(pe_slab), const(w1_bd), const(b1_slab), const(w2), const(b2)],
            out_specs=pl.BlockSpec((1, 2, KV_HEADS, nblk, HEAD_DIM), lambda b, pt: (b, 0, 0, 0, 0)),
            scratch_shapes=[pltpu.VMEM((nblk * CMP_PITCH, SLAB), F32)],
        ),
        out_shape=jax.ShapeDtypeStruct((db, 2, KV_HEADS, nblk, HEAD_DIM), F32),
        compiler_params=_cparams("parallel"),
        name="compress_paged",
    )(page_table, *([cache] * n_pages), pe_slab, w1_bd, b1_slab, w2, b2)


def _rel_bucket(dist):
    n = jnp.maximum(dist, 0)
    exact = N_BUCKETS // 2
    nf = jnp.maximum(n, 1).astype(F32)
    large = exact + (jnp.log(nf / exact) / math.log(MAX_DISTANCE / exact) * (N_BUCKETS - exact)).astype(jnp.int32)
    large = jnp.minimum(large, N_BUCKETS - 1)
    return jnp.where(n < exact, n, large)


class _BiasTable:
    def __init__(self, table, max_dist, pad):
        tab = table.astype(F32)

        def lookup(dist):
            bucket = _rel_bucket(dist)
            out = jnp.zeros((tab.shape[1], dist.shape[0]), F32)
            for k in range(N_BUCKETS):
                out = jnp.where(bucket[None, :] == k, tab[k][:, None], out)
            return out

        self.pad = pad
        self.n = max_dist
        self.ext = lookup(jnp.arange(-pad, max_dist))
        self.rev = lookup(max_dist - 1 - jnp.arange(pad + max_dist))

    def at(self, d):
        return self.ext[:, self.pad + d]

    def rising(self, d0, length):
        return lax.slice_in_dim(self.ext, self.pad + d0, self.pad + d0 + length, axis=1)

    def falling(self, d0, length, step=1):
        start = self.n - 1 - d0
        return lax.slice(self.rev, (0, start), (self.rev.shape[0], start + step * (length - 1) + 1), (1, step))

    def frames(self, d0, count, hop, length):
        reps = length // hop
        seq = self.rising(d0, hop * (count + reps - 1)).reshape(-1, count + reps - 1, hop)
        return jnp.concatenate([seq[:, a:a + count] for a in range(reps)], axis=-1)

    def toeplitz(self, d0, count, hop, size):
        assert hop == size
        period = 2 * size - 1
        base = self.rising(d0 - (size - 1), hop * (count + 1)).reshape(-1, count + 1, hop)
        w = jnp.concatenate([base[:, :count, size - 1:], base[:, 1:, :size - 1],
                             base[:, :count, :size - 1]], axis=-1)
        rows = jnp.tile(w, (1, 1, size))[:, :, :size * (period - 1)]
        return rows.reshape(w.shape[0], count, size, period - 1)[..., :size]


def _softmax_step(s, vt, m, l, acc):
    m_new = jnp.maximum(m, jnp.max(s, axis=0, keepdims=True))
    alpha = jnp.exp(m - m_new)
    p = jnp.exp(s - m_new)
    l = alpha * l + jnp.sum(p, axis=0, keepdims=True)
    acc = alpha * acc + _dot(vt, p.astype(BF16))
    return m_new, l, acc


def _attn_kernel(bfar_ref, q_ref, gt_ref, kaug_ref, vst_ref, kw_ref, vwt_ref, ck_ref, cvt_ref,
                 bc_ref, bn_ref, o_ref, imp_ref, sc_ref, s_ref, *, nb, nc):
    gp = pl.program_id(0)
    qt = pl.program_id(2)
    qs = qt * QT
    gq = GROUP * QT
    gd = GROUP * HEAD_DIM
    lane_cat = lambda parts: jnp.concatenate(parts, axis=1)
    col_i = lax.broadcasted_iota(jnp.int32, (1, gq), 1) % QT
    sub_per_chunk = KEY_STEP // QT
    row_k = lax.broadcasted_iota(jnp.int32, (KEY_STEP, gq), 0)
    last_chunk = qt // sub_per_chunk
    slots = range(KV_PER_STEP)

    def near_bias(hs, kt0, n_sub):
        parts = []
        for sub in range(n_sub):
            dm = jnp.clip(qt - (kt0 + sub), 0, NEAR_SPAN - 1)
            parts.append(lane_cat([bn_ref[hs * GROUP + h, dm] for h in range(GROUP)]))
        return jnp.concatenate(parts, axis=0)

    def prepare(hs):
        q_t = q_ref[0, :, hs * gd:(hs + 1) * gd].astype(F32).T.astype(BF16)
        q_heads = [q_t[h * HEAD_DIM:(h + 1) * HEAD_DIM] for h in range(GROUP)]
        q_plain = lane_cat(q_heads)

        s_c = _dot(ck_ref[0, hs], q_plain) + lane_cat([bc_ref[hs * GROUP + h, 0] for h in range(GROUP)])
        n_idx = lax.broadcasted_iota(jnp.int32, (nc, gq), 0)
        valid_c = (n_idx * CMP_BLOCK + CMP_BLOCK - 1) <= (qs + col_i)
        s_c = jnp.where(valid_c, s_c, M_INIT)
        m_c = jnp.max(s_c, axis=0, keepdims=True)
        p_c = jnp.where(valid_c, jnp.exp(s_c - m_c), 0.0)
        l_c = jnp.sum(p_c, axis=0, keepdims=True)
        p_c = p_c * jnp.where(l_c > 0.0, 1.0 / l_c, 0.0)
        o_c = _dot(cvt_ref[0, hs], p_c.astype(BF16))

        imp = p_c[:, 0:QT]
        for h in range(1, GROUP):
            imp = imp + p_c[:, h * QT:(h + 1) * QT]
        imp_ref[hs] = imp
        ratio = SEL_BLOCK // CMP_BLOCK
        imp2 = imp_ref[hs, pl.ds(0, nb, stride=ratio), :]
        for r in range(1, ratio):
            imp2 = imp2 + imp_ref[hs, pl.ds(r, nb, stride=ratio), :]
        j_idx = lax.broadcasted_iota(jnp.int32, (nb, QT), 0)
        i_idx = lax.broadcasted_iota(jnp.int32, (nb, QT), 1)
        jq = (qs + i_idx) // SEL_BLOCK
        forced = (j_idx == 0) | (j_idx == jq) | (j_idx == jq - 1)
        score = jnp.where(j_idx <= jq, jnp.where(forced, FORCE, imp2), -1.0)
        sc_ref[hs] = score

        sub8 = lax.broadcasted_iota(jnp.int32, (8, QT), 0)
        groups = [score[8 * r:8 * r + 8] for r in range(nb // 8)]
        counts = [jnp.zeros((8, QT), F32) for _ in groups]
        for c in range(nb):
            row = sc_ref[hs, c:c + 1, :]
            for r, grp_score in enumerate(groups):
                if r > c // 8:
                    beats = row >= grp_score
                elif r < c // 8:
                    beats = row > grp_score
                else:
                    tie = jnp.where(row == grp_score, 1.0, 0.0) * jnp.where(sub8 > c % 8, 1.0, 0.0)
                    counts[r] = counts[r] + tie
                    beats = row > grp_score
                counts[r] = counts[r] + jnp.where(beats, 1.0, 0.0)
        cnt = jnp.concatenate(counts, axis=0)
        selected = (cnt < float(min(N_SEL, nb))) & (score >= 0.0)
        mask_feat = jnp.where(selected, 0.0, MASK_NEG).astype(BF16)
        q_aug = lane_cat([jnp.concatenate([qh, mask_feat], axis=0) for qh in q_heads])
        far_row = lane_cat([jnp.full((1, QT), bfar_ref[(gp * KV_PER_STEP + hs) * GROUP + h], F32)
                            for h in range(GROUP)])

        wt0 = jnp.maximum(qt - WINDOW // QT, 0)
        ws = pl.multiple_of(wt0 * QT, QT)
        wk = WINDOW + QT
        s_w = _dot(kw_ref[0, hs, pl.ds(ws, wk), :], q_plain) + near_bias(hs, wt0, wk // QT)
        dist = (qs + col_i) - (ws + lax.broadcasted_iota(jnp.int32, (wk, gq), 0))
        s_w = jnp.where((dist >= 0) & (dist < WINDOW), s_w, MASK_NEG)
        p_w = jnp.exp(s_w - jnp.max(s_w, axis=0, keepdims=True))
        l_w = jnp.sum(p_w, axis=0, keepdims=True)
        o_w = _dot(vwt_ref[0, hs, :, pl.ds(ws, wk)], p_w.astype(BF16)) / l_w
        return q_aug, far_row, o_c, o_w

    prepared = [prepare(hs) for hs in slots]

    def raw_scores(hs, c):
        ks = pl.multiple_of(jnp.minimum(c, last_chunk) * KEY_STEP, KEY_STEP)
        return _dot(kaug_ref[0, hs, pl.ds(ks, KEY_STEP), :], prepared[hs][0])

    def pipelined(finish, hs):
        def body(c, carry):
            cur = s_ref[hs, c % 2]
            s_ref[hs, (c + 1) % 2] = raw_scores(hs, c + 1)
            ks = pl.multiple_of(c * KEY_STEP, KEY_STEP)
            return _softmax_step(finish(hs, c, cur), vst_ref[0, hs, :, pl.ds(ks, KEY_STEP)], *carry)
        return body

    def far_finish(hs, c, s):
        return s + prepared[hs][1]

    def near_finish(hs, c, s):
        return s + near_bias(hs, c * sub_per_chunk, sub_per_chunk)

    init = (jnp.full((1, gq), M_INIT, F32), jnp.zeros((1, gq), F32), jnp.zeros((HEAD_DIM, gq), F32))
    for hs in slots:
        s_ref[hs, 0] = raw_scores(hs, 0)
    n_far = jnp.maximum(qt - (NEAR_TILES - 1), 0) // sub_per_chunk
    carry = []
    for hs in slots:
        part = lax.fori_loop(0, n_far, pipelined(far_finish, hs), init)
        carry.append(lax.fori_loop(n_far, last_chunk, pipelined(near_finish, hs), part))
    ks = pl.multiple_of(last_chunk * KEY_STEP, KEY_STEP)
    for hs in slots:
        _, _, o_c, o_w = prepared[hs]
        s = near_finish(hs, last_chunk, s_ref[hs, last_chunk % 2])
        s = jnp.where(ks + row_k <= qs + col_i, s, MASK_NEG)
        m_s, l_s, acc_s = _softmax_step(s, vst_ref[0, hs, :, pl.ds(ks, KEY_STEP)], *carry[hs])
        o_s = acc_s / l_s
        gate = lambda br: lane_cat([gt_ref[0, hs, h * 3 + br:h * 3 + br + 1, :] for h in range(GROUP)])
        o_t = gate(0) * o_c + gate(1) * o_s + gate(2) * o_w
        o_rows = jnp.concatenate([o_t[:, h * QT:(h + 1) * QT] for h in range(GROUP)], axis=0)
        o_ref[0, :, hs * gd:(hs + 1) * gd] = o_rows.T.astype(BF16)


def prompt_attention(q, gates_t, kaug, vst, kwin, vwt, ck, cvt, bias_cmp, bias_near, bias_far):
    bsz, t, dq = q.shape
    nb = t // SEL_BLOCK
    nc = t // CMP_BLOCK
    nq = t // QT
    gd = GROUP * HEAD_DIM
    kps = KV_PER_STEP
    per_bg = lambda a: pl.BlockSpec((1, kps) + a.shape[2:], lambda g, b, i, *_: (b, g, 0, 0))
    grid_spec = pltpu.PrefetchScalarGridSpec(
        num_scalar_prefetch=1,
        grid=(KV_HEADS // kps, bsz, nq),
        in_specs=[pl.BlockSpec((1, QT, kps * gd), lambda g, b, i, *_: (b, i, g)),
                  pl.BlockSpec((1, kps, 3 * GROUP, QT), lambda g, b, i, *_: (b, g, 0, i)),
                  per_bg(kaug), per_bg(vst), per_bg(kwin), per_bg(vwt), per_bg(ck), per_bg(cvt),
                  pl.BlockSpec((kps * GROUP, 1, nc, QT), lambda g, b, i, *_: (g, i, 0, 0)),
                  pl.BlockSpec((kps * GROUP, NEAR_SPAN, QT, QT), lambda g, b, i, *_: (g, 0, 0, 0))],
        out_specs=pl.BlockSpec((1, QT, kps * gd), lambda g, b, i, *_: (b, i, g)),
        scratch_shapes=[pltpu.VMEM((kps, nc, QT), F32), pltpu.VMEM((kps, nb, QT), F32),
                        pltpu.VMEM((kps, 2, KEY_STEP, GROUP * QT), F32)],
    )
    return pl.pallas_call(
        functools.partial(_attn_kernel, nb=nb, nc=nc),
        grid_spec=grid_spec,
        out_shape=jax.ShapeDtypeStruct((bsz, t, dq), BF16),
        compiler_params=_cparams("parallel", "parallel", "parallel"),
        name="prompt_attention",
    )(bias_far, q, gates_t, kaug, vst, kwin, vwt, ck, cvt, bias_cmp, bias_near)


KEY_CHUNK = 1024
NEW_PAD = LANES


def _diag_heads(full, row_g):
    out = None
    for g in range(KV_HEADS):
        part = jnp.where(row_g == g, full[:, g * HEAD_DIM:(g + 1) * HEAD_DIM], 0.0)
        out = part if out is None else out + part
    return out


def _sample_attn_kernel(pt_ref, q_ref, gates_ref, *refs, past, ts, n_pages):
    page_refs = refs[:n_pages]
    (nsel_ref, kwin_ref, nwin_ref, ck_ref, cv_ref, bsel_ref, bwin_ref, bcmp_ref, e_ref, o_ref,
     s_scr) = refs[n_pages:]
    kvd = KV_HEADS * HEAD_DIM
    page = page_refs[0].shape[1]
    pages_per_chunk = KEY_CHUNK // page

    def cached(c, lo, hi):
        parts = [page_refs[c * pages_per_chunk + j][0, :, lo:hi] for j in range(pages_per_chunk)]
        return jnp.concatenate(parts, axis=0).astype(BF16)

    rows = GROUP * KV_HEADS * ts
    nc = ck_ref.shape[1]
    q = q_ref[0]
    r_idx = lax.broadcasted_iota(jnp.int32, (rows, 1), 0)
    row_q = r_idx % ts
    row_g = (r_idx // ts) % KV_HEADS

    s_c = _dot_nt(q, ck_ref[0]) + bcmp_ref[...]
    p_c = jnp.exp(s_c - jnp.max(s_c, axis=1, keepdims=True))
    p_c = p_c / jnp.sum(p_c, axis=1, keepdims=True)
    o_c = _diag_heads(_dot(p_c.astype(BF16), cv_ref[0]), row_g)

    per = KV_HEADS * ts
    imp = p_c[0:per]
    for h in range(1, GROUP):
        imp = imp + p_c[h * per:(h + 1) * per]
    ratio = SEL_BLOCK // CMP_BLOCK
    imp2 = imp
    for r in range(1, ratio):
        imp2 = imp2 + pltpu.roll(imp, nc - r, axis=1)
    lane = lax.broadcasted_iota(jnp.int32, (per, nc), 1)
    blk = lane // ratio
    jq = past // SEL_BLOCK
    forced = (blk == 0) | (blk == jq - 1)
    score = jnp.where(lane % ratio == 0, jnp.where(forced, FORCE, imp2), -2.0)
    cnt = jnp.where(score < FORCE, 1.0, 0.0)
    for c in range(nc // ratio):
        col = score[:, c * ratio:c * ratio + 1]
        ge = jnp.where(col >= score, 1.0, 0.0)
        gt = jnp.where(col > score, 1.0, 0.0)
        cnt = cnt + jnp.where(lane > c * ratio, ge, gt)
    selected = (cnt < float(min(N_SEL, jq + 1))) & (score >= 0.0)
    mask_feat = jnp.where(selected, 0.0, MASK_NEG).astype(BF16)
    mask_feat = jnp.concatenate([mask_feat] * GROUP, axis=0)

    for c in range(past // KEY_CHUNK):
        sl = slice(c * KEY_CHUNK, (c + 1) * KEY_CHUNK)
        s_scr[:, sl] = _dot_nt(q, cached(c, 0, kvd)) + bsel_ref[:, sl] + _dot(mask_feat, e_ref[:, sl])
    new_j = lax.broadcasted_iota(jnp.int32, (rows, NEW_PAD), 1)
    new_ok = new_j <= row_q
    s_new = _dot_nt(q, nsel_ref[0, :, 0:kvd].astype(BF16)) + bsel_ref[:, past:past + NEW_PAD]
    s_scr[:, past:past + NEW_PAD] = jnp.where(new_ok, s_new, MASK_NEG)
    s_all = s_scr[...]
    p_s = jnp.exp(s_all - jnp.max(s_all, axis=1, keepdims=True))
    l_s = jnp.sum(p_s, axis=1, keepdims=True)
    s_scr[...] = p_s
    acc = _dot(s_scr[:, past:past + NEW_PAD].astype(BF16), nsel_ref[0, :, kvd:2 * kvd].astype(BF16))
    for c in range(past // KEY_CHUNK):
        sl = slice(c * KEY_CHUNK, (c + 1) * KEY_CHUNK)
        acc = acc + _dot(s_scr[:, sl].astype(BF16), cached(c, kvd, 2 * kvd))
    o_s = _diag_heads(acc, row_g) / l_s

    wlen = kwin_ref.shape[1]
    win_i = lax.broadcasted_iota(jnp.int32, (rows, wlen), 1)
    s_w1 = _dot_nt(q, kwin_ref[0, :, 0:kvd].astype(BF16)) + bwin_ref[:, 0:wlen]
    s_w1 = jnp.where(win_i + (WINDOW - wlen) > row_q, s_w1, MASK_NEG)
    s_w2 = _dot_nt(q, nwin_ref[0, :, 0:kvd].astype(BF16)) + bwin_ref[:, wlen:wlen + NEW_PAD]
    s_w2 = jnp.where(new_ok, s_w2, MASK_NEG)
    m_w = jnp.maximum(jnp.max(s_w1, axis=1, keepdims=True), jnp.max(s_w2, axis=1, keepdims=True))
    p_w1 = jnp.exp(s_w1 - m_w)
    p_w2 = jnp.exp(s_w2 - m_w)
    l_w = jnp.sum(p_w1, axis=1, keepdims=True) + jnp.sum(p_w2, axis=1, keepdims=True)
    acc_w = (_dot(p_w1.astype(BF16), kwin_ref[0, :, kvd:2 * kvd].astype(BF16))
             + _dot(p_w2.astype(BF16), nwin_ref[0, :, kvd:2 * kvd].astype(BF16)))
    o_w = _diag_heads(acc_w, row_g) / l_w

    gts = gates_ref[0]
    o_ref[0] = gts[:, 0:1] * o_c + gts[:, 1:2] * o_s + gts[:, 2:3] * o_w


def sample_attention(page_table, q_bd, gates, cache, nsel, kwin, nwin, ck, cv, bsel, bwin, bcmp, e_mat, ts):
    db, rows, kvd = q_bd.shape
    n_pages = page_table.shape[1]
    page = cache.shape[1]
    past = n_pages * page
    assert KEY_CHUNK % page == 0 and past % KEY_CHUNK == 0
    seq = lambda a: pl.BlockSpec((1,) + a.shape[1:], lambda b, pt: (b,) + (0,) * (a.ndim - 1))
    const = lambda a: pl.BlockSpec(a.shape, lambda b, pt: (0,) * a.ndim)

    def page_spec(k):
        return pl.BlockSpec((1,) + cache.shape[1:], lambda b, pt: (pt[b, k], 0, 0))

    return pl.pallas_call(
        functools.partial(_sample_attn_kernel, past=past, ts=ts, n_pages=n_pages),
        grid_spec=pltpu.PrefetchScalarGridSpec(
            num_scalar_prefetch=1,
            grid=(db,),
            in_specs=[seq(q_bd), seq(gates)] + [page_spec(k) for k in range(n_pages)]
            + [seq(nsel), seq(kwin), seq(nwin), seq(ck), seq(cv), const(bsel), const(bwin), const(bcmp),
               const(e_mat)],
            out_specs=pl.BlockSpec((1, rows, HEAD_DIM), lambda b, pt: (b, 0, 0)),
            scratch_shapes=[pltpu.VMEM((rows, past + NEW_PAD), F32)],
        ),
        out_shape=jax.ShapeDtypeStruct((db, rows, HEAD_DIM), F32),
        compiler_params=_cparams("parallel"),
        name="sample_attention",
    )(page_table, q_bd, gates, *([cache] * n_pages), nsel, kwin, nwin, ck, cv, bsel, bwin, bcmp, e_mat)


def _prep_weights(P):
    W = {}
    cast = lambda a: a.astype(BF16)
    for k in ("w_pw1", "w_pw2", "w_kv", "w_o", "w_gate", "w_up", "w_down", "w_phi1", "w_phi2"):
        W[k] = cast(P[k])
    nq = N_HEADS * HEAD_DIM
    wqg = P["w_qg"]
    pad = LANES - (wqg.shape[2] - nq)
    W["w_qg"] = cast(jnp.pad(wqg, ((0, 0), (0, 0), (0, pad))))
    rows = lambda a: a.reshape(-1, 1, a.shape[-1])
    for k in ("norm_g", "b_pw1", "b_dw", "ln_g", "ln_b", "b_pw2"):
        W[k] = rows(P[k])
    W["w_dw"] = jnp.pad(P["w_dw"], ((0, 0), (0, CONV_PAD - CONV_W), (0, 0)))
    return W


def _trunk(x, mods, is_prompt, caches, P, W, bias_dist):
    bsz, t, d = x.shape
    m = bsz * t
    depth = P["w_gate"].shape[0]
    n_a = P["w_pw1"].shape[0]
    if is_prompt:
        tm, tpg = PROMPT_ROW_TILE, t // PROMPT_ROW_TILE
    else:
        tm, tpg = m, 1
    x2 = x.reshape(m, d)
    row2 = lambda a: a.reshape(1, -1)
    conv_states = []
    y = None
    for l in range(depth):
        shift, scale, gate = mods[(l, 0)][:3]
        g0 = _layer(W["norm_g"], 2 * l)
        attn = None
        if l < n_a:
            u = pw1_glu(x2, g0, shift, scale, _layer(W["w_pw1"], l), _layer(W["b_pw1"], l), tm, tpg)
            u3 = u.reshape(bsz, t, d)
            args = tuple(_layer(W[k], l) for k in ("w_dw", "b_dw", "ln_g", "ln_b", "w_pw2", "b_pw2"))
            if is_prompt:
                conv_states.append(u3[:, -(CONV_W - 1):])
                x2 = conv_pw2_residual(u3, *args, x2, gate, tm, tpg)
            else:
                hist = jnp.pad(caches["state_conv"][l], ((0, 0), (CONV_PAD - CONV_W + 1, 0), (0, 0)))
                full = jnp.concatenate([hist, u3], axis=1)
                conv_states.append(full[:, -(CONV_W - 1):])
                x2 = conv_pw2_residual_small(full, *args, x2.reshape(bsz, t, d),
                                             mods[(l, 0)][3]).reshape(m, d)
        else:
            if l == n_a:
                rows6 = rows.reshape(bsz, t, 3, 2, KV_HEADS, HEAD_DIM)
                if is_prompt:
                    att = _prompt_attention_setup(rows6, P, W, bias_dist)
                    win_state = rows6[:, -min(WINDOW, t):, 2]
                else:
                    att = _sample_attention_setup(rows6, caches, P, W, bias_dist)
                    wl = caches["cache_kv_win"].shape[1]
                    win_state = jnp.concatenate([caches["cache_kv_win"], rows6[:, :, 2]], axis=1)[:, -wl:]
            lb = l - n_a
            q, gates = qg_proj(x2, g0, shift, scale, _layer(W["w_qg"], lb), tm, tpg)
            if is_prompt:
                o = _prompt_attention_layer(q.reshape(bsz, t, -1), gates.reshape(bsz, t, -1), att)
            else:
                o = _sample_attention_layer(q.reshape(bsz, t, -1), gates.reshape(bsz, t, -1), att)
            attn = (o.reshape(m, -1), _layer(W["w_o"], lb), gate)
        shift, scale, gate = mods[(l, 1)][:3]
        fg = row2(P["final_g"]) if l == depth - 1 else None
        kv = (row2(P["g_kv"]), W["w_kv"]) if l == n_a - 1 else None
        out = ffn(x2, _layer(W["norm_g"], 2 * l + 1), shift, scale, gate, _layer(W["w_gate"], l),
                  _layer(W["w_up"], l), _layer(W["w_down"], l), tm, tpg, attn=attn, kv=kv, final_g=fg)
        if kv is not None:
            out, rows = out
        if l == depth - 1:
            y = out
        else:
            x2 = out
    return (y.reshape(bsz, t, d), rows6[:, :, 0], rows6[:, :, 1], win_state, jnp.stack(conv_states))


def _compress_weights(P, W):
    pe = P["pe_cmp"]
    pe_slab = jnp.repeat(jnp.tile(pe, (1, 1, HEADS_PER_SLAB)), KV_HEADS // HEADS_PER_SLAB, axis=0)
    w1 = W["w_phi1"].reshape(2, CMP_BLOCK, HEAD_DIM, -1)
    zero = jnp.zeros_like(w1)
    w1_bd = jnp.concatenate([jnp.concatenate([w1, zero], axis=-1), jnp.concatenate([zero, w1], axis=-1)],
                            axis=2)
    w1_bd = w1_bd.reshape(2, CMP_BLOCK // R_PER_DOT, R_PER_DOT * SLAB, -1)
    b1_slab = jnp.tile(P["b_phi1"], (1, HEADS_PER_SLAB))[:, None, :]
    return pe_slab, w1_bd, b1_slab, W["w_phi2"], P["b_phi2"][:, None, :]


def _prompt_attention_setup(rows6, P, W, bias_dist):
    bsz, t = rows6.shape[:2]
    nb = t // SEL_BLOCK
    nc = t // CMP_BLOCK
    nq = t // QT
    cmp = compress(rows6.reshape(bsz, t, -1), nc, *_compress_weights(P, W))
    ck = cmp[:, 0].astype(BF16)
    cvt = cmp[:, 1].transpose(0, 1, 3, 2).astype(BF16)
    onehot = (jnp.arange(t)[:, None] // SEL_BLOCK == jnp.arange(nb)[None, :]).astype(BF16)
    ksel = rows6[:, :, 1, 0].transpose(0, 2, 1, 3).astype(BF16)
    kaug = jnp.concatenate([ksel, jnp.broadcast_to(onehot, (bsz, KV_HEADS, t, nb))], axis=-1)
    vst = rows6[:, :, 1, 1].transpose(0, 2, 3, 1).astype(BF16)
    kwin = rows6[:, :, 2, 0].transpose(0, 2, 1, 3).astype(BF16)
    vwt = rows6[:, :, 2, 1].transpose(0, 2, 3, 1).astype(BF16)
    per_qt = QT // CMP_BLOCK
    n_u = nc + per_qt * (nq - 1)
    frames = bias_dist.frames(-(CMP_BLOCK - 1) - CMP_BLOCK * (nc - 1), n_u, CMP_BLOCK, QT)[:, ::-1]
    frames = lax.optimization_barrier(frames)
    bias_cmp = jnp.stack([frames[:, per_qt * (nq - 1 - i):per_qt * (nq - 1 - i) + nc] for i in range(nq)],
                         axis=1)
    bias_near = bias_dist.toeplitz(0, NEAR_SPAN, QT, QT)
    bias_far = bias_dist.at(NEAR_TILES * QT)
    return dict(kaug=kaug, vst=vst, kwin=kwin, vwt=vwt, ck=ck, cvt=cvt, bias_cmp=bias_cmp,
                bias_near=bias_near, bias_far=bias_far)


def _prompt_attention_layer(q, gates, att):
    bsz, t, _ = q.shape
    gates_t = gates[:, :, :3 * N_HEADS].reshape(bsz, t, KV_HEADS, 3 * GROUP).transpose(0, 2, 3, 1)
    return prompt_attention(q, gates_t, att["kaug"], att["vst"], att["kwin"], att["vwt"], att["ck"],
                            att["cvt"], att["bias_cmp"], att["bias_near"], att["bias_far"])


def _sample_attention_setup(rows6, caches, P, W, bias_dist):
    db, ts = rows6.shape[:2]
    pt = caches["page_table"]
    n_pages = pt.shape[1]
    page = caches["cache_kv_cmp"].shape[1]
    past = n_pages * page
    c = 2 * KV_HEADS * HEAD_DIM
    sel_cache = caches["cache_kv_sel"].reshape(-1, page, c)
    nc = past // CMP_BLOCK
    assert (past + ts) // CMP_BLOCK == nc and past % KEY_CHUNK == 0
    cmp = compress_paged(caches["cache_kv_cmp"].reshape(-1, page, c), pt, *_compress_weights(P, W))
    ck = cmp[:, 0].transpose(0, 2, 1, 3).reshape(db, nc, KV_HEADS * HEAD_DIM).astype(BF16)
    cv = cmp[:, 1].transpose(0, 2, 1, 3).reshape(db, nc, KV_HEADS * HEAD_DIM).astype(BF16)
    pad_new = lambda a: jnp.pad(a.reshape(db, ts, c), ((0, 0), (0, NEW_PAD - ts), (0, 0)))
    nsel = pad_new(rows6[:, :, 1])
    nwin = pad_new(rows6[:, :, 2])
    kwin = caches["cache_kv_win"].reshape(db, -1, c)
    wlen = kwin.shape[1]
    rows = GROUP * KV_HEADS * ts
    def by_row(f):
        per_q = jnp.stack([f(q) for q in range(ts)], axis=0)
        per_q = per_q.reshape(ts, KV_HEADS, GROUP, -1).transpose(2, 1, 0, 3)
        return per_q.reshape(rows, -1)

    bsel = by_row(lambda q: bias_dist.falling(past + q, past + NEW_PAD))
    bwin = by_row(lambda q: jnp.concatenate([bias_dist.falling(wlen + q, wlen),
                                             bias_dist.falling(q, NEW_PAD)], axis=1))
    bcmp = by_row(lambda q: bias_dist.falling(past + q - (CMP_BLOCK - 1), nc, step=CMP_BLOCK))
    ratio = SEL_BLOCK // CMP_BLOCK
    lane_blk = np.where(np.arange(nc) % ratio == 0, np.arange(nc) // ratio, -1)
    e_mat = jnp.asarray(lane_blk[:, None] == (np.arange(past) // SEL_BLOCK)[None, :], BF16)
    return dict(page_table=pt, sel_cache=sel_cache, nsel=nsel, kwin=kwin, nwin=nwin, ck=ck, cv=cv, bsel=bsel, bwin=bwin,
                bcmp=bcmp, e_mat=e_mat, past=past)


def _sample_attention_layer(q, gates, att):
    db, ts, _ = q.shape
    rows = GROUP * KV_HEADS * ts
    q5 = q.reshape(db, ts, KV_HEADS, GROUP, HEAD_DIM).transpose(0, 3, 2, 1, 4)
    eye = jnp.eye(KV_HEADS, dtype=q.dtype)
    q_bd = (q5[:, :, :, :, None, :] * eye[None, None, :, None, :, None]).reshape(db, rows, KV_HEADS * HEAD_DIM)
    g5 = gates[:, :, :3 * N_HEADS].reshape(db, ts, KV_HEADS, GROUP, 3).transpose(0, 3, 2, 1, 4)
    g_rows = jnp.pad(g5.reshape(db, rows, 3), ((0, 0), (0, 0), (0, LANES - 3)))
    o = sample_attention(att["page_table"], q_bd, g_rows, att["sel_cache"], att["nsel"], att["kwin"],
                         att["nwin"], att["ck"], att["cv"], att["bsel"], att["bwin"], att["bcmp"],
                         att["e_mat"], ts)
    o = o.reshape(db, GROUP, KV_HEADS, ts, HEAD_DIM).transpose(0, 3, 2, 1, 4)
    return o.reshape(db, ts, N_HEADS * HEAD_DIM).astype(BF16)


def kernel(x_prompt, x_sample, c_prompt, c_sample, cache_kv_cmp, cache_kv_sel, cache_kv_win, state_conv, page_table, w_ada, b_ada, norm_g, w_pw1, b_pw1, w_dw, b_dw, ln_g, ln_b, w_pw2, b_pw2, g_kv, w_kv, w_phi1, b_phi1, w_phi2, b_phi2, pe_cmp, w_qg, w_o, rel_table, w_gate, w_up, w_down, final_g):
    P = dict(w_ada=w_ada, b_ada=b_ada, norm_g=norm_g, w_pw1=w_pw1, b_pw1=b_pw1, w_dw=w_dw, b_dw=b_dw,
             ln_g=ln_g, ln_b=ln_b, w_pw2=w_pw2, b_pw2=b_pw2, g_kv=g_kv, w_kv=w_kv, w_phi1=w_phi1,
             b_phi1=b_phi1, w_phi2=w_phi2, b_phi2=b_phi2, pe_cmp=pe_cmp, w_qg=w_qg, w_o=w_o,
             rel_table=rel_table, w_gate=w_gate, w_up=w_up, w_down=w_down, final_g=final_g)
    W = _prep_weights(P)
    bp, tp, d = x_prompt.shape
    db, ts, _ = x_sample.shape
    depth = w_ada.shape[0]

    n_c = bp + db
    r_pad = -(-n_c // 8) * 8
    c_all = jnp.pad(jnp.concatenate([c_prompt, c_sample], axis=0), ((0, r_pad - n_c), (0, 0)))
    mod = ada_modulation(c_all, w_ada.reshape(depth * 2, d, 3 * d), b_ada.reshape(depth * 2, 1, 3 * d))
    mod_rows = mod.reshape(depth * 2, r_pad, 3, 1, d)
    mod_tok = jnp.repeat(mod[:, bp:n_c], ts, axis=1)
    mods_p, mods_s = {}, {}
    for l in range(depth):
        for j in range(2):
            lj = l * 2 + j
            mods_p[(l, j)] = tuple(
                _Slice(mod_rows, (None, None, None, 1, d), lambda grp, lj=lj, k=k: (lj, grp, k, 0, 0))
                for k in range(3))
            mods_s[(l, j)] = tuple(
                _Slice(mod_tok, (None, db * ts, d), lambda grp, lj=lj, k=k: (lj, 0, k)) for k in range(3)
            ) + (mod[lj, bp:n_c, 2 * d:3 * d][:, None, :],)

    past = page_table.shape[1] * cache_kv_cmp.shape[1]
    bias_dist = _BiasTable(rel_table, max(tp, past + 2 * NEW_PAD), pad=max(tp, NEW_PAD))
    caches = dict(cache_kv_cmp=cache_kv_cmp, cache_kv_sel=cache_kv_sel, cache_kv_win=cache_kv_win,
                  state_conv=state_conv, page_table=page_table)
    y_p, cmp_p, sel_p, win_p, conv_p = _trunk(x_prompt, mods_p, True, None, P, W, bias_dist)
    y_s, cmp_s, sel_s, win_s, conv_s = _trunk(x_sample, mods_s, False, caches, P, W, bias_dist)
    return (y_p, y_s, cmp_p, cmp_s, sel_p, sel_s, win_p, win_s, conv_p, conv_s)
```
